```python
import math
import jax, jax.numpy as jnp
from jax import lax
import numpy as np

D_MODEL = 2048
BATCH = 8
SEQ = 2048
DEPTH = 4

CHUNK = 64
EPS = 1e-6
N_EVEN = (DEPTH + 1) // 2
N_ODD = DEPTH // 2

GM_BLOCK = 128
GM_GROUPS = 8
GM_WIDTH = D_MODEL
GM_GROUP_DIM = GM_WIDTH // GM_GROUPS
SSM_D_INNER = D_MODEL
SSM_HEAD_DIM = 64
SSM_HEADS = SSM_D_INNER // SSM_HEAD_DIM
SSM_GROUPS = 4
SSM_HEADS_PER_GROUP = SSM_HEADS // SSM_GROUPS
SSM_STATE = 128
SSM_CONV = 4
SSM_CHUNK = CHUNK
SSM_CONV_DIM = SSM_D_INNER + 2 * SSM_GROUPS * SSM_STATE
EVEN_IN = 2 * GM_WIDTH + SSM_D_INNER + SSM_CONV_DIM + SSM_HEADS
EVEN_MIX = GM_WIDTH + SSM_D_INNER
MLA_HEADS = 16
MLA_Q_RANK = 512
MLA_KV_RANK = 512
MLA_NOPE = 128
MLA_ROPE = 64
MLA_V = 128
MLA_QK = MLA_NOPE + MLA_ROPE
ODD_IN = MLA_Q_RANK + MLA_KV_RANK + MLA_ROPE
ATTN_BLOCK = 128
ROPE_THETA = 10000.0
MAX_OFFSET_CHUNKS = 64
D_FF = 5632
FFN_CONV = 3

kernel_name = "hybrid_gmlp_ssd_mla_convffn"


def rmsnorm(x, w):
    xf = x.astype(jnp.float32)
    y = xf * lax.rsqrt(jnp.mean(xf * xf, -1, keepdims=True) + EPS)
    return (y * w.astype(jnp.float32)).astype(x.dtype)


def causal_dwconv(x, w, b):
    k = w.shape[0]
    c = x.shape[-1]
    y = lax.conv_general_dilated(
        x, w[:, None, :].astype(x.dtype), window_strides=(1,),
        padding=[(k - 1, 0)], dimension_numbers=("NWC", "WIO", "NWC"),
        feature_group_count=c)
    return y + b.astype(x.dtype)


def gmlp_sgu(u, v, ln_g, ln_b, w_s, b_s):
    bsz, s, _ = u.shape
    nb = s // GM_BLOCK
    vf = v.reshape(bsz, nb, GM_BLOCK, GM_GROUPS, GM_GROUP_DIM).astype(jnp.float32)
    mu = jnp.mean(vf, -1, keepdims=True)
    var = jnp.mean(jnp.square(vf - mu), -1, keepdims=True)
    vn = ((vf - mu) * lax.rsqrt(var + EPS) * ln_g + ln_b).astype(u.dtype)
    chunk_id = jnp.arange(GM_BLOCK) // CHUNK
    mask = chunk_id[:, None] >= chunk_id[None, :]
    ws = jnp.where(mask, w_s, jnp.zeros((), w_s.dtype)).astype(u.dtype)
    gate = jnp.einsum("gij,bnjgc->bnigc", ws, vn) + b_s.T[:, :, None].astype(u.dtype)
    return u * gate.reshape(bsz, s, GM_WIDTH)


def segsum(a):
    t = a.shape[-1]
    cs = jnp.cumsum(a, -1)
    d = cs[..., :, None] - cs[..., None, :]
    mask = jnp.tril(jnp.ones((t, t), dtype=bool))
    return jnp.where(mask, d, -jnp.inf)


def ssd_scan(x, dt, a, b, c):
    bsz, s = x.shape[:2]
    nc = s // SSM_CHUNK
    G, E, P, N, L = SSM_GROUPS, SSM_HEADS_PER_GROUP, SSM_HEAD_DIM, SSM_STATE, SSM_CHUNK
    xd = (x * dt[..., None]).reshape(bsz, nc, L, G, E, P)
    da = jnp.moveaxis((dt * a).reshape(bsz, nc, L, G, E), 2, -1)
    bc = b.reshape(bsz, nc, L, G, N)
    cc = c.reshape(bsz, nc, L, G, N)
    a_cum = jnp.cumsum(da, -1)
    decay = jnp.exp(segsum(da))
    cb = jnp.einsum("bclgn,bcsgn->bcgls", cc, bc)
    y_diag = jnp.einsum("bcgls,bcgels,bcsgep->bclgep", cb, decay, xd)
    decay_states = jnp.exp(a_cum[..., -1:] - a_cum)
    states = jnp.einsum("bclgn,bcgel,bclgep->bcgepn", bc, decay_states, xd)
    chunk_decay = jnp.exp(a_cum[..., -1])

    def step(h, inp):
        s_c, d_c = inp
        return d_c[..., None, None] * h + s_c, h

    h0 = jnp.zeros((bsz, G, E, P, N), jnp.float32)
    _, prev = lax.scan(step, h0, (jnp.moveaxis(states, 1, 0), jnp.moveaxis(chunk_decay, 1, 0)))
    prev = jnp.moveaxis(prev, 0, 1)
    y_off = jnp.einsum("bclgn,bcgepn,bcgel->bclgep", cc, prev, jnp.exp(a_cum))
    return (y_diag + y_off).reshape(bsz, s, SSM_HEADS, P)


def mamba_branch(z, xbc, dt_raw, conv_w, conv_b, dt_bias, a_log, d_skip, norm_w):
    bsz, s, _ = z.shape
    xbc = jax.nn.silu(causal_dwconv(xbc, conv_w, conv_b)).astype(jnp.float32)
    xs, bs, cs = jnp.split(xbc, [SSM_D_INNER, SSM_D_INNER + SSM_GROUPS * SSM_STATE], axis=-1)
    dt = jax.nn.softplus(dt_raw.astype(jnp.float32) + dt_bias.astype(jnp.float32))
    a = -jnp.exp(a_log.astype(jnp.float32))
    xh = xs.reshape(bsz, s, SSM_HEADS, SSM_HEAD_DIM)
    y = ssd_scan(xh, dt, a,
                 bs.reshape(bsz, s, SSM_GROUPS, SSM_STATE),
                 cs.reshape(bsz, s, SSM_GROUPS, SSM_STATE))
    y = y + d_skip.astype(jnp.float32)[:, None] * xh
    y = y.reshape(bsz, s, SSM_D_INNER) * jax.nn.silu(z.astype(jnp.float32))
    y = y.reshape(bsz, s, SSM_GROUPS, SSM_D_INNER // SSM_GROUPS)
    y = y * lax.rsqrt(jnp.mean(y * y, -1, keepdims=True) + EPS)
    y = y.reshape(bsz, s, SSM_D_INNER) * norm_w.astype(jnp.float32)
    return y.astype(z.dtype)


def even_mixer(h, w_in, gm_ln_g, gm_ln_b, gm_ws, gm_bs, conv_w, conv_b,
               dt_bias, a_log, d_skip, ssm_norm_w, w_out):
    proj = h @ w_in
    o1 = GM_WIDTH
    o2 = 2 * GM_WIDTH
    o3 = o2 + SSM_D_INNER
    o4 = o3 + SSM_CONV_DIM
    u, v, z, xbc, dt_raw = jnp.split(proj, [o1, o2, o3, o4], axis=-1)
    ya = gmlp_sgu(jax.nn.gelu(u), jax.nn.gelu(v), gm_ln_g, gm_ln_b, gm_ws, gm_bs)
    yb = mamba_branch(z, xbc, dt_raw, conv_w, conv_b, dt_bias, a_log, d_skip, ssm_norm_w)
    return jnp.concatenate([ya, yb], axis=-1) @ w_out


def rope(x, cos, sin):
    half = x.shape[-1] // 2
    x1, x2 = x[..., :half], x[..., half:]
    return jnp.concatenate([x1 * cos - x2 * sin, x1 * sin + x2 * cos], axis=-1)


def mla_mixer(h, cos, sin, w_in, q_norm_w, kv_norm_w, w_uq, w_ukv, w_o):
    bsz, s, _ = h.shape
    proj = h @ w_in
    cq, ckv, kr = jnp.split(proj, [MLA_Q_RANK, MLA_Q_RANK + MLA_KV_RANK], axis=-1)
    cq = rmsnorm(cq, q_norm_w)
    ckv = rmsnorm(ckv, kv_norm_w)
    q = (cq @ w_uq).reshape(bsz, s, MLA_HEADS, MLA_QK)
    q_nope = q[..., :MLA_NOPE]
    q_pe = rope(q[..., MLA_NOPE:], cos[:, :, None, :], sin[:, :, None, :])
    kv = (ckv @ w_ukv).reshape(bsz, s, MLA_HEADS, MLA_NOPE + MLA_V)
    k_nope, v = kv[..., :MLA_NOPE], kv[..., MLA_NOPE:]
    k_pe = rope(kr, cos, sin)
    scale = MLA_QK ** -0.5
    outs = []
    for i in range(s // ATTN_BLOCK):
        q0 = i * ATTN_BLOCK
        kend = q0 + ATTN_BLOCK
        sc = (jnp.einsum("bqhd,bkhd->bhqk", q_nope[:, q0:kend], k_nope[:, :kend])
              + jnp.einsum("bqhr,bkr->bhqk", q_pe[:, q0:kend], k_pe[:, :kend]))
        sc = sc.astype(jnp.float32) * scale
        qc = (q0 + jnp.arange(ATTN_BLOCK)) // CHUNK
        kc = jnp.arange(kend) // CHUNK
        sc = jnp.where(kc[None, :] <= qc[:, None], sc, -jnp.inf)
        p = jax.nn.softmax(sc, axis=-1).astype(v.dtype)
        outs.append(jnp.einsum("bhqk,bkhd->bqhd", p, v[:, :kend]))
    o = jnp.concatenate(outs, axis=1).reshape(bsz, s, MLA_HEADS * MLA_V)
    return o @ w_o


def conv_ffn(h, w_up, conv_w, conv_b, w_down):
    up = h @ w_up
    g, val = up[..., :D_FF], up[..., D_FF:]
    g = causal_dwconv(g, conv_w, conv_b)
    return (jax.nn.gelu(g) * val) @ w_down


def _fwd_setup_inputs(seed: int = 0) -> dict:
    key = jax.random.key(seed)
    ks = iter(jax.random.split(key, 40))

    def nrm(shape, scale):
        return jax.random.normal(next(ks), shape, jnp.float32) * scale

    def gain(shape):
        return 1.0 + nrm(shape, 0.02)

    x = jax.random.normal(next(ks), (BATCH, SEQ, D_MODEL), jnp.float32)
    offset = jax.random.randint(next(ks), (BATCH,), 0, MAX_OFFSET_CHUNKS) * CHUNK
    positions = (offset[:, None] + jnp.arange(SEQ)[None, :]).astype(jnp.int32)

    dt = jnp.exp(jax.random.uniform(next(ks), (N_EVEN, SSM_HEADS), jnp.float32)
                 * (math.log(0.1) - math.log(0.001)) + math.log(0.001))
    dt_bias = dt + jnp.log(-jnp.expm1(-dt))
    a_log = jnp.log(jax.random.uniform(next(ks), (N_EVEN, SSM_HEADS), jnp.float32, 1.0, 16.0))

    return {
        "x": x,
        "positions": positions,
        "norm_mix": gain((DEPTH, D_MODEL)),
        "norm_ffn": gain((DEPTH, D_MODEL)),
        "norm_final": gain((D_MODEL,)),
        "ev_w_in": nrm((N_EVEN, D_MODEL, EVEN_IN), D_MODEL ** -0.5),
        "ev_gm_ln_g": gain((N_EVEN, GM_GROUPS, GM_GROUP_DIM)),
        "ev_gm_ln_b": nrm((N_EVEN, GM_GROUPS, GM_GROUP_DIM), 0.02),
        "ev_gm_ws": nrm((N_EVEN, GM_GROUPS, GM_BLOCK, GM_BLOCK), GM_BLOCK ** -0.5),
        "ev_gm_bs": gain((N_EVEN, GM_GROUPS, GM_BLOCK)),
        "ev_conv_w": nrm((N_EVEN, SSM_CONV, SSM_CONV_DIM), SSM_CONV ** -0.5),
        "ev_conv_b": nrm((N_EVEN, SSM_CONV_DIM), 0.02),
        "ev_dt_bias": dt_bias,
        "ev_a_log": a_log,
        "ev_d_skip": gain((N_EVEN, SSM_HEADS)),
        "ev_ssm_norm_w": gain((N_EVEN, SSM_D_INNER)),
        "ev_w_out": nrm((N_EVEN, EVEN_MIX, D_MODEL), EVEN_MIX ** -0.5),
        "od_w_in": nrm((N_ODD, D_MODEL, ODD_IN), D_MODEL ** -0.5),
        "od_q_norm": gain((N_ODD, MLA_Q_RANK)),
        "od_kv_norm": gain((N_ODD, MLA_KV_RANK)),
        "od_w_uq": nrm((N_ODD, MLA_Q_RANK, MLA_HEADS * MLA_QK), MLA_Q_RANK ** -0.5),
        "od_w_ukv": nrm((N_ODD, MLA_KV_RANK, MLA_HEADS * (MLA_NOPE + MLA_V)), MLA_KV_RANK ** -0.5),
        "od_w_o": nrm((N_ODD, MLA_HEADS * MLA_V, D_MODEL), (MLA_HEADS * MLA_V) ** -0.5),
        "ff_w_up": nrm((DEPTH, D_MODEL, 2 * D_FF), D_MODEL ** -0.5),
        "ff_conv_w": nrm((DEPTH, FFN_CONV, D_FF), FFN_CONV ** -0.5),
        "ff_conv_b": nrm((DEPTH, D_FF), 0.02),
        "ff_w_down": nrm((DEPTH, D_FF, D_MODEL), D_FF ** -0.5),
    }


def _fwd_reference(x, positions, norm_mix, norm_ffn, norm_final,
              ev_w_in, ev_gm_ln_g, ev_gm_ln_b, ev_gm_ws, ev_gm_bs,
              ev_conv_w, ev_conv_b, ev_dt_bias, ev_a_log, ev_d_skip, ev_ssm_norm_w, ev_w_out,
              od_w_in, od_q_norm, od_kv_norm, od_w_uq, od_w_ukv, od_w_o,
              ff_w_up, ff_conv_w, ff_conv_b, ff_w_down):
    inv_freq = ROPE_THETA ** (-jnp.arange(0, MLA_ROPE, 2, dtype=jnp.float32) / MLA_ROPE)
    ang = positions.astype(jnp.float32)[..., None] * inv_freq
    cos = jnp.cos(ang).astype(x.dtype)
    sin = jnp.sin(ang).astype(x.dtype)

    h = x
    for layer in range(DEPTH):
        j = layer // 2
        hn = rmsnorm(h, norm_mix[layer])
        if layer % 2 == 0:
            mix = even_mixer(hn, ev_w_in[j], ev_gm_ln_g[j], ev_gm_ln_b[j], ev_gm_ws[j], ev_gm_bs[j],
                             ev_conv_w[j], ev_conv_b[j], ev_dt_bias[j], ev_a_log[j], ev_d_skip[j],
                             ev_ssm_norm_w[j], ev_w_out[j])
        else:
            mix = mla_mixer(hn, cos, sin, od_w_in[j], od_q_norm[j], od_kv_norm[j],
                            od_w_uq[j], od_w_ukv[j], od_w_o[j])
        h = h + mix
        h = h + conv_ffn(rmsnorm(h, norm_ffn[layer]), ff_w_up[layer], ff_conv_w[layer],
                         ff_conv_b[layer], ff_w_down[layer])
    return rmsnorm(h, norm_final)


import jax as _jax
import jax.numpy as _jnp

TWIN_FORMAT = 'train_step'
FWD_PARAMS = ['x', 'positions', 'norm_mix', 'norm_ffn', 'norm_final', 'ev_w_in', 'ev_gm_ln_g', 'ev_gm_ln_b', 'ev_gm_ws', 'ev_gm_bs', 'ev_conv_w', 'ev_conv_b', 'ev_dt_bias', 'ev_a_log', 'ev_d_skip', 'ev_ssm_norm_w', 'ev_w_out', 'od_w_in', 'od_q_norm', 'od_kv_norm', 'od_w_uq', 'od_w_ukv', 'od_w_o', 'ff_w_up', 'ff_conv_w', 'ff_conv_b', 'ff_w_down']
TWIN_WEIGHTS = ['norm_mix', 'norm_ffn', 'norm_final', 'ev_w_in', 'ev_gm_ln_g', 'ev_gm_ln_b', 'ev_gm_ws', 'ev_gm_bs', 'ev_conv_w', 'ev_conv_b', 'ev_dt_bias', 'ev_a_log', 'ev_d_skip', 'ev_ssm_norm_w', 'ev_w_out', 'od_w_in', 'od_q_norm', 'od_kv_norm', 'od_w_uq', 'od_w_ukv', 'od_w_o', 'ff_w_up', 'ff_conv_w', 'ff_conv_b', 'ff_w_down']
TWIN_DIFF_INPUT = 'x'
TWIN_INPUTS = ['x', 'positions', 'norm_mix', 'norm_ffn', 'norm_final', 'ev_w_in', 'ev_gm_ln_g', 'ev_gm_ln_b', 'ev_gm_ws', 'ev_gm_bs', 'ev_conv_w', 'ev_conv_b', 'ev_dt_bias', 'ev_a_log', 'ev_d_skip', 'ev_ssm_norm_w', 'ev_w_out', 'od_w_in', 'od_q_norm', 'od_kv_norm', 'od_w_uq', 'od_w_ukv', 'od_w_o', 'ff_w_up', 'ff_conv_w', 'ff_conv_b', 'ff_w_down', 'loss_target', 'm_norm_mix', 'm_norm_ffn', 'm_norm_final', 'm_ev_w_in', 'm_ev_gm_ln_g', 'm_ev_gm_ln_b', 'm_ev_gm_ws', 'm_ev_gm_bs', 'm_ev_conv_w', 'm_ev_conv_b', 'm_ev_dt_bias', 'm_ev_a_log', 'm_ev_d_skip', 'm_ev_ssm_norm_w', 'm_ev_w_out', 'm_od_w_in', 'm_od_q_norm', 'm_od_kv_norm', 'm_od_w_uq', 'm_od_w_ukv', 'm_od_w_o', 'm_ff_w_up', 'm_ff_conv_w', 'm_ff_conv_b', 'm_ff_w_down', 'v_norm_mix', 'v_norm_ffn', 'v_norm_final', 'v_ev_w_in', 'v_ev_gm_ln_g', 'v_ev_gm_ln_b', 'v_ev_gm_ws', 'v_ev_gm_bs', 'v_ev_conv_w', 'v_ev_conv_b', 'v_ev_dt_bias', 'v_ev_a_log', 'v_ev_d_skip', 'v_ev_ssm_norm_w', 'v_ev_w_out', 'v_od_w_in', 'v_od_q_norm', 'v_od_kv_norm', 'v_od_w_uq', 'v_od_w_ukv', 'v_od_w_o', 'v_ff_w_up', 'v_ff_conv_w', 'v_ff_conv_b', 'v_ff_w_down']
TWIN_OUTPUTS = ['loss', 'grad_x', 'grad_norm_mix', 'grad_norm_ffn', 'grad_norm_final', 'grad_ev_w_in', 'grad_ev_gm_ln_g', 'grad_ev_gm_ln_b', 'grad_ev_gm_ws', 'grad_ev_gm_bs', 'grad_ev_conv_w', 'grad_ev_conv_b', 'grad_ev_dt_bias', 'grad_ev_a_log', 'grad_ev_d_skip', 'grad_ev_ssm_norm_w', 'grad_ev_w_out', 'grad_od_w_in', 'grad_od_q_norm', 'grad_od_kv_norm', 'grad_od_w_uq', 'grad_od_w_ukv', 'grad_od_w_o', 'grad_ff_w_up', 'grad_ff_conv_w', 'grad_ff_conv_b', 'grad_ff_w_down', 'delta_norm_mix', 'delta_norm_ffn', 'delta_norm_final', 'delta_ev_w_in', 'delta_ev_gm_ln_g', 'delta_ev_gm_ln_b', 'delta_ev_gm_ws', 'delta_ev_gm_bs', 'delta_ev_conv_w', 'delta_ev_conv_b', 'delta_ev_dt_bias', 'delta_ev_a_log', 'delta_ev_d_skip', 'delta_ev_ssm_norm_w', 'delta_ev_w_out', 'delta_od_w_in', 'delta_od_q_norm', 'delta_od_kv_norm', 'delta_od_w_uq', 'delta_od_w_ukv', 'delta_od_w_o', 'delta_ff_w_up', 'delta_ff_conv_w', 'delta_ff_conv_b', 'delta_ff_w_down', 'new_m_norm_mix', 'new_m_norm_ffn', 'new_m_norm_final', 'new_m_ev_w_in', 'new_m_ev_gm_ln_g', 'new_m_ev_gm_ln_b', 'new_m_ev_gm_ws', 'new_m_ev_gm_bs', 'new_m_ev_conv_w', 'new_m_ev_conv_b', 'new_m_ev_dt_bias', 'new_m_ev_a_log', 'new_m_ev_d_skip', 'new_m_ev_ssm_norm_w', 'new_m_ev_w_out', 'new_m_od_w_in', 'new_m_od_q_norm', 'new_m_od_kv_norm', 'new_m_od_w_uq', 'new_m_od_w_ukv', 'new_m_od_w_o', 'new_m_ff_w_up', 'new_m_ff_conv_w', 'new_m_ff_conv_b', 'new_m_ff_w_down', 'new_v_norm_mix', 'new_v_norm_ffn', 'new_v_norm_final', 'new_v_ev_w_in', 'new_v_ev_gm_ln_g', 'new_v_ev_gm_ln_b', 'new_v_ev_gm_ws', 'new_v_ev_gm_bs', 'new_v_ev_conv_w', 'new_v_ev_conv_b', 'new_v_ev_dt_bias', 'new_v_ev_a_log', 'new_v_ev_d_skip', 'new_v_ev_ssm_norm_w', 'new_v_ev_w_out', 'new_v_od_w_in', 'new_v_od_q_norm', 'new_v_od_kv_norm', 'new_v_od_w_uq', 'new_v_od_w_ukv', 'new_v_od_w_o', 'new_v_ff_w_up', 'new_v_ff_conv_w', 'new_v_ff_conv_b', 'new_v_ff_w_down']
TWIN_LEAF_KINDS = {'loss': 'loss', 'grad_x': 'grad_x', 'grad_norm_mix': 'grad_w', 'grad_norm_ffn': 'grad_w', 'grad_norm_final': 'grad_w', 'grad_ev_w_in': 'grad_w', 'grad_ev_gm_ln_g': 'grad_w', 'grad_ev_gm_ln_b': 'grad_w', 'grad_ev_gm_ws': 'grad_w', 'grad_ev_gm_bs': 'grad_w', 'grad_ev_conv_w': 'grad_w', 'grad_ev_conv_b': 'grad_w', 'grad_ev_dt_bias': 'grad_w', 'grad_ev_a_log': 'grad_w', 'grad_ev_d_skip': 'grad_w', 'grad_ev_ssm_norm_w': 'grad_w', 'grad_ev_w_out': 'grad_w', 'grad_od_w_in': 'grad_w', 'grad_od_q_norm': 'grad_w', 'grad_od_kv_norm': 'grad_w', 'grad_od_w_uq': 'grad_w', 'grad_od_w_ukv': 'grad_w', 'grad_od_w_o': 'grad_w', 'grad_ff_w_up': 'grad_w', 'grad_ff_conv_w': 'grad_w', 'grad_ff_conv_b': 'grad_w', 'grad_ff_w_down': 'grad_w', 'delta_norm_mix': 'delta_w', 'delta_norm_ffn': 'delta_w', 'delta_norm_final': 'delta_w', 'delta_ev_w_in': 'delta_w', 'delta_ev_gm_ln_g': 'delta_w', 'delta_ev_gm_ln_b': 'delta_w', 'delta_ev_gm_ws': 'delta_w', 'delta_ev_gm_bs': 'delta_w', 'delta_ev_conv_w': 'delta_w', 'delta_ev_conv_b': 'delta_w', 'delta_ev_dt_bias': 'delta_w', 'delta_ev_a_log': 'delta_w', 'delta_ev_d_skip': 'delta_w', 'delta_ev_ssm_norm_w': 'delta_w', 'delta_ev_w_out': 'delta_w', 'delta_od_w_in': 'delta_w', 'delta_od_q_norm': 'delta_w', 'delta_od_kv_norm': 'delta_w', 'delta_od_w_uq': 'delta_w', 'delta_od_w_ukv': 'delta_w', 'delta_od_w_o': 'delta_w', 'delta_ff_w_up': 'delta_w', 'delta_ff_conv_w': 'delta_w', 'delta_ff_conv_b': 'delta_w', 'delta_ff_w_down': 'delta_w', 'new_m_norm_mix': 'new_m', 'new_m_norm_ffn': 'new_m', 'new_m_norm_final': 'new_m', 'new_m_ev_w_in': 'new_m', 'new_m_ev_gm_ln_g': 'new_m', 'new_m_ev_gm_ln_b': 'new_m', 'new_m_ev_gm_ws': 'new_m', 'new_m_ev_gm_bs': 'new_m', 'new_m_ev_conv_w': 'new_m', 'new_m_ev_conv_b': 'new_m', 'new_m_ev_dt_bias': 'new_m', 'new_m_ev_a_log': 'new_m', 'new_m_ev_d_skip': 'new_m', 'new_m_ev_ssm_norm_w': 'new_m', 'new_m_ev_w_out': 'new_m', 'new_m_od_w_in': 'new_m', 'new_m_od_q_norm': 'new_m', 'new_m_od_kv_norm': 'new_m', 'new_m_od_w_uq': 'new_m', 'new_m_od_w_ukv': 'new_m', 'new_m_od_w_o': 'new_m', 'new_m_ff_w_up': 'new_m', 'new_m_ff_conv_w': 'new_m', 'new_m_ff_conv_b': 'new_m', 'new_m_ff_w_down': 'new_m', 'new_v_norm_mix': 'new_v', 'new_v_norm_ffn': 'new_v', 'new_v_norm_final': 'new_v', 'new_v_ev_w_in': 'new_v', 'new_v_ev_gm_ln_g': 'new_v', 'new_v_ev_gm_ln_b': 'new_v', 'new_v_ev_gm_ws': 'new_v', 'new_v_ev_gm_bs': 'new_v', 'new_v_ev_conv_w': 'new_v', 'new_v_ev_conv_b': 'new_v', 'new_v_ev_dt_bias': 'new_v', 'new_v_ev_a_log': 'new_v', 'new_v_ev_d_skip': 'new_v', 'new_v_ev_ssm_norm_w': 'new_v', 'new_v_ev_w_out': 'new_v', 'new_v_od_w_in': 'new_v', 'new_v_od_q_norm': 'new_v', 'new_v_od_kv_norm': 'new_v', 'new_v_od_w_uq': 'new_v', 'new_v_od_w_ukv': 'new_v', 'new_v_od_w_o': 'new_v', 'new_v_ff_w_up': 'new_v', 'new_v_ff_conv_w': 'new_v', 'new_v_ff_conv_b': 'new_v', 'new_v_ff_w_down': 'new_v'}


def _forward(args):
    return _fwd_reference(*[args[k] for k in FWD_PARAMS])


def _output_shape():
    out = _jax.eval_shape(lambda: _forward(_fwd_setup_inputs(0)))
    return out.shape, out.dtype

N_MICROBATCH = 1
ADAM_LR = 0.001
ADAM_B1 = 0.9
ADAM_B2 = 0.999
ADAM_EPS = 1e-08
ADAM_WD = 0.01
ADAM_STEP = 10
PER_EXAMPLE_BATCH_AXIS = {'x': 0, 'positions': 0, 'loss_target': 0}
SHARED_INPUTS = []
_WEIGHT_DTYPES = {'norm_mix': _jnp.float32, 'norm_ffn': _jnp.float32, 'norm_final': _jnp.float32, 'ev_w_in': _jnp.float32, 'ev_gm_ln_g': _jnp.float32, 'ev_gm_ln_b': _jnp.float32, 'ev_gm_ws': _jnp.float32, 'ev_gm_bs': _jnp.float32, 'ev_conv_w': _jnp.float32, 'ev_conv_b': _jnp.float32, 'ev_dt_bias': _jnp.float32, 'ev_a_log': _jnp.float32, 'ev_d_skip': _jnp.float32, 'ev_ssm_norm_w': _jnp.float32, 'ev_w_out': _jnp.float32, 'od_w_in': _jnp.float32, 'od_q_norm': _jnp.float32, 'od_kv_norm': _jnp.float32, 'od_w_uq': _jnp.float32, 'od_w_ukv': _jnp.float32, 'od_w_o': _jnp.float32, 'ff_w_up': _jnp.float32, 'ff_conv_w': _jnp.float32, 'ff_conv_b': _jnp.float32, 'ff_w_down': _jnp.float32}
MOMENT_SCALE = {'norm_mix': 5.090660e-02, 'norm_ffn': 3.931259e-02, 'norm_final': 8.027260e+00, 'ev_w_in': 3.313339e-02, 'ev_gm_ln_g': 2.135624e-02, 'ev_gm_ln_b': 2.091696e-02, 'ev_gm_ws': 2.991946e-02, 'ev_gm_bs': 3.529014e-02, 'ev_conv_w': 3.302724e-02, 'ev_conv_b': 4.834344e-02, 'ev_dt_bias': 7.217161e-02, 'ev_a_log': 9.734146e-02, 'ev_d_skip': 1.716533e-01, 'ev_ssm_norm_w': 3.859756e-02, 'ev_w_out': 5.054435e-02, 'od_w_in': 1.883363e-02, 'od_q_norm': 1.255979e-02, 'od_kv_norm': 2.473418e-02, 'od_w_uq': 5.032731e-03, 'od_w_ukv': 8.412245e-03, 'od_w_o': 1.070247e-02, 'ff_w_up': 1.692484e-02, 'ff_conv_w': 1.710704e-02, 'ff_conv_b': 1.641376e-02, 'ff_w_down': 2.765787e-02}


def _to_microbatches(a, axis):
    t = _jnp.moveaxis(a, axis, 0)
    t = t.reshape((N_MICROBATCH, t.shape[0] // N_MICROBATCH) + t.shape[1:])
    return _jnp.moveaxis(t, 1, axis + 1)


def setup_inputs(seed: int = 0) -> dict:
    inp = _fwd_setup_inputs(seed)
    key = _jax.random.fold_in(_jax.random.key(seed), 7919)
    shape, _ = _output_shape()
    out = dict(inp)
    out["loss_target"] = _jax.random.normal(_jax.random.fold_in(key, 0), shape, _jnp.float32)
    for i, name in enumerate(TWIN_WEIGHTS):
        w = inp[name].astype(_jnp.float32)
        if MOMENT_SCALE is None:
            s = _jnp.sqrt(_jnp.mean(_jnp.square(w)) + 1e-30)
        else:
            s = MOMENT_SCALE[name]
        km, kv = _jax.random.split(_jax.random.fold_in(key, i + 1))
        out[name] = w
        out["m_" + name] = s * _jax.random.normal(km, w.shape, _jnp.float32)
        out["v_" + name] = (s * s) * _jax.random.uniform(kv, w.shape, _jnp.float32, 0.5, 1.5)
    if N_MICROBATCH > 1:
        for name, axis in PER_EXAMPLE_BATCH_AXIS.items():
            out[name] = _to_microbatches(out[name], axis)
    return {'x': out['x'], 'positions': out['positions'], 'norm_mix': out['norm_mix'], 'norm_ffn': out['norm_ffn'], 'norm_final': out['norm_final'], 'ev_w_in': out['ev_w_in'], 'ev_gm_ln_g': out['ev_gm_ln_g'], 'ev_gm_ln_b': out['ev_gm_ln_b'], 'ev_gm_ws': out['ev_gm_ws'], 'ev_gm_bs': out['ev_gm_bs'], 'ev_conv_w': out['ev_conv_w'], 'ev_conv_b': out['ev_conv_b'], 'ev_dt_bias': out['ev_dt_bias'], 'ev_a_log': out['ev_a_log'], 'ev_d_skip': out['ev_d_skip'], 'ev_ssm_norm_w': out['ev_ssm_norm_w'], 'ev_w_out': out['ev_w_out'], 'od_w_in': out['od_w_in'], 'od_q_norm': out['od_q_norm'], 'od_kv_norm': out['od_kv_norm'], 'od_w_uq': out['od_w_uq'], 'od_w_ukv': out['od_w_ukv'], 'od_w_o': out['od_w_o'], 'ff_w_up': out['ff_w_up'], 'ff_conv_w': out['ff_conv_w'], 'ff_conv_b': out['ff_conv_b'], 'ff_w_down': out['ff_w_down'], 'loss_target': out['loss_target'], 'm_norm_mix': out['m_norm_mix'], 'm_norm_ffn': out['m_norm_ffn'], 'm_norm_final': out['m_norm_final'], 'm_ev_w_in': out['m_ev_w_in'], 'm_ev_gm_ln_g': out['m_ev_gm_ln_g'], 'm_ev_gm_ln_b': out['m_ev_gm_ln_b'], 'm_ev_gm_ws': out['m_ev_gm_ws'], 'm_ev_gm_bs': out['m_ev_gm_bs'], 'm_ev_conv_w': out['m_ev_conv_w'], 'm_ev_conv_b': out['m_ev_conv_b'], 'm_ev_dt_bias': out['m_ev_dt_bias'], 'm_ev_a_log': out['m_ev_a_log'], 'm_ev_d_skip': out['m_ev_d_skip'], 'm_ev_ssm_norm_w': out['m_ev_ssm_norm_w'], 'm_ev_w_out': out['m_ev_w_out'], 'm_od_w_in': out['m_od_w_in'], 'm_od_q_norm': out['m_od_q_norm'], 'm_od_kv_norm': out['m_od_kv_norm'], 'm_od_w_uq': out['m_od_w_uq'], 'm_od_w_ukv': out['m_od_w_ukv'], 'm_od_w_o': out['m_od_w_o'], 'm_ff_w_up': out['m_ff_w_up'], 'm_ff_conv_w': out['m_ff_conv_w'], 'm_ff_conv_b': out['m_ff_conv_b'], 'm_ff_w_down': out['m_ff_w_down'], 'v_norm_mix': out['v_norm_mix'], 'v_norm_ffn': out['v_norm_ffn'], 'v_norm_final': out['v_norm_final'], 'v_ev_w_in': out['v_ev_w_in'], 'v_ev_gm_ln_g': out['v_ev_gm_ln_g'], 'v_ev_gm_ln_b': out['v_ev_gm_ln_b'], 'v_ev_gm_ws': out['v_ev_gm_ws'], 'v_ev_gm_bs': out['v_ev_gm_bs'], 'v_ev_conv_w': out['v_ev_conv_w'], 'v_ev_conv_b': out['v_ev_conv_b'], 'v_ev_dt_bias': out['v_ev_dt_bias'], 'v_ev_a_log': out['v_ev_a_log'], 'v_ev_d_skip': out['v_ev_d_skip'], 'v_ev_ssm_norm_w': out['v_ev_ssm_norm_w'], 'v_ev_w_out': out['v_ev_w_out'], 'v_od_w_in': out['v_od_w_in'], 'v_od_q_norm': out['v_od_q_norm'], 'v_od_kv_norm': out['v_od_kv_norm'], 'v_od_w_uq': out['v_od_w_uq'], 'v_od_w_ukv': out['v_od_w_ukv'], 'v_od_w_o': out['v_od_w_o'], 'v_ff_w_up': out['v_ff_w_up'], 'v_ff_conv_w': out['v_ff_conv_w'], 'v_ff_conv_b': out['v_ff_conv_b'], 'v_ff_w_down': out['v_ff_w_down']}


def _loss(weights, diff, rest, loss_target):
    with _jax.named_scope("forward"):
        args = {**rest, TWIN_DIFF_INPUT: diff, **{k: w.astype(_WEIGHT_DTYPES[k]) for k, w in weights.items()}}
        y = _forward(args)
    with _jax.named_scope("loss_head"):
        err = _jnp.square(y.astype(_jnp.float32) - loss_target)
        return 0.5 * _jnp.sum(_jnp.mean(err, axis=-1)) if err.ndim else 0.5 * err


def _adamw(w, g, m, v):
    m = ADAM_B1 * m + (1.0 - ADAM_B1) * g
    v = ADAM_B2 * v + (1.0 - ADAM_B2) * _jnp.square(g)
    m_hat = m / (1.0 - ADAM_B1 ** ADAM_STEP)
    v_hat = v / (1.0 - ADAM_B2 ** ADAM_STEP)
    delta = -ADAM_LR * (m_hat / (_jnp.sqrt(v_hat) + ADAM_EPS) + ADAM_WD * w)
    return delta, m, v


def reference(x, positions, norm_mix, norm_ffn, norm_final, ev_w_in, ev_gm_ln_g, ev_gm_ln_b, ev_gm_ws, ev_gm_bs, ev_conv_w, ev_conv_b, ev_dt_bias, ev_a_log, ev_d_skip, ev_ssm_norm_w, ev_w_out, od_w_in, od_q_norm, od_kv_norm, od_w_uq, od_w_ukv, od_w_o, ff_w_up, ff_conv_w, ff_conv_b, ff_w_down, loss_target, m_norm_mix, m_norm_ffn, m_norm_final, m_ev_w_in, m_ev_gm_ln_g, m_ev_gm_ln_b, m_ev_gm_ws, m_ev_gm_bs, m_ev_conv_w, m_ev_conv_b, m_ev_dt_bias, m_ev_a_log, m_ev_d_skip, m_ev_ssm_norm_w, m_ev_w_out, m_od_w_in, m_od_q_norm, m_od_kv_norm, m_od_w_uq, m_od_w_ukv, m_od_w_o, m_ff_w_up, m_ff_conv_w, m_ff_conv_b, m_ff_w_down, v_norm_mix, v_norm_ffn, v_norm_final, v_ev_w_in, v_ev_gm_ln_g, v_ev_gm_ln_b, v_ev_gm_ws, v_ev_gm_bs, v_ev_conv_w, v_ev_conv_b, v_ev_dt_bias, v_ev_a_log, v_ev_d_skip, v_ev_ssm_norm_w, v_ev_w_out, v_od_w_in, v_od_q_norm, v_od_kv_norm, v_od_w_uq, v_od_w_ukv, v_od_w_o, v_ff_w_up, v_ff_conv_w, v_ff_conv_b, v_ff_w_down):
    given = dict(x=x, positions=positions, norm_mix=norm_mix, norm_ffn=norm_ffn, norm_final=norm_final, ev_w_in=ev_w_in, ev_gm_ln_g=ev_gm_ln_g, ev_gm_ln_b=ev_gm_ln_b, ev_gm_ws=ev_gm_ws, ev_gm_bs=ev_gm_bs, ev_conv_w=ev_conv_w, ev_conv_b=ev_conv_b, ev_dt_bias=ev_dt_bias, ev_a_log=ev_a_log, ev_d_skip=ev_d_skip, ev_ssm_norm_w=ev_ssm_norm_w, ev_w_out=ev_w_out, od_w_in=od_w_in, od_q_norm=od_q_norm, od_kv_norm=od_kv_norm, od_w_uq=od_w_uq, od_w_ukv=od_w_ukv, od_w_o=od_w_o, ff_w_up=ff_w_up, ff_conv_w=ff_conv_w, ff_conv_b=ff_conv_b, ff_w_down=ff_w_down, loss_target=loss_target, m_norm_mix=m_norm_mix, m_norm_ffn=m_norm_ffn, m_norm_final=m_norm_final, m_ev_w_in=m_ev_w_in, m_ev_gm_ln_g=m_ev_gm_ln_g, m_ev_gm_ln_b=m_ev_gm_ln_b, m_ev_gm_ws=m_ev_gm_ws, m_ev_gm_bs=m_ev_gm_bs, m_ev_conv_w=m_ev_conv_w, m_ev_conv_b=m_ev_conv_b, m_ev_dt_bias=m_ev_dt_bias, m_ev_a_log=m_ev_a_log, m_ev_d_skip=m_ev_d_skip, m_ev_ssm_norm_w=m_ev_ssm_norm_w, m_ev_w_out=m_ev_w_out, m_od_w_in=m_od_w_in, m_od_q_norm=m_od_q_norm, m_od_kv_norm=m_od_kv_norm, m_od_w_uq=m_od_w_uq, m_od_w_ukv=m_od_w_ukv, m_od_w_o=m_od_w_o, m_ff_w_up=m_ff_w_up, m_ff_conv_w=m_ff_conv_w, m_ff_conv_b=m_ff_conv_b, m_ff_w_down=m_ff_w_down, v_norm_mix=v_norm_mix, v_norm_ffn=v_norm_ffn, v_norm_final=v_norm_final, v_ev_w_in=v_ev_w_in, v_ev_gm_ln_g=v_ev_gm_ln_g, v_ev_gm_ln_b=v_ev_gm_ln_b, v_ev_gm_ws=v_ev_gm_ws, v_ev_gm_bs=v_ev_gm_bs, v_ev_conv_w=v_ev_conv_w, v_ev_conv_b=v_ev_conv_b, v_ev_dt_bias=v_ev_dt_bias, v_ev_a_log=v_ev_a_log, v_ev_d_skip=v_ev_d_skip, v_ev_ssm_norm_w=v_ev_ssm_norm_w, v_ev_w_out=v_ev_w_out, v_od_w_in=v_od_w_in, v_od_q_norm=v_od_q_norm, v_od_kv_norm=v_od_kv_norm, v_od_w_uq=v_od_w_uq, v_od_w_ukv=v_od_w_ukv, v_od_w_o=v_od_w_o, v_ff_w_up=v_ff_w_up, v_ff_conv_w=v_ff_conv_w, v_ff_conv_b=v_ff_conv_b, v_ff_w_down=v_ff_w_down)
    weights = {n: given[n] for n in TWIN_WEIGHTS}
    shared = {n: given[n] for n in SHARED_INPUTS}
    per_example = {n: given[n] for n in ['x', 'positions']}
    grad_fn = _jax.value_and_grad(_loss, argnums=(0, 1))

    def one_microbatch(ex, loss_target):
        ex = dict(ex)
        diff = ex.pop(TWIN_DIFF_INPUT)
        return grad_fn(weights, diff, {**shared, **ex}, loss_target)

    if N_MICROBATCH == 1:
        loss, (grad_w, grad_x) = one_microbatch(per_example, given["loss_target"])
    else:
        def body(carry, xs):
            loss_sum, grad_sum = carry
            l_k, (gw_k, gx_k) = one_microbatch(xs[0], xs[1])
            with _jax.named_scope("update"):
                return (loss_sum + l_k, _jax.tree.map(_jnp.add, grad_sum, gw_k)), gx_k

        init = (_jnp.zeros((), _jnp.float32), _jax.tree.map(_jnp.zeros_like, weights))
        (loss, grad_w), grad_x = _jax.lax.scan(body, init, (per_example, given["loss_target"]))
    with _jax.named_scope("update"):
        delta_w, new_m, new_v = {}, {}, {}
        for n in TWIN_WEIGHTS:
            delta_w[n], new_m[n], new_v[n] = _adamw(weights[n], grad_w[n], given["m_" + n], given["v_" + n])
    return (loss, grad_x, *[grad_w[n] for n in TWIN_WEIGHTS], *[delta_w[n] for n in TWIN_WEIGHTS],
            *[new_m[n] for n in TWIN_WEIGHTS], *[new_v[n] for n in TWIN_WEIGHTS])
```

```python
import functools
import math

import jax
import jax.numpy as jnp
from jax import lax
from jax.experimental import pallas as pl
from jax.experimental.pallas import tpu as pltpu

F32 = jnp.float32
BF16 = jnp.bfloat16
EPS = 1e-6
CHUNK = 64
BLK = 128
GM_GROUPS = 8
SSM_HEAD_DIM = 64
SSM_GROUPS = 4
SSM_STATE = 128
SSM_CONV = 4
FFN_CONV = 3
MLA_NOPE = 128
MLA_ROPE = 64
MLA_V = 128
ROPE_THETA = 10000.0
V7X_VMEM_LIMIT = 56 * 1024 * 1024
HI = lax.Precision.HIGHEST

ADAM_LR = 0.001
ADAM_B1 = 0.9
ADAM_B2 = 0.999
ADAM_EPS = 1e-08
ADAM_WD = 0.01
ADAM_STEP = 10


def _cp(*sem):
    return pltpu.CompilerParams(dimension_semantics=sem if sem else None, vmem_limit_bytes=V7X_VMEM_LIMIT)


def _tile(n, cands):
    for c in cands:
        if n % c == 0:
            return c
    return n


def _row(v):
    return v.reshape(1, -1).astype(F32)


def _mm(a, b, *, ta=False, tb=False, out_dtype=F32, residual=None, name):
    m, k = (a.shape[1], a.shape[0]) if ta else a.shape
    n = b.shape[0] if tb else b.shape[1]
    assert k == (b.shape[1] if tb else b.shape[0]), (a.shape, b.shape, ta, tb)
    tm = _tile(m, (1024, 512, 256, 128))
    tn = _tile(n, (1024, 768, 512, 384, 256, 128))
    tk = _tile(k, (512, 256, 128))
    nk = k // tk
    dn = (((0 if ta else 1,), (1 if tb else 0,)), ((), ()))
    has_res = residual is not None

    def body(*refs):
        if has_res:
            a_ref, b_ref, r_ref, o_ref, acc = refs
        else:
            a_ref, b_ref, o_ref, acc = refs
        kk = pl.program_id(2)

        @pl.when(kk == 0)
        def _():
            acc[...] = jnp.zeros_like(acc)

        acc[...] += lax.dot_general(a_ref[...].astype(BF16), b_ref[...].astype(BF16), dn,
                                    preferred_element_type=F32)

        @pl.when(kk == nk - 1)
        def _():
            r = acc[...]
            if has_res:
                r = r + r_ref[...].astype(F32)
            o_ref[...] = r.astype(out_dtype)

    a_spec = pl.BlockSpec((tk, tm), lambda i, j, kk: (kk, i)) if ta else pl.BlockSpec((tm, tk), lambda i, j, kk: (i, kk))
    b_spec = pl.BlockSpec((tn, tk), lambda i, j, kk: (j, kk)) if tb else pl.BlockSpec((tk, tn), lambda i, j, kk: (kk, j))
    in_specs = [a_spec, b_spec]
    args = [a, b]
    if has_res:
        in_specs.append(pl.BlockSpec((tm, tn), lambda i, j, kk: (i, j)))
        args.append(residual)
    return pl.pallas_call(
        body, name=name,
        grid=(m // tm, n // tn, nk),
        in_specs=in_specs,
        out_specs=pl.BlockSpec((tm, tn), lambda i, j, kk: (i, j)),
        out_shape=jax.ShapeDtypeStruct((m, n), out_dtype),
        scratch_shapes=[pltpu.VMEM((tm, tn), F32)],
        compiler_params=_cp("parallel", "parallel", "arbitrary"),
    )(*args)


def _rms_fwd(x, w, *, width=None, col=0, out_dtype=None, name):
    out_dtype = out_dtype or BF16
    s = x.shape[0]
    width = width or x.shape[1]
    tr = _tile(s, (256, 128))

    def body(x_ref, w_ref, o_ref):
        xv = x_ref[...]
        r = lax.rsqrt(jnp.mean(xv * xv, axis=-1, keepdims=True) + EPS)
        o_ref[...] = (xv * r * w_ref[...]).astype(out_dtype)

    return pl.pallas_call(
        body, name=name, grid=(s // tr,),
        in_specs=[pl.BlockSpec((tr, width), lambda i: (i, col)), pl.BlockSpec((1, width), lambda i: (0, 0))],
        out_specs=pl.BlockSpec((tr, width), lambda i: (i, 0)),
        out_shape=jax.ShapeDtypeStruct((s, width), out_dtype),
        compiler_params=_cp("parallel"),
    )(x, _row(w))


def _rms_bwd(x, w, dy, *, add=None, width=None, col=0, dy_col=0, out_dtype=F32, name):
    s = x.shape[0]
    width = width or x.shape[1]
    tr = _tile(s, (256, 128))
    has_add = add is not None

    def body(*refs):
        if has_add:
            x_ref, w_ref, dy_ref, add_ref, dx_ref, dw_ref = refs
        else:
            x_ref, w_ref, dy_ref, dx_ref, dw_ref = refs
        xv = x_ref[...]
        dyv = dy_ref[...].astype(F32)
        r = lax.rsqrt(jnp.mean(xv * xv, axis=-1, keepdims=True) + EPS)
        xh = xv * r
        g = dyv * w_ref[...]
        dx = r * (g - xh * jnp.mean(g * xh, axis=-1, keepdims=True))
        if has_add:
            dx = dx + add_ref[...]
        dx_ref[...] = dx.astype(out_dtype)

        @pl.when(pl.program_id(0) == 0)
        def _():
            dw_ref[...] = jnp.zeros_like(dw_ref)

        dw_ref[...] += jnp.sum(dyv * xh, axis=0, keepdims=True)

    in_specs = [pl.BlockSpec((tr, width), lambda i: (i, col)), pl.BlockSpec((1, width), lambda i: (0, 0)),
                pl.BlockSpec((tr, width), lambda i: (i, dy_col))]
    args = [x, _row(w), dy]
    if has_add:
        in_specs.append(pl.BlockSpec((tr, width), lambda i: (i, 0)))
        args.append(add)
    return pl.pallas_call(
        body, name=name, grid=(s // tr,),
        in_specs=in_specs,
        out_specs=[pl.BlockSpec((tr, width), lambda i: (i, 0)), pl.BlockSpec((1, width), lambda i: (0, 0))],
        out_shape=[jax.ShapeDtypeStruct((s, width), out_dtype), jax.ShapeDtypeStruct((1, width), F32)],
        compiler_params=_cp("arbitrary"),
    )(*args)


def _loss_bwd(h, w, tgt, *, name):
    s, d = h.shape
    tr = _tile(s, (256, 128))

    def body(x_ref, w_ref, t_ref, loss_ref, dx_ref, dw_ref):
        xv = x_ref[...]
        r = lax.rsqrt(jnp.mean(xv * xv, axis=-1, keepdims=True) + EPS)
        xh = xv * r
        e = xh * w_ref[...] - t_ref[...]
        part = 0.5 * jnp.sum(jnp.mean(e * e, axis=-1, keepdims=True), axis=0, keepdims=True)
        dyv = e * (1.0 / d)
        g = dyv * w_ref[...]
        dx_ref[...] = r * (g - xh * jnp.mean(g * xh, axis=-1, keepdims=True))

        @pl.when(pl.program_id(0) == 0)
        def _():
            dw_ref[...] = jnp.zeros_like(dw_ref)
            loss_ref[...] = jnp.zeros_like(loss_ref)

        dw_ref[...] += jnp.sum(dyv * xh, axis=0, keepdims=True)
        loss_ref[...] += jnp.broadcast_to(part, loss_ref.shape)

    return pl.pallas_call(
        body, name=name, grid=(s // tr,),
        in_specs=[pl.BlockSpec((tr, d), lambda i: (i, 0)), pl.BlockSpec((1, d), lambda i: (0, 0)),
                  pl.BlockSpec((tr, d), lambda i: (i, 0))],
        out_specs=[pl.BlockSpec((1, 128), lambda i: (0, 0)), pl.BlockSpec((tr, d), lambda i: (i, 0)),
                   pl.BlockSpec((1, d), lambda i: (0, 0))],
        out_shape=[jax.ShapeDtypeStruct((1, 128), F32), jax.ShapeDtypeStruct((s, d), F32),
                   jax.ShapeDtypeStruct((1, d), F32)],
        compiler_params=_cp("arbitrary"),
    )(h, _row(w), tgt)


_G0 = math.sqrt(2.0 / math.pi)
_G1 = 0.044715


def _gelu(x):
    return 0.5 * x * (1.0 + jnp.tanh(_G0 * (x + _G1 * x * x * x)))


def _gelu_and_grad(x):
    th = jnp.tanh(_G0 * (x + _G1 * x * x * x))
    val = 0.5 * x * (1.0 + th)
    grad = 0.5 * (1.0 + th) + 0.5 * x * (1.0 - th * th) * _G0 * (1.0 + 3.0 * _G1 * x * x)
    return val, grad


def _sigmoid(x):
    return 1.0 / (1.0 + jnp.exp(-x))


def _shift_down(x, k):
    if k == 0:
        return x
    rows = lax.broadcasted_iota(jnp.int32, x.shape, 0)
    return jnp.where(rows >= k, pltpu.roll(x, k, 0), 0.0)


def _shift_up(x, k):
    if k == 0:
        return x
    n = x.shape[0]
    rows = lax.broadcasted_iota(jnp.int32, x.shape, 0)
    return jnp.where(rows < n - k, pltpu.roll(x, n - k, 0), 0.0)


def _conv_rows(x, w_ref, b_ref, kw):
    y = b_ref[...] + w_ref[kw - 1:kw, :] * x
    for k in range(kw - 1):
        y = y + w_ref[k:k + 1, :] * _shift_down(x, kw - 1 - k)
    return y


def _conv_rows_bwd(x, dgc, w_ref, dw_ref, db_ref, kw):
    dx = w_ref[kw - 1:kw, :] * dgc
    dw_ref[kw - 1:kw, :] = jnp.sum(dgc * x, axis=0, keepdims=True)
    for k in range(kw - 1):
        sh = kw - 1 - k
        dx = dx + w_ref[k:k + 1, :] * _shift_up(dgc, sh)
        dw_ref[k:k + 1, :] = jnp.sum(dgc * _shift_down(x, sh), axis=0, keepdims=True)
    db_ref[...] = jnp.sum(dgc, axis=0, keepdims=True)
    return dx


def _ffn_mid_fwd(up, conv_w, conv_b, *, name):
    s = up.shape[0]
    f = up.shape[1] // 2
    tc = _tile(f, (256, 128))
    nf = f // tc

    def body(g_ref, v_ref, w_ref, b_ref, o_ref):
        gc = _conv_rows(g_ref[...], w_ref, b_ref, FFN_CONV)
        o_ref[...] = (_gelu(gc) * v_ref[...]).astype(BF16)

    return pl.pallas_call(
        body, name=name, grid=(nf,),
        in_specs=[pl.BlockSpec((s, tc), lambda j: (0, j)), pl.BlockSpec((s, tc), lambda j: (0, j + nf)),
                  pl.BlockSpec((FFN_CONV, tc), lambda j: (0, j)), pl.BlockSpec((1, tc), lambda j: (0, j))],
        out_specs=pl.BlockSpec((s, tc), lambda j: (0, j)),
        out_shape=jax.ShapeDtypeStruct((s, f), BF16),
        compiler_params=_cp("parallel"),
    )(up, up, conv_w, _row(conv_b))


def _ffn_mid_bwd(up, conv_w, conv_b, da, *, name):
    s = up.shape[0]
    f = up.shape[1] // 2
    tc = _tile(f, (256, 128))
    nf = f // tc

    def body(g_ref, v_ref, w_ref, b_ref, da_ref, dg_ref, dv_ref, dw_ref, db_ref):
        g = g_ref[...]
        gc = _conv_rows(g, w_ref, b_ref, FFN_CONV)
        gel, dgel = _gelu_and_grad(gc)
        dav = da_ref[...]
        dv_ref[...] = (dav * gel).astype(BF16)
        dgc = dav * v_ref[...] * dgel
        dg_ref[...] = _conv_rows_bwd(g, dgc, w_ref, dw_ref, db_ref, FFN_CONV).astype(BF16)

    col = lambda j: (0, j)
    return pl.pallas_call(
        body, name=name, grid=(nf,),
        in_specs=[pl.BlockSpec((s, tc), col), pl.BlockSpec((s, tc), lambda j: (0, j + nf)),
                  pl.BlockSpec((FFN_CONV, tc), col), pl.BlockSpec((1, tc), col), pl.BlockSpec((s, tc), col)],
        out_specs=[pl.BlockSpec((s, tc), col), pl.BlockSpec((s, tc), col),
                   pl.BlockSpec((FFN_CONV, tc), col), pl.BlockSpec((1, tc), col)],
        out_shape=[jax.ShapeDtypeStruct((s, f), BF16), jax.ShapeDtypeStruct((s, f), BF16),
                   jax.ShapeDtypeStruct((FFN_CONV, f), F32), jax.ShapeDtypeStruct((1, f), F32)],
        compiler_params=_cp("parallel"),
    )(up, up, conv_w, _row(conv_b), da)


def _gm_mask():
    r = lax.broadcasted_iota(jnp.int32, (BLK, BLK), 0) // CHUNK
    c = lax.broadcasted_iota(jnp.int32, (BLK, BLK), 1) // CHUNK
    return r >= c


def _gm_specs(s, gd):
    nb = s // BLK
    u_spec = pl.BlockSpec((BLK, gd), lambda g, n: (n, g))
    v_spec = pl.BlockSpec((BLK, gd), lambda g, n: (n, g + GM_GROUPS))
    vec_spec = pl.BlockSpec((1, gd), lambda g, n: (0, g))
    ws_spec = pl.BlockSpec((1, BLK, BLK), lambda g, n: (g, 0, 0))
    bs_spec = pl.BlockSpec((1, BLK, 1), lambda g, n: (g, 0, 0))
    return nb, u_spec, v_spec, vec_spec, ws_spec, bs_spec


def _gm_fwd(proj, ln_g, ln_b, ws, bs, *, name):
    s = proj.shape[0]
    gd = ln_g.shape[-1]
    w = GM_GROUPS * gd
    nb, u_spec, v_spec, vec_spec, ws_spec, bs_spec = _gm_specs(s, gd)

    def body(u_ref, v_ref, lg_ref, lb_ref, ws_ref, bs_ref, o_ref):
        ua = _gelu(u_ref[...])
        va = _gelu(v_ref[...])
        mu = jnp.mean(va, axis=-1, keepdims=True)
        vc = va - mu
        var = jnp.mean(vc * vc, axis=-1, keepdims=True)
        vn = vc * lax.rsqrt(var + EPS) * lg_ref[...] + lb_ref[...]
        wm = jnp.where(_gm_mask(), ws_ref[0], 0.0).astype(BF16)
        gate = jnp.dot(wm, vn.astype(BF16), preferred_element_type=F32) + bs_ref[0]
        o_ref[...] = (ua * gate).astype(BF16)

    return pl.pallas_call(
        body, name=name, grid=(GM_GROUPS, nb),
        in_specs=[u_spec, v_spec, vec_spec, vec_spec, ws_spec, bs_spec],
        out_specs=pl.BlockSpec((BLK, gd), lambda g, n: (n, g)),
        out_shape=jax.ShapeDtypeStruct((s, w), BF16),
        compiler_params=_cp("parallel", "parallel"),
    )(proj, proj, ln_g.reshape(1, w), ln_b.reshape(1, w), ws, bs.reshape(GM_GROUPS, BLK, 1))


def _gm_bwd(proj, ln_g, ln_b, ws, bs, dya, *, name):
    s = proj.shape[0]
    gd = ln_g.shape[-1]
    w = GM_GROUPS * gd
    nb, u_spec, v_spec, vec_spec, ws_spec, bs_spec = _gm_specs(s, gd)

    def body(u_ref, v_ref, lg_ref, lb_ref, ws_ref, bs_ref, dy_ref, du_ref, dv_ref, dlg_ref, dlb_ref, dws_ref, dbs_ref):
        ua, dua_du = _gelu_and_grad(u_ref[...])
        va, dva_dv = _gelu_and_grad(v_ref[...])
        mu = jnp.mean(va, axis=-1, keepdims=True)
        vc = va - mu
        var = jnp.mean(vc * vc, axis=-1, keepdims=True)
        rstd = lax.rsqrt(var + EPS)
        xh = vc * rstd
        vn = (xh * lg_ref[...] + lb_ref[...]).astype(BF16)
        mask = _gm_mask()
        wm = jnp.where(mask, ws_ref[0], 0.0).astype(BF16)
        gate = jnp.dot(wm, vn, preferred_element_type=F32) + bs_ref[0]
        dy = dy_ref[...]
        du_ref[...] = (dy * gate * dua_du).astype(BF16)
        dgate = dy * ua
        dgb = dgate.astype(BF16)
        dwm = lax.dot_general(dgb, vn, (((1,), (1,)), ((), ())), preferred_element_type=F32)
        dvn = lax.dot_general(wm, dgb, (((0,), (0,)), ((), ())), preferred_element_type=F32)
        dxh = dvn * lg_ref[...]
        dva = rstd * (dxh - jnp.mean(dxh, axis=-1, keepdims=True) - xh * jnp.mean(dxh * xh, axis=-1, keepdims=True))
        dv_ref[...] = (dva * dva_dv).astype(BF16)

        @pl.when(pl.program_id(1) == 0)
        def _():
            dlg_ref[...] = jnp.zeros_like(dlg_ref)
            dlb_ref[...] = jnp.zeros_like(dlb_ref)
            dws_ref[...] = jnp.zeros_like(dws_ref)
            dbs_ref[...] = jnp.zeros_like(dbs_ref)

        dlg_ref[...] += jnp.sum(dvn * xh, axis=0, keepdims=True)
        dlb_ref[...] += jnp.sum(dvn, axis=0, keepdims=True)
        dws_ref[0] += jnp.where(mask, dwm, 0.0)
        dbs_ref[0] += jnp.sum(dgate, axis=-1, keepdims=True)

    out_uv = pl.BlockSpec((BLK, gd), lambda g, n: (n, g))
    return pl.pallas_call(
        body, name=name, grid=(GM_GROUPS, nb),
        in_specs=[u_spec, v_spec, vec_spec, vec_spec, ws_spec, bs_spec, pl.BlockSpec((BLK, gd), lambda g, n: (n, g))],
        out_specs=[out_uv, out_uv, vec_spec, vec_spec, ws_spec, bs_spec],
        out_shape=[jax.ShapeDtypeStruct((s, w), BF16), jax.ShapeDtypeStruct((s, w), BF16),
                   jax.ShapeDtypeStruct((1, w), F32), jax.ShapeDtypeStruct((1, w), F32),
                   jax.ShapeDtypeStruct((GM_GROUPS, BLK, BLK), F32), jax.ShapeDtypeStruct((GM_GROUPS, BLK, 1), F32)],
        compiler_params=_cp("parallel", "arbitrary"),
    )(proj, proj, ln_g.reshape(1, w), ln_b.reshape(1, w), ws, bs.reshape(GM_GROUPS, BLK, 1), dya)


def _silu_conv_fwd(proj, conv_w, conv_b, *, col0, name):
    s = proj.shape[0]
    c = conv_w.shape[1]
    tc = _tile(c, (256, 128))
    off = col0 // tc

    def body(x_ref, w_ref, b_ref, o_ref):
        y = _conv_rows(x_ref[...], w_ref, b_ref, SSM_CONV)
        o_ref[...] = y * _sigmoid(y)

    col = lambda j: (0, j)
    return pl.pallas_call(
        body, name=name, grid=(c // tc,),
        in_specs=[pl.BlockSpec((s, tc), lambda j: (0, j + off)), pl.BlockSpec((SSM_CONV, tc), col), pl.BlockSpec((1, tc), col)],
        out_specs=pl.BlockSpec((s, tc), col),
        out_shape=jax.ShapeDtypeStruct((s, c), F32),
        compiler_params=_cp("parallel"),
    )(proj, conv_w, _row(conv_b))


def _silu_conv_bwd(proj, conv_w, conv_b, dact, *, col0, name):
    s = proj.shape[0]
    c = conv_w.shape[1]
    tc = _tile(c, (256, 128))
    off = col0 // tc

    def body(x_ref, w_ref, b_ref, d_ref, dx_ref, dw_ref, db_ref):
        x = x_ref[...]
        y = _conv_rows(x, w_ref, b_ref, SSM_CONV)
        sg = _sigmoid(y)
        dgc = d_ref[...] * sg * (1.0 + y * (1.0 - sg))
        dx_ref[...] = _conv_rows_bwd(x, dgc, w_ref, dw_ref, db_ref, SSM_CONV).astype(BF16)

    col = lambda j: (0, j)
    return pl.pallas_call(
        body, name=name, grid=(c // tc,),
        in_specs=[pl.BlockSpec((s, tc), lambda j: (0, j + off)), pl.BlockSpec((SSM_CONV, tc), col), pl.BlockSpec((1, tc), col),
                  pl.BlockSpec((s, tc), col)],
        out_specs=[pl.BlockSpec((s, tc), col), pl.BlockSpec((SSM_CONV, tc), col), pl.BlockSpec((1, tc), col)],
        out_shape=[jax.ShapeDtypeStruct((s, c), BF16), jax.ShapeDtypeStruct((SSM_CONV, c), F32),
                   jax.ShapeDtypeStruct((1, c), F32)],
        compiler_params=_cp("parallel"),
    )(proj, conv_w, _row(conv_b), dact)


def _head_select(heads):
    r = lax.broadcasted_iota(jnp.int32, (128, heads * SSM_HEAD_DIM), 0)
    c = lax.broadcasted_iota(jnp.int32, (128, heads * SSM_HEAD_DIM), 1) // SSM_HEAD_DIM
    return (r == c).astype(F32)


def _dt_fwd(dt_raw, dt_bias, *, heads, name):
    s = dt_raw.shape[0]
    d = heads * SSM_HEAD_DIM
    tr = _tile(s, (256, 128))

    def body(x_ref, b_ref, o_ref):
        pre = jnp.dot(x_ref[...] + b_ref[...], _head_select(heads), precision=HI, preferred_element_type=F32)
        o_ref[...] = jax.nn.softplus(pre)

    return pl.pallas_call(
        body, name=name, grid=(s // tr,),
        in_specs=[pl.BlockSpec((tr, 128), lambda i: (i, 0)), pl.BlockSpec((1, 128), lambda i: (0, 0))],
        out_specs=pl.BlockSpec((tr, d), lambda i: (i, 0)),
        out_shape=jax.ShapeDtypeStruct((s, d), F32),
        compiler_params=_cp("parallel"),
    )(dt_raw, dt_bias)


def _dt_bwd(dt_raw, dt_bias, zt, da_lane, dd_lane, a_row, *, heads, name):
    s = dt_raw.shape[0]
    d = heads * SSM_HEAD_DIM
    tr = _tile(s, (256, 128))
    nt = (((1,), (1,)), ((), ()))

    def body(x_ref, b_ref, z_ref, da_ref, dd_ref, a_ref, o_ref, db_ref, dal_ref, dds_ref):
        sel = _head_select(heads)
        ddt = lax.dot_general(z_ref[...], sel, nt, precision=HI, preferred_element_type=F32)
        g = ddt * _sigmoid(x_ref[...] + b_ref[...])
        o_ref[...] = g.astype(BF16)

        @pl.when(pl.program_id(0) == 0)
        def _():
            db_ref[...] = jnp.zeros_like(db_ref)
            da = lax.dot_general(da_ref[...], sel, nt, precision=HI, preferred_element_type=F32)
            dal_ref[...] = da * a_ref[...]
            dds_ref[...] = lax.dot_general(dd_ref[...], sel, nt, precision=HI, preferred_element_type=F32)

        db_ref[...] += jnp.sum(g, axis=0, keepdims=True)

    vec = pl.BlockSpec((1, 128), lambda i: (0, 0))
    lane = pl.BlockSpec((1, d), lambda i: (0, 0))
    return pl.pallas_call(
        body, name=name, grid=(s // tr,),
        in_specs=[pl.BlockSpec((tr, 128), lambda i: (i, 0)), vec, pl.BlockSpec((tr, d), lambda i: (i, 0)), lane, lane, vec],
        out_specs=[pl.BlockSpec((tr, 128), lambda i: (i, 0)), vec, vec, vec],
        out_shape=[jax.ShapeDtypeStruct((s, 128), BF16)] + [jax.ShapeDtypeStruct((1, 128), F32)] * 3,
        compiler_params=_cp("arbitrary"),
    )(dt_raw, dt_bias, zt, da_lane, dd_lane, a_row)


_NT = (((1,), (1,)), ((), ()))
_TN = (((0,), (0,)), ((), ()))


def _bdot(a, b, dn=None):
    if dn is None:
        return jnp.dot(a, b, preferred_element_type=F32)
    return lax.dot_general(a, b, dn, preferred_element_type=F32)


def _ssd_common(x_ref, b_ref, c_ref, dt_ref, a_ref):
    x = x_ref[...]
    dt = dt_ref[...]
    rows = lax.broadcasted_iota(jnp.int32, (BLK, BLK), 0)
    cols = lax.broadcasted_iota(jnp.int32, (BLK, BLK), 1)
    tl = (rows >= cols).astype(F32)
    acum = jnp.dot(tl, dt * a_ref[...], precision=HI, preferred_element_type=F32)
    alast = acum[BLK - 1:BLK, :]
    bm = b_ref[...].astype(BF16)
    cm = c_ref[...].astype(BF16)
    cb = _bdot(cm, bm, _NT)
    return x, dt, rows, cols, acum, alast, bm, cm, cb


def _ssd_decay(ap, apt, e, low):
    acol = ap[:, e * SSM_HEAD_DIM:e * SSM_HEAD_DIM + 1]
    arow = apt[e * SSM_HEAD_DIM:e * SSM_HEAD_DIM + 1, :]
    return jnp.where(low, jnp.exp(jnp.minimum(acol - arow, 0.0)), 0.0)


def _ssd_specs(s, d):
    gw = d // SSM_GROUPS
    bcol = d // SSM_STATE
    return gw, bcol


def _ssd_fwd(act, dte, a_lane, d_lane, *, name):
    s = act.shape[0]
    d = dte.shape[1]
    gw, bcol = _ssd_specs(s, d)
    npair = gw // 128
    nc = s // BLK

    def body(x_ref, b_ref, c_ref, dt_ref, a_ref, dsk_ref, y_ref, st_ref, ht):
        @pl.when(pl.program_id(1) == 0)
        def _():
            ht[...] = jnp.zeros_like(ht)

        x, dt, rows, cols, acum, alast, bm, cm, cb = _ssd_common(x_ref, b_ref, c_ref, dt_ref, a_ref)
        low = rows >= cols
        first = cols < SSM_HEAD_DIM
        xd = x * dt
        h_in = ht[...]
        st_ref[0] = h_in
        yoff = _bdot(cm, h_in.astype(BF16)) * jnp.exp(acum)
        parts = []
        for p in range(npair):
            ap = acum[:, p * 128:(p + 1) * 128]
            apt = ap.T
            xdp = xd[:, p * 128:(p + 1) * 128].astype(BF16)
            ys = [_bdot((cb * _ssd_decay(ap, apt, e, low)).astype(BF16), xdp) for e in range(2)]
            parts.append(jnp.where(first, ys[0], ys[1]))
        ydiag = parts[0] if npair == 1 else jnp.concatenate(parts, axis=1)
        y_ref[...] = ydiag + yoff + dsk_ref[...] * x
        w = (xd * jnp.exp(alast - acum)).astype(BF16)
        ht[...] = h_in * jnp.exp(alast) + _bdot(bm, w, _TN)

    blk = lambda g, c: (c, g)
    vec = pl.BlockSpec((1, gw), lambda g, c: (0, g))
    return pl.pallas_call(
        body, name=name, grid=(SSM_GROUPS, nc),
        in_specs=[pl.BlockSpec((BLK, gw), blk),
                  pl.BlockSpec((BLK, SSM_STATE), lambda g, c: (c, bcol + g)),
                  pl.BlockSpec((BLK, SSM_STATE), lambda g, c: (c, bcol + SSM_GROUPS + g)),
                  pl.BlockSpec((BLK, gw), blk), vec, vec],
        out_specs=[pl.BlockSpec((BLK, gw), blk), pl.BlockSpec((1, SSM_STATE, gw), lambda g, c: (c, 0, g))],
        out_shape=[jax.ShapeDtypeStruct((s, d), F32), jax.ShapeDtypeStruct((nc, SSM_STATE, d), F32)],
        scratch_shapes=[pltpu.VMEM((SSM_STATE, gw), F32)],
        compiler_params=_cp("parallel", "arbitrary"),
    )(act, act, act, dte, a_lane, d_lane)


def _ssd_bwd(act, dte, a_lane, d_lane, states, dy, *, name):
    s = act.shape[0]
    d = dte.shape[1]
    gw, bcol = _ssd_specs(s, d)
    npair = gw // 128
    nc = s // BLK
    gn = SSM_GROUPS * SSM_STATE

    def body(x_ref, b_ref, c_ref, dt_ref, a_ref, dsk_ref, st_ref, dy_ref,
             dx_ref, db_ref, dc_ref, zt_ref, dal_ref, ddl_ref, dht):
        @pl.when(pl.program_id(1) == 0)
        def _():
            dht[...] = jnp.zeros_like(dht)
            dal_ref[...] = jnp.zeros_like(dal_ref)
            ddl_ref[...] = jnp.zeros_like(ddl_ref)

        x, dt, rows, cols, acum, alast, bm, cm, cb = _ssd_common(x_ref, b_ref, c_ref, dt_ref, a_ref)
        low = rows >= cols
        first = cols < SSM_HEAD_DIM
        a = a_ref[...]
        xd = x * dt
        ea = jnp.exp(acum)
        wdec = jnp.exp(alast - acum)
        el = jnp.exp(alast)
        h_in = st_ref[0]
        hb = h_in.astype(BF16)
        g = dy_ref[...]
        dh = dht[...]
        dhb = dh.astype(BF16)

        yoff = _bdot(cm, hb) * ea
        geb = (g * ea).astype(BF16)
        dc = _bdot(geb, hb, _NT)
        u = _bdot(bm, dhb)
        wx = xd * wdec
        db = _bdot(wx.astype(BF16), dhb, _NT)
        dxd = wdec * u
        xwu = wx * u
        da_l = g * yoff - xwu
        dalast = jnp.sum(xwu, axis=0, keepdims=True) + el * jnp.sum(dh * h_in, axis=0, keepdims=True)
        dht[...] = dh * el + _bdot(cm, geb, _TN)

        dcb = jnp.zeros((BLK, BLK), F32)
        dxd_parts, col_parts = [], []
        for p in range(npair):
            ap = acum[:, p * 128:(p + 1) * 128]
            apt = ap.T
            xdp = xd[:, p * 128:(p + 1) * 128].astype(BF16)
            gp = g[:, p * 128:(p + 1) * 128]
            dxp = jnp.zeros((BLK, 128), F32)
            colsum = []
            for e in range(2):
                dec = _ssd_decay(ap, apt, e, low)
                m = cb * dec
                gpm = jnp.where(first if e == 0 else jnp.logical_not(first), gp, 0.0).astype(BF16)
                dm = _bdot(gpm, xdp, _NT)
                q = dm * m
                colsum.append(jnp.sum(q, axis=1, keepdims=True) - jnp.sum(q.T, axis=1, keepdims=True))
                dcb = dcb + dm * dec
                dxp = dxp + _bdot(m.astype(BF16), gpm, _TN)
            dxd_parts.append(dxp)
            col_parts.append(jnp.where(first, colsum[0], colsum[1]) * (1.0 / SSM_HEAD_DIM))
        cat = (lambda ps: ps[0] if npair == 1 else jnp.concatenate(ps, axis=1))
        dxd = dxd + cat(dxd_parts)
        da_l = da_l + cat(col_parts)
        rows_w = lax.broadcasted_iota(jnp.int32, (BLK, gw), 0)
        da_l = da_l + jnp.where(rows_w == BLK - 1, dalast, 0.0)
        dcbb = dcb.astype(BF16)
        dc_ref[...] = dc + _bdot(dcbb, bm)
        db_ref[...] = db + _bdot(dcbb, cm, _TN)
        tu = (rows <= cols).astype(F32)
        dda = jnp.dot(tu, da_l, precision=HI, preferred_element_type=F32)
        zt_ref[...] = dxd * x + dda * a
        dal_ref[...] += jnp.sum(dda * dt, axis=0, keepdims=True)
        ddl_ref[...] += jnp.sum(g * x, axis=0, keepdims=True)
        dx_ref[...] = dsk_ref[...] * g + dxd * dt

    blk = lambda g, c: (nc - 1 - c, g)
    vec = pl.BlockSpec((1, gw), lambda g, c: (0, g))
    bc_out = pl.BlockSpec((BLK, SSM_STATE), blk)
    return pl.pallas_call(
        body, name=name, grid=(SSM_GROUPS, nc),
        in_specs=[pl.BlockSpec((BLK, gw), blk),
                  pl.BlockSpec((BLK, SSM_STATE), lambda g, c: (nc - 1 - c, bcol + g)),
                  pl.BlockSpec((BLK, SSM_STATE), lambda g, c: (nc - 1 - c, bcol + SSM_GROUPS + g)),
                  pl.BlockSpec((BLK, gw), blk), vec, vec,
                  pl.BlockSpec((1, SSM_STATE, gw), lambda g, c: (nc - 1 - c, 0, g)),
                  pl.BlockSpec((BLK, gw), blk)],
        out_specs=[pl.BlockSpec((BLK, gw), blk), bc_out, bc_out, pl.BlockSpec((BLK, gw), blk), vec, vec],
        out_shape=[jax.ShapeDtypeStruct((s, d), F32), jax.ShapeDtypeStruct((s, gn), F32),
                   jax.ShapeDtypeStruct((s, gn), F32), jax.ShapeDtypeStruct((s, d), F32),
                   jax.ShapeDtypeStruct((1, d), F32), jax.ShapeDtypeStruct((1, d), F32)],
        scratch_shapes=[pltpu.VMEM((SSM_STATE, gw), F32)],
        compiler_params=_cp("parallel", "arbitrary"),
    )(act, act, act, dte, a_lane, d_lane, states, dy)


def _gnorm_fwd(y, proj, norm_w, *, zcol, name):
    s, d = y.shape
    tr = _tile(s, (256, 128))
    gw = d // SSM_GROUPS

    def body(y_ref, z_ref, w_ref, o_ref):
        z = z_ref[...]
        y2 = y_ref[...] * (z * _sigmoid(z))
        for g in range(SSM_GROUPS):
            sl = slice(g * gw, (g + 1) * gw)
            v = y2[:, sl]
            r = lax.rsqrt(jnp.mean(v * v, axis=-1, keepdims=True) + EPS)
            o_ref[:, sl] = (v * r * w_ref[:, sl]).astype(BF16)

    return pl.pallas_call(
        body, name=name, grid=(s // tr,),
        in_specs=[pl.BlockSpec((tr, d), lambda i: (i, 0)), pl.BlockSpec((tr, d), lambda i: (i, zcol)),
                  pl.BlockSpec((1, d), lambda i: (0, 0))],
        out_specs=pl.BlockSpec((tr, d), lambda i: (i, 0)),
        out_shape=jax.ShapeDtypeStruct((s, d), BF16),
        compiler_params=_cp("parallel"),
    )(y, proj, _row(norm_w))


def _gnorm_bwd(y, proj, norm_w, dout, *, zcol, dcol, name):
    s, d = y.shape
    tr = _tile(s, (256, 128))
    gw = d // SSM_GROUPS

    def body(y_ref, z_ref, w_ref, do_ref, dy_ref, dz_ref, dw_ref):
        @pl.when(pl.program_id(0) == 0)
        def _():
            dw_ref[...] = jnp.zeros_like(dw_ref)

        z = z_ref[...]
        yv = y_ref[...]
        sg = _sigmoid(z)
        sz = z * sg
        y2 = yv * sz
        for g in range(SSM_GROUPS):
            sl = slice(g * gw, (g + 1) * gw)
            v = y2[:, sl]
            do = do_ref[:, sl]
            r = lax.rsqrt(jnp.mean(v * v, axis=-1, keepdims=True) + EPS)
            xh = v * r
            gg = do * w_ref[:, sl]
            dy2 = r * (gg - xh * jnp.mean(gg * xh, axis=-1, keepdims=True))
            dw_ref[:, sl] += jnp.sum(do * xh, axis=0, keepdims=True)
            dy_ref[:, sl] = dy2 * sz[:, sl]
            dz_ref[:, sl] = (dy2 * yv[:, sl] * (sg[:, sl] * (1.0 + z[:, sl] * (1.0 - sg[:, sl])))).astype(BF16)

    return pl.pallas_call(
        body, name=name, grid=(s // tr,),
        in_specs=[pl.BlockSpec((tr, d), lambda i: (i, 0)), pl.BlockSpec((tr, d), lambda i: (i, zcol)),
                  pl.BlockSpec((1, d), lambda i: (0, 0)), pl.BlockSpec((tr, d), lambda i: (i, dcol))],
        out_specs=[pl.BlockSpec((tr, d), lambda i: (i, 0)), pl.BlockSpec((tr, d), lambda i: (i, 0)),
                   pl.BlockSpec((1, d), lambda i: (0, 0))],
        out_shape=[jax.ShapeDtypeStruct((s, d), F32), jax.ShapeDtypeStruct((s, d), BF16),
                   jax.ShapeDtypeStruct((1, d), F32)],
        compiler_params=_cp("arbitrary"),
    )(y, proj, _row(norm_w), dout)


def _lanes(v):
    return jnp.repeat(v.astype(F32), SSM_HEAD_DIM).reshape(1, -1)


def _pad128(v):
    return jnp.pad(v.astype(F32).reshape(1, -1), ((0, 0), (0, 128 - v.shape[-1])))


def _even_fwd(h, p, tag):
    d = h.shape[1]
    heads = d // SSM_HEAD_DIM
    hn = _rms_fwd(h, p["norm_mix"], name=tag + "_rms")
    proj = _mm(hn, p["w_in_main"], name=tag + "_in")
    pdt = _mm(hn, p["w_in_dt"], name=tag + "_indt")
    ya = _gm_fwd(proj, p["gm_ln_g"], p["gm_ln_b"], p["gm_ws"], p["gm_bs"], name=tag + "_gm")
    act = _silu_conv_fwd(proj, p["conv_w"], p["conv_b"], col0=3 * d, name=tag + "_conv")
    dte = _dt_fwd(pdt, _pad128(p["dt_bias"]), heads=heads, name=tag + "_dt")
    a = -jnp.exp(p["a_log"].astype(F32))
    y, states = _ssd_fwd(act, dte, _lanes(a), _lanes(p["d_skip"]), name=tag + "_ssd")
    yb = _gnorm_fwd(y, proj, p["ssm_norm_w"], zcol=2, name=tag + "_gn")
    cat = jnp.concatenate([ya, yb], axis=1)
    h1 = _mm(cat, p["w_out"], residual=h, name=tag + "_out")
    return h1, (h, hn, proj, pdt, act, dte, y, states, cat)


def _even_bwd(dh1, p, saved, tag):
    h, hn, proj, pdt, act, dte, y, states, cat = saved
    d = h.shape[1]
    heads = d // SSM_HEAD_DIM
    a = -jnp.exp(p["a_log"].astype(F32))
    g = {}
    dcat = _mm(dh1, p["w_out"], tb=True, name=tag + "_dcat")
    g["w_out"] = _mm(cat, dh1, ta=True, name=tag + "_dwout")
    du, dv, dlg, dlb, dws, dbs = _gm_bwd(proj, p["gm_ln_g"], p["gm_ln_b"], p["gm_ws"], p["gm_bs"], dcat, name=tag + "_gmb")
    g["gm_ln_g"] = dlg.reshape(GM_GROUPS, -1)
    g["gm_ln_b"] = dlb.reshape(GM_GROUPS, -1)
    g["gm_ws"] = dws
    g["gm_bs"] = dws_bs = dbs.reshape(GM_GROUPS, BLK)
    dy, dz, dnw = _gnorm_bwd(y, proj, p["ssm_norm_w"], dcat, zcol=2, dcol=1, name=tag + "_gnb")
    g["ssm_norm_w"] = dnw[0]
    dxs, db, dc, zt, dal, ddl = _ssd_bwd(act, dte, _lanes(a), _lanes(p["d_skip"]), states, dy, name=tag + "_ssdb")
    ddt, ddtb, dalog, ddsk = _dt_bwd(pdt, _pad128(p["dt_bias"]), zt, dal, ddl, _pad128(a), heads=heads, name=tag + "_dtb")
    g["dt_bias"] = ddtb[0, :heads]
    g["a_log"] = dalog[0, :heads]
    g["d_skip"] = ddsk[0, :heads]
    dact = jnp.concatenate([dxs, db, dc], axis=1)
    dxbc, dcw, dcb = _silu_conv_bwd(proj, p["conv_w"], p["conv_b"], dact, col0=3 * d, name=tag + "_convb")
    g["conv_w"] = dcw
    g["conv_b"] = dcb[0]
    dproj = jnp.concatenate([du, dv, dz, dxbc], axis=1)
    dhn = _mm(dproj, p["w_in_main"], tb=True, name=tag + "_dhn")
    dhn = _mm(ddt, p["w_in_dt"], tb=True, residual=dhn, name=tag + "_dhn2")
    g["w_in_main"] = _mm(hn, dproj, ta=True, name=tag + "_dwin")
    g["w_in_dt"] = _mm(hn, ddt, ta=True, name=tag + "_dwdt")
    dh, dnm = _rms_bwd(h, p["norm_mix"], dhn, add=dh1, name=tag + "_rmsb")
    g["norm_mix"] = dnm[0]
    return dh, g


def _ffn_fwd(h, p, tag):
    hn = _rms_fwd(h, p["norm_ffn"], name=tag + "_rms")
    up = _mm(hn, p["w_up"], name=tag + "_up")
    a = _ffn_mid_fwd(up, p["conv_w"], p["conv_b"], name=tag + "_mid")
    h2 = _mm(a, p["w_down"], residual=h, name=tag + "_down")
    return h2, (h, hn, up, a)


def _ffn_bwd(dh2, p, saved, tag):
    h, hn, up, a = saved
    g = {}
    da = _mm(dh2, p["w_down"], tb=True, name=tag + "_da")
    g["w_down"] = _mm(a, dh2, ta=True, name=tag + "_dwdown")
    dg, dv, dcw, dcb = _ffn_mid_bwd(up, p["conv_w"], p["conv_b"], da, name=tag + "_midb")
    g["conv_w"] = dcw
    g["conv_b"] = dcb[0]
    dup = jnp.concatenate([dg, dv], axis=1)
    dhn = _mm(dup, p["w_up"], tb=True, name=tag + "_dhn")
    g["w_up"] = _mm(hn, dup, ta=True, name=tag + "_dwup")
    dh, dnw = _rms_bwd(h, p["norm_ffn"], dhn, add=dh2, name=tag + "_rmsb")
    g["norm_ffn"] = dnw[0]
    return dh, g


def _rope(x, cos_p, sin_p):
    half = MLA_ROPE // 2
    lane = lax.broadcasted_iota(jnp.int32, x.shape, 1)
    swapped = jnp.where(lane < half, pltpu.roll(x, 128 - half, 1), pltpu.roll(x, half, 1))
    return x * cos_p + swapped * sin_p


def _rope_t(g, cos_p, sin_p):
    half = MLA_ROPE // 2
    gs = g * sin_p
    lane = lax.broadcasted_iota(jnp.int32, g.shape, 1)
    swapped = jnp.where(lane < half, pltpu.roll(gs, 128 - half, 1), pltpu.roll(gs, half, 1))
    return g * cos_p + swapped


def _attn_probs(qn_ref, qp_ref, kn_ref, kp_ref, cq_ref, sq_ref, ck_ref, sk_ref, tq):
    s = kn_ref.shape[0]
    scale = (MLA_NOPE + MLA_ROPE) ** -0.5
    qn = qn_ref[...].astype(BF16)
    qp = _rope(qp_ref[...], cq_ref[...], sq_ref[...]).astype(BF16)
    kn = kn_ref[...].astype(BF16)
    kp = _rope(kp_ref[...], ck_ref[...], sk_ref[...]).astype(BF16)
    sc = (_bdot(qn, kn, _NT) + _bdot(qp, kp, _NT)) * scale
    qpos = pl.program_id(1) * tq + lax.broadcasted_iota(jnp.int32, (tq, s), 0)
    kpos = lax.broadcasted_iota(jnp.int32, (tq, s), 1)
    sc = jnp.where(kpos // CHUNK <= qpos // CHUNK, sc, -jnp.inf)
    sc = sc - jnp.max(sc, axis=-1, keepdims=True)
    e = jnp.exp(sc)
    p = e / jnp.sum(e, axis=-1, keepdims=True)
    return p, qn, qp, kn, kp, scale


def _attn_in_specs(s, tq, kr_col):
    return [pl.BlockSpec((tq, 128), lambda h, i: (i, 2 * h)), pl.BlockSpec((tq, 128), lambda h, i: (i, 2 * h + 1)),
            pl.BlockSpec((s, 128), lambda h, i: (0, 2 * h)), pl.BlockSpec((s, 128), lambda h, i: (0, kr_col)),
            pl.BlockSpec((tq, 128), lambda h, i: (i, 0)), pl.BlockSpec((tq, 128), lambda h, i: (i, 0)),
            pl.BlockSpec((s, 128), lambda h, i: (0, 0)), pl.BlockSpec((s, 128), lambda h, i: (0, 0)),
            pl.BlockSpec((s, 128), lambda h, i: (0, 2 * h + 1))]


def _attn_fwd(q, kv, proj, cos_p, sin_p, *, kr_col, name):
    s = q.shape[0]
    heads = q.shape[1] // 256
    tq = _tile(s, (256, 128))

    def body(qn_ref, qp_ref, kn_ref, kp_ref, cq_ref, sq_ref, ck_ref, sk_ref, v_ref, o_ref):
        p = _attn_probs(qn_ref, qp_ref, kn_ref, kp_ref, cq_ref, sq_ref, ck_ref, sk_ref, tq)[0]
        o_ref[...] = _bdot(p.astype(BF16), v_ref[...].astype(BF16)).astype(BF16)

    return pl.pallas_call(
        body, name=name, grid=(heads, s // tq),
        in_specs=_attn_in_specs(s, tq, kr_col),
        out_specs=pl.BlockSpec((tq, 128), lambda h, i: (i, h)),
        out_shape=jax.ShapeDtypeStruct((s, heads * MLA_V), BF16),
        compiler_params=_cp("parallel", "parallel"),
    )(q, q, kv, proj, cos_p, sin_p, cos_p, sin_p, kv)


def _attn_bwd(q, kv, proj, cos_p, sin_p, do, *, kr_col, name):
    s = q.shape[0]
    heads = q.shape[1] // 256
    tq = _tile(s, (256, 128))

    def body(qn_ref, qp_ref, kn_ref, kp_ref, cq_ref, sq_ref, ck_ref, sk_ref, v_ref, do_ref, dq_ref, dkv_ref, dkp_ref):
        h = pl.program_id(0)
        i = pl.program_id(1)

        @pl.when(i == 0)
        def _():
            dkv_ref[...] = jnp.zeros_like(dkv_ref)

        @pl.when(jnp.logical_and(h == 0, i == 0))
        def _():
            dkp_ref[...] = jnp.zeros_like(dkp_ref)

        p, qn, qp, kn, kp, scale = _attn_probs(qn_ref, qp_ref, kn_ref, kp_ref, cq_ref, sq_ref, ck_ref, sk_ref, tq)
        dob = do_ref[...].astype(BF16)
        pb = p.astype(BF16)
        dv = _bdot(pb, dob, _TN)
        dp = _bdot(dob, v_ref[...].astype(BF16), _NT)
        ds = (p * (dp - jnp.sum(dp * p, axis=-1, keepdims=True)) * scale).astype(BF16)
        dq_ref[:, 0:128] = _bdot(ds, kn).astype(BF16)
        dq_ref[:, 128:256] = _rope_t(_bdot(ds, kp), cq_ref[...], sq_ref[...]).astype(BF16)
        dkv_ref[:, 0:128] += _bdot(ds, qn, _TN)
        dkv_ref[:, 128:256] += dv
        dkp_ref[...] += _rope_t(_bdot(ds, qp, _TN), ck_ref[...], sk_ref[...])

    return pl.pallas_call(
        body, name=name, grid=(heads, s // tq),
        in_specs=_attn_in_specs(s, tq, kr_col) + [pl.BlockSpec((tq, 128), lambda h, i: (i, h))],
        out_specs=[pl.BlockSpec((tq, 256), lambda h, i: (i, h)), pl.BlockSpec((s, 256), lambda h, i: (0, h)),
                   pl.BlockSpec((s, 128), lambda h, i: (0, 0))],
        out_shape=[jax.ShapeDtypeStruct((s, heads * 256), BF16), jax.ShapeDtypeStruct((s, heads * 256), F32),
                   jax.ShapeDtypeStruct((s, 128), F32)],
        compiler_params=_cp("arbitrary", "arbitrary"),
    )(q, q, kv, proj, cos_p, sin_p, cos_p, sin_p, kv, do)


def _rope_tables(positions):
    inv_freq = ROPE_THETA ** (-jnp.arange(0, MLA_ROPE, 2, dtype=F32) / MLA_ROPE)
    ang = positions.astype(F32)[:, None] * inv_freq
    cos, sin = jnp.cos(ang), jnp.sin(ang)
    zero = jnp.zeros((positions.shape[0], 128 - MLA_ROPE), F32)
    return jnp.concatenate([cos, cos, zero], axis=1), jnp.concatenate([-sin, sin, zero], axis=1)


def _odd_fwd(h, p, cos_p, sin_p, tag):
    rank = p["q_norm"].shape[0]
    hn = _rms_fwd(h, p["norm_mix"], name=tag + "_rms")
    proj = _mm(hn, p["w_in"], name=tag + "_in")
    cqn = _rms_fwd(proj, p["q_norm"], width=rank, col=0, name=tag + "_qn")
    ckvn = _rms_fwd(proj, p["kv_norm"], width=rank, col=1, name=tag + "_kvn")
    q = _mm(cqn, p["w_uq"], name=tag + "_uq")
    kv = _mm(ckvn, p["w_ukv"], name=tag + "_ukv")
    o = _attn_fwd(q, kv, proj, cos_p, sin_p, kr_col=2 * rank // 128, name=tag + "_attn")
    h1 = _mm(o, p["w_o"], residual=h, name=tag + "_o")
    return h1, (h, hn, proj, cqn, ckvn, q, kv, o)


def _odd_bwd(dh1, p, cos_p, sin_p, saved, tag):
    h, hn, proj, cqn, ckvn, q, kv, o = saved
    rank = p["q_norm"].shape[0]
    g = {}
    do = _mm(dh1, p["w_o"], tb=True, name=tag + "_do")
    g["w_o"] = _mm(o, dh1, ta=True, name=tag + "_dwo")
    dq, dkv, dkp = _attn_bwd(q, kv, proj, cos_p, sin_p, do, kr_col=2 * rank // 128, name=tag + "_attnb")
    dcqn = _mm(dq, p["w_uq"], tb=True, name=tag + "_dcqn")
    g["w_uq"] = _mm(cqn, dq, ta=True, name=tag + "_dwuq")
    dckvn = _mm(dkv, p["w_ukv"], tb=True, name=tag + "_dckvn")
    g["w_ukv"] = _mm(ckvn, dkv, ta=True, name=tag + "_dwukv")
    dcq, dqn = _rms_bwd(proj, p["q_norm"], dcqn, width=rank, col=0, out_dtype=BF16, name=tag + "_qnb")
    dckv, dkvn = _rms_bwd(proj, p["kv_norm"], dckvn, width=rank, col=1, out_dtype=BF16, name=tag + "_kvnb")
    g["q_norm"] = dqn[0]
    g["kv_norm"] = dkvn[0]
    dproj = jnp.concatenate([dcq, dckv, dkp.astype(BF16)], axis=1)
    dhn = _mm(dproj, p["w_in"], tb=True, name=tag + "_dhn")
    g["w_in"] = _mm(hn, dproj, ta=True, name=tag + "_dwin")
    dh, dnm = _rms_bwd(h, p["norm_mix"], dhn, add=dh1, name=tag + "_rmsb")
    g["norm_mix"] = dnm[0]
    return dh, g


PACK_W = 1024
N_CHIPS = 4
_MESH = pl.DeviceIdType.MESH
_ANY = pl.BlockSpec(memory_space=pl.ANY)


def _place():
    x, y, c = lax.axis_index("x"), lax.axis_index("y"), lax.axis_index("c")
    others = [(1 - x, y), (x, 1 - y), (1 - x, 1 - y)]
    return x, y, c, others


def _all_gather_shards(shard, *, name):
    _, r, w = shard.shape

    def body(w_ref, out_ref, send_sems, recv_sems, local_sem):
        x, y, c, others = _place()
        k = 2 * x + y
        sibling = (x, y, 1 - c)
        mine = pltpu.make_async_copy(w_ref, out_ref.at[k], local_sem)
        mine.start()

        def from_chip(j, half):
            cx, cy = others[j]
            return pltpu.make_async_remote_copy(
                src_ref=w_ref.at[half], dst_ref=out_ref.at[2 * cx + cy, half],
                send_sem=send_sems.at[j], recv_sem=recv_sems.at[j], device_id=(cx, cy, c), device_id_type=_MESH)

        first = []
        for j, (cx, cy) in enumerate(others):
            first.append(pltpu.make_async_remote_copy(
                src_ref=w_ref.at[c], dst_ref=out_ref.at[k, c],
                send_sem=send_sems.at[j], recv_sem=recv_sems.at[j], device_id=(cx, cy, c), device_id_type=_MESH))
        for cp in first:
            cp.start()
        passed = []
        for j, (cx, cy) in enumerate(others):
            from_chip(j, c).wait_recv()
            blk = out_ref.at[2 * cx + cy, c]
            fw = pltpu.make_async_remote_copy(
                src_ref=blk, dst_ref=blk, send_sem=send_sems.at[3 + j], recv_sem=recv_sems.at[3 + j],
                device_id=sibling, device_id_type=_MESH)
            fw.start()
            passed.append(fw)
        for j, (cx, cy) in enumerate(others):
            blk = out_ref.at[2 * cx + cy, 1 - c]
            pltpu.make_async_remote_copy(
                src_ref=blk, dst_ref=blk, send_sem=send_sems.at[3 + j], recv_sem=recv_sems.at[3 + j],
                device_id=sibling, device_id_type=_MESH).wait_recv()
        for cp in first + passed:
            cp.wait_send()
        mine.wait()

    return pl.pallas_call(
        body, name=name,
        in_specs=[_ANY], out_specs=_ANY,
        out_shape=jax.ShapeDtypeStruct((N_CHIPS, 2, r, w), shard.dtype),
        scratch_shapes=[pltpu.SemaphoreType.DMA((6,)), pltpu.SemaphoreType.DMA((6,)), pltpu.SemaphoreType.DMA],
    )(shard)


def _pair_exchange(g, *, name):
    _, n, r, w = g.shape

    def body(g_ref, a_ref, send_sem, recv_sem):
        x, y, c, _ = _place()
        cp = pltpu.make_async_remote_copy(src_ref=g_ref.at[1 - c], dst_ref=a_ref, send_sem=send_sem, recv_sem=recv_sem,
                                          device_id=(x, y, 1 - c), device_id_type=_MESH)
        cp.start()
        cp.wait()

    return pl.pallas_call(
        body, name=name, in_specs=[_ANY], out_specs=_ANY,
        out_shape=jax.ShapeDtypeStruct((n, r, w), g.dtype),
        scratch_shapes=[pltpu.SemaphoreType.DMA, pltpu.SemaphoreType.DMA],
    )(g)


def _chip_exchange(t, *, name):
    _, r, w = t.shape

    def body(t_ref, b_ref, send_sems, recv_sems):
        x, y, c, others = _place()
        copies = []
        for j, (cx, cy) in enumerate(others):
            copies.append(pltpu.make_async_remote_copy(
                src_ref=t_ref.at[2 * cx + cy], dst_ref=b_ref.at[j], send_sem=send_sems.at[j], recv_sem=recv_sems.at[j],
                device_id=(cx, cy, c), device_id_type=_MESH))
        for cp in copies:
            cp.start()
        for cp in copies:
            cp.wait()

    return pl.pallas_call(
        body, name=name, in_specs=[_ANY], out_specs=_ANY,
        out_shape=jax.ShapeDtypeStruct((N_CHIPS - 1, r, w), t.dtype),
        scratch_shapes=[pltpu.SemaphoreType.DMA((3,)), pltpu.SemaphoreType.DMA((3,))],
    )(t)


def _pair_share(rsum, *, name):
    r, w = rsum.shape

    def body(r_ref, out_ref, send_sem, recv_sem, local_sem):
        x, y, c, _ = _place()
        loc = pltpu.make_async_copy(r_ref, out_ref.at[c], local_sem)
        loc.start()
        cp = pltpu.make_async_remote_copy(src_ref=r_ref, dst_ref=out_ref.at[c], send_sem=send_sem, recv_sem=recv_sem,
                                          device_id=(x, y, 1 - c), device_id_type=_MESH)
        cp.start()
        pltpu.make_async_remote_copy(src_ref=r_ref, dst_ref=out_ref.at[1 - c], send_sem=send_sem, recv_sem=recv_sem,
                                     device_id=(x, y, 1 - c), device_id_type=_MESH).wait_recv()
        cp.wait_send()
        loc.wait()

    return pl.pallas_call(
        body, name=name, in_specs=[_ANY], out_specs=_ANY,
        out_shape=jax.ShapeDtypeStruct((2, r, w), rsum.dtype),
        scratch_shapes=[pltpu.SemaphoreType.DMA, pltpu.SemaphoreType.DMA, pltpu.SemaphoreType.DMA],
    )(rsum)


def _pair_add(g, a, ck, *, name):
    _, n, r, w = g.shape
    tr = _tile(r, (512, 256, 128, 8))

    def body(ck_ref, g_ref, a_ref, t_ref, own_ref):
        v = g_ref[...] + a_ref[...]
        t_ref[...] = v.astype(BF16)

        @pl.when(pl.program_id(1) == ck_ref[1])
        def _():
            own_ref[...] = v

    return pl.pallas_call(
        body, name=name,
        grid_spec=pltpu.PrefetchScalarGridSpec(
            num_scalar_prefetch=1, grid=(r // tr, n),
            in_specs=[pl.BlockSpec((None, None, tr, w), lambda i, j, s: (s[0], j, i, 0)),
                      pl.BlockSpec((None, tr, w), lambda i, j, s: (j, i, 0))],
            out_specs=[pl.BlockSpec((None, tr, w), lambda i, j, s: (j, i, 0)),
                       pl.BlockSpec((tr, w), lambda i, j, s: (i, 0))]),
        out_shape=[jax.ShapeDtypeStruct((n, r, w), BF16), jax.ShapeDtypeStruct((r, w), F32)],
        compiler_params=_cp("arbitrary", "arbitrary"),
    )(ck, g, a)


def _chip_add(own, b, *, name):
    r, w = own.shape
    tr = _tile(r, (512, 256, 128, 8))

    def body(o_ref, b_ref, r_ref):
        acc = o_ref[...]
        for j in range(N_CHIPS - 1):
            acc = acc + b_ref[j].astype(F32)
        r_ref[...] = acc

    return pl.pallas_call(
        body, name=name, grid=(r // tr,),
        in_specs=[pl.BlockSpec((tr, w), lambda i: (i, 0)), pl.BlockSpec((N_CHIPS - 1, tr, w), lambda i: (0, i, 0))],
        out_specs=pl.BlockSpec((tr, w), lambda i: (i, 0)),
        out_shape=jax.ShapeDtypeStruct((r, w), F32),
        compiler_params=_cp("parallel"),
    )(own, b)


def _all_reduce_small(v, *, name):
    r, w = v.shape
    n_dev = 8

    def body(x_ref, sum_ref, out_ref, send_sems, recv_sems, local_sem):
        x, y, c, others = _place()
        me, sibling = (x, y, c), (x, y, 1 - c)

        def rows(px, py, pc):
            return out_ref.at[pl.ds((4 * px + 2 * py + pc) * r, r), :]

        def copy(k, block, to, src=None):
            return pltpu.make_async_remote_copy(
                src_ref=rows(*block) if src is None else src, dst_ref=rows(*block),
                send_sem=send_sems.at[k], recv_sem=recv_sems.at[k], device_id=to, device_id_type=_MESH)

        mine = pltpu.make_async_copy(x_ref, rows(*me), local_sem)
        mine.start()
        first = [copy(0, me, sibling, src=x_ref)]
        first += [copy(1 + j, me, (*chip, c), src=x_ref) for j, chip in enumerate(others)]
        for cp in first:
            cp.start()
        passed = [copy(4 + j, (*chip, c), sibling) for j, chip in enumerate(others)]
        for j, chip in enumerate(others):
            copy(1 + j, (*chip, c), me).wait_recv()
            passed[j].start()
        copy(0, sibling, me).wait_recv()
        for j, chip in enumerate(others):
            copy(4 + j, (*chip, 1 - c), me).wait_recv()
        for cp in first + passed:
            cp.wait_send()
        mine.wait()
        acc = out_ref[pl.ds(0, r), :]
        for dev in range(1, n_dev):
            acc = acc + out_ref[pl.ds(dev * r, r), :]
        sum_ref[...] = acc

    vmem = pl.BlockSpec(memory_space=pltpu.VMEM)
    return pl.pallas_call(
        body, name=name, in_specs=[vmem], out_specs=[vmem, vmem],
        out_shape=[jax.ShapeDtypeStruct((r, w), F32), jax.ShapeDtypeStruct((n_dev * r, w), F32)],
        scratch_shapes=[pltpu.SemaphoreType.DMA((7,)), pltpu.SemaphoreType.DMA((7,)), pltpu.SemaphoreType.DMA],
        compiler_params=pltpu.CompilerParams(vmem_limit_bytes=V7X_VMEM_LIMIT),
    )(v)[0]


def _adamw(w, g, m, v, *, name):
    shape = w.shape
    cols = shape[-1]
    rows = max(1, math.prod(shape[:-1]))
    tr = rows
    for cand in (512, 256, 128, 64, 32, 16, 8):
        if rows % cand == 0 and cand * cols * 4 <= 2 * 1024 * 1024:
            tr = cand
            break
    c1 = 1.0 - ADAM_B1 ** ADAM_STEP
    c2 = 1.0 - ADAM_B2 ** ADAM_STEP

    def body(w_ref, g_ref, m_ref, v_ref, d_ref, mo_ref, vo_ref):
        gv = g_ref[...]
        mn = ADAM_B1 * m_ref[...] + (1.0 - ADAM_B1) * gv
        vn = ADAM_B2 * v_ref[...] + (1.0 - ADAM_B2) * (gv * gv)
        d_ref[...] = -ADAM_LR * ((mn / c1) / (jnp.sqrt(vn / c2) + ADAM_EPS) + ADAM_WD * w_ref[...])
        mo_ref[...] = mn
        vo_ref[...] = vn

    spec = pl.BlockSpec((tr, cols), lambda i: (i, 0))
    outs = pl.pallas_call(
        body, name=name, grid=(rows // tr,),
        in_specs=[spec] * 4, out_specs=[spec] * 3,
        out_shape=[jax.ShapeDtypeStruct((rows, cols), F32)] * 3,
        compiler_params=_cp("parallel"),
    )(*[t.reshape(rows, cols) for t in (w, g, m, v)])
    return [o.reshape(shape) for o in outs]


_WEIGHTS = ["norm_mix", "norm_ffn", "norm_final", "ev_w_in", "ev_gm_ln_g", "ev_gm_ln_b", "ev_gm_ws", "ev_gm_bs",
            "ev_conv_w", "ev_conv_b", "ev_dt_bias", "ev_a_log", "ev_d_skip", "ev_ssm_norm_w", "ev_w_out", "od_w_in",
            "od_q_norm", "od_kv_norm", "od_w_uq", "od_w_ukv", "od_w_o", "ff_w_up", "ff_conv_w", "ff_conv_b", "ff_w_down"]
_BIG = {"ev_w_in": -1, "ev_w_out": -2, "od_w_in": -2, "od_w_uq": -1, "od_w_ukv": -1, "od_w_o": -2,
        "ff_w_up": -1, "ff_w_down": -2}
_SMALL = {"ev_gm_ln_g": -1, "ev_gm_ln_b": -1, "ev_conv_w": -1, "od_q_norm": -1, "od_kv_norm": -1, "ff_conv_w": -1}
_SHARDED = {**_BIG, **_SMALL}
_REPLICATED = [n for n in _WEIGHTS if n not in _SHARDED]
N_CHUNKS = 4


def _from_slabs(slabs, axis):
    t = jnp.moveaxis(slabs, 0, axis - 1)
    shape = list(t.shape)
    if axis == -1:
        return t.reshape(shape[:-2] + [shape[-2] * shape[-1]])
    return t.reshape(shape[:-3] + [shape[-3] * shape[-2], shape[-1]])


def _to_slabs(full, axis):
    shape = list(full.shape)
    if axis == -1:
        t = full.reshape(shape[:-1] + [N_CHIPS, shape[-1] // N_CHIPS])
    else:
        t = full.reshape(shape[:-2] + [N_CHIPS, shape[-2] // N_CHIPS, shape[-1]])
    return jnp.moveaxis(t, axis - 1, 0)


def _half_rows(n_elems, n_chunks):
    rows = -(-n_elems // (PACK_W * 2 * n_chunks))
    unit = 512 if rows > 512 else 16
    return -(-rows // unit) * unit


def _gather(shards, dtype, n_chunks, tag):
    n = sum(a.size for a in shards)
    mh = _half_rows(n, n_chunks)
    flat = jnp.concatenate([a.astype(dtype).reshape(-1) for a in shards])
    flat = jnp.pad(flat, (0, n_chunks * 2 * mh * PACK_W - n)).reshape(n_chunks, 2, mh, PACK_W)
    outs = [_all_gather_shards(flat[i], name=f"{tag}{i}") for i in range(n_chunks)]
    full = jnp.stack(outs, axis=1).reshape(N_CHIPS, -1)
    res, off = [], 0
    for a in shards:
        res.append(full[:, off:off + a.size].reshape((N_CHIPS,) + a.shape))
        off += a.size
    return res


def _reduce_scatter(slabs, n_chunks, tag):
    n = sum(a[0].size for a in slabs)
    mh = _half_rows(n, n_chunks)
    flat = jnp.concatenate([a.reshape(N_CHIPS, -1) for a in slabs], axis=1)
    flat = jnp.pad(flat, ((0, 0), (0, n_chunks * 2 * mh * PACK_W - n)))
    g = flat.reshape(N_CHIPS, n_chunks, 2, mh, PACK_W).transpose(1, 2, 0, 3, 4)
    ck = jnp.stack([lax.axis_index("c"), 2 * lax.axis_index("x") + lax.axis_index("y")]).astype(jnp.int32)
    outs = []
    for i in range(n_chunks):
        a = _pair_exchange(g[i], name=f"{tag}_px{i}")
        t, own = _pair_add(g[i], a, ck, name=f"{tag}_pa{i}")
        b = _chip_exchange(t, name=f"{tag}_cx{i}")
        r = _chip_add(own, b, name=f"{tag}_ca{i}")
        outs.append(_pair_share(r, name=f"{tag}_ps{i}"))
    full = jnp.stack(outs, axis=0).reshape(-1)
    res, off = [], 0
    for a in slabs:
        res.append(full[off:off + a[0].size].reshape(a.shape[1:]))
        off += a[0].size
    return res


def _all_reduce(arrs, tag):
    n = sum(a.size for a in arrs)
    rows = -(-n // PACK_W)
    rows = -(-rows // 8) * 8
    flat = jnp.concatenate([a.astype(F32).reshape(-1) for a in arrs])
    flat = jnp.pad(flat, (0, rows * PACK_W - n)).reshape(rows, PACK_W)
    tot = _all_reduce_small(flat, name=tag).reshape(-1)
    res, off = [], 0
    for a in arrs:
        res.append(tot[off:off + a.size].reshape(a.shape))
        off += a.size
    return res


def _pad_cols(w, cols):
    return jnp.pad(w, ((0, 0), (0, cols - w.shape[1])))


def _layer_params(full, layer):
    j = layer // 2
    p = {"norm_mix": full["norm_mix"][layer]}
    if layer % 2 == 0:
        w_in = full["ev_w_in"][j]
        d = w_in.shape[0]
        main = 4 * d + 2 * SSM_GROUPS * SSM_STATE
        p.update(w_in_main=w_in[:, :main], w_in_dt=_pad_cols(w_in[:, main:], 128),
                 gm_ln_g=full["ev_gm_ln_g"][j], gm_ln_b=full["ev_gm_ln_b"][j], gm_ws=full["ev_gm_ws"][j],
                 gm_bs=full["ev_gm_bs"][j], conv_w=full["ev_conv_w"][j], conv_b=full["ev_conv_b"][j],
                 dt_bias=full["ev_dt_bias"][j], a_log=full["ev_a_log"][j], d_skip=full["ev_d_skip"][j],
                 ssm_norm_w=full["ev_ssm_norm_w"][j], w_out=full["ev_w_out"][j])
    else:
        w_uq = full["od_w_uq"][j]
        rank = w_uq.shape[0]
        heads = w_uq.shape[1] // (MLA_NOPE + MLA_ROPE)
        w_uq = jnp.pad(w_uq.reshape(rank, heads, MLA_NOPE + MLA_ROPE), ((0, 0), (0, 0), (0, 256 - MLA_NOPE - MLA_ROPE)))
        w_in = full["od_w_in"][j]
        p.update(w_in=_pad_cols(w_in, -(-w_in.shape[1] // 128) * 128), q_norm=full["od_q_norm"][j],
                 kv_norm=full["od_kv_norm"][j], w_uq=w_uq.reshape(rank, heads * 256), w_ukv=full["od_w_ukv"][j],
                 w_o=full["od_w_o"][j])
    f = {"norm_ffn": full["norm_ffn"][layer], "w_up": full["ff_w_up"][layer], "conv_w": full["ff_conv_w"][layer],
         "conv_b": full["ff_conv_b"][layer], "w_down": full["ff_w_down"][layer]}
    return p, f


def _layer_grads(g, gf, layer, full):
    out = {"norm_mix": g["norm_mix"], "norm_ffn": gf["norm_ffn"], "ff_w_up": gf["w_up"], "ff_conv_w": gf["conv_w"],
           "ff_conv_b": gf["conv_b"], "ff_w_down": gf["w_down"]}
    if layer % 2 == 0:
        heads = full["ev_dt_bias"].shape[1]
        out.update(ev_w_in=jnp.concatenate([g["w_in_main"], g["w_in_dt"][:, :heads]], axis=1),
                   ev_gm_ln_g=g["gm_ln_g"], ev_gm_ln_b=g["gm_ln_b"], ev_gm_ws=g["gm_ws"], ev_gm_bs=g["gm_bs"],
                   ev_conv_w=g["conv_w"], ev_conv_b=g["conv_b"], ev_dt_bias=g["dt_bias"], ev_a_log=g["a_log"],
                   ev_d_skip=g["d_skip"], ev_ssm_norm_w=g["ssm_norm_w"], ev_w_out=g["w_out"])
    else:
        rank = g["w_uq"].shape[0]
        heads = g["w_uq"].shape[1] // 256
        w_uq = g["w_uq"].reshape(rank, heads, 256)[:, :, :MLA_NOPE + MLA_ROPE].reshape(rank, heads * (MLA_NOPE + MLA_ROPE))
        out.update(od_w_in=g["w_in"][:, :full["od_w_in"].shape[2]], od_q_norm=g["q_norm"], od_kv_norm=g["kv_norm"],
                   od_w_uq=w_uq, od_w_ukv=g["w_ukv"], od_w_o=g["w_o"])
    return out


def _step(x, positions, loss_target, w, m, v):
    depth = w["norm_mix"].shape[0]
    h = x[0]
    tgt = loss_target[0]
    cos_p, sin_p = _rope_tables(positions[0])

    big = _gather([w[n] for n in _BIG], BF16, N_CHUNKS, "ag")
    small = _gather([w[n] for n in _SMALL], F32, 1, "ags")
    full = {n: w[n] for n in _REPLICATED}
    for n, slabs in zip(list(_BIG) + list(_SMALL), big + small):
        full[n] = _from_slabs(slabs, _SHARDED[n])

    params, saved = [], []
    for layer in range(depth):
        p, f = _layer_params(full, layer)
        if layer % 2 == 0:
            h, sv = _even_fwd(h, p, f"l{layer}m")
        else:
            h, sv = _odd_fwd(h, p, cos_p, sin_p, f"l{layer}m")
        h, svf = _ffn_fwd(h, f, f"l{layer}f")
        params.append((p, f))
        saved.append((sv, svf))

    loss, dh, dnf = _loss_bwd(h, w["norm_final"], tgt, name="loss")
    per_layer = []
    for layer in reversed(range(depth)):
        p, f = params[layer]
        sv, svf = saved[layer]
        dh, gf = _ffn_bwd(dh, f, svf, f"l{layer}fb")
        if layer % 2 == 0:
            dh, g = _even_bwd(dh, p, sv, f"l{layer}mb")
        else:
            dh, g = _odd_bwd(dh, p, cos_p, sin_p, sv, f"l{layer}mb")
        per_layer.append((layer, _layer_grads(g, gf, layer, w)))
    per_layer.sort(key=lambda t: t[0])
    local = {"norm_final": dnf[0]}
    for n in _WEIGHTS:
        if n == "norm_final":
            continue
        rows = [lg[n] for _, lg in per_layer if n in lg]
        local[n] = jnp.stack(rows, axis=0)

    red = _reduce_scatter([_to_slabs(local[n], _SHARDED[n]) for n in _SHARDED], N_CHUNKS, "rs")
    grads = dict(zip(_SHARDED, red))
    grads.update(zip(_REPLICATED, _all_reduce([local[n] for n in _REPLICATED], "ar")))
    loss = lax.psum(loss[0, 0], ("x", "y", "c"))

    delta, new_m, new_v = {}, {}, {}
    for n in _WEIGHTS:
        grads[n] = grads[n].reshape(w[n].shape)
        delta[n], new_m[n], new_v[n] = _adamw(w[n], grads[n], m[n], v[n], name="adamw_" + n)
    return (loss, dh[None], *[grads[n] for n in _WEIGHTS], *[delta[n] for n in _WEIGHTS],
            *[new_m[n] for n in _WEIGHTS], *[new_v[n] for n in _WEIGHTS])


def kernel(x, positions, norm_mix, norm_ffn, norm_final, ev_w_in, ev_gm_ln_g, ev_gm_ln_b, ev_gm_ws, ev_gm_bs, ev_conv_w, ev_conv_b, ev_dt_bias, ev_a_log, ev_d_skip, ev_ssm_norm_w, ev_w_out, od_w_in, od_q_norm, od_kv_norm, od_w_uq, od_w_ukv, od_w_o, ff_w_up, ff_conv_w, ff_conv_b, ff_w_down, loss_target, m_norm_mix, m_norm_ffn, m_norm_final, m_ev_w_in, m_ev_gm_ln_g, m_ev_gm_ln_b, m_ev_gm_ws, m_ev_gm_bs, m_ev_conv_w, m_ev_conv_b, m_ev_dt_bias, m_ev_a_log, m_ev_d_skip, m_ev_ssm_norm_w, m_ev_w_out, m_od_w_in, m_od_q_norm, m_od_kv_norm, m_od_w_uq, m_od_w_ukv, m_od_w_o, m_ff_w_up, m_ff_conv_w, m_ff_conv_b, m_ff_w_down, v_norm_mix, v_norm_ffn, v_norm_final, v_ev_w_in, v_ev_gm_ln_g, v_ev_gm_ln_b, v_ev_gm_ws, v_ev_gm_bs, v_ev_conv_w, v_ev_conv_b, v_ev_dt_bias, v_ev_a_log, v_ev_d_skip, v_ev_ssm_norm_w, v_ev_w_out, v_od_w_in, v_od_q_norm, v_od_kv_norm, v_od_w_uq, v_od_w_ukv, v_od_w_o, v_ff_w_up, v_ff_conv_w, v_ff_conv_b, v_ff_w_down):
    ws = (norm_mix, norm_ffn, norm_final, ev_w_in, ev_gm_ln_g, ev_gm_ln_b, ev_gm_ws, ev_gm_bs, ev_conv_w, ev_conv_b, ev_dt_bias, ev_a_log, ev_d_skip, ev_ssm_norm_w, ev_w_out, od_w_in, od_q_norm, od_kv_norm, od_w_uq, od_w_ukv, od_w_o, ff_w_up, ff_conv_w, ff_conv_b, ff_w_down)
    ms = (m_norm_mix, m_norm_ffn, m_norm_final, m_ev_w_in, m_ev_gm_ln_g, m_ev_gm_ln_b, m_ev_gm_ws, m_ev_gm_bs, m_ev_conv_w, m_ev_conv_b, m_ev_dt_bias, m_ev_a_log, m_ev_d_skip, m_ev_ssm_norm_w, m_ev_w_out, m_od_w_in, m_od_q_norm, m_od_kv_norm, m_od_w_uq, m_od_w_ukv, m_od_w_o, m_ff_w_up, m_ff_conv_w, m_ff_conv_b, m_ff_w_down)
    vs = (v_norm_mix, v_norm_ffn, v_norm_final, v_ev_w_in, v_ev_gm_ln_g, v_ev_gm_ln_b, v_ev_gm_ws, v_ev_gm_bs, v_ev_conv_w, v_ev_conv_b, v_ev_dt_bias, v_ev_a_log, v_ev_d_skip, v_ev_ssm_norm_w, v_ev_w_out, v_od_w_in, v_od_q_norm, v_od_kv_norm, v_od_w_uq, v_od_w_ukv, v_od_w_o, v_ff_w_up, v_ff_conv_w, v_ff_conv_b, v_ff_w_down)
    return _step(x, positions, loss_target, dict(zip(_WEIGHTS, ws)), dict(zip(_WEIGHTS, ms)), dict(zip(_WEIGHTS, vs)))
```

```python
import functools
import math

import jax
import jax.numpy as jnp
from jax import lax
from jax.experimental import pallas as pl
from jax.experimental.pallas import tpu as pltpu

F32 = jnp.float32
BF16 = jnp.bfloat16
EPS = 1e-6
CHUNK = 64
BLK = 128
GM_GROUPS = 8
SSM_HEAD_DIM = 64
SSM_GROUPS = 4
SSM_STATE = 128
SSM_CONV = 4
FFN_CONV = 3
MLA_NOPE = 128
MLA_ROPE = 64
MLA_V = 128
ROPE_THETA = 10000.0
V7X_VMEM_LIMIT = 56 * 1024 * 1024
HI = lax.Precision.HIGHEST

ADAM_LR = 0.001
ADAM_B1 = 0.9
ADAM_B2 = 0.999
ADAM_EPS = 1e-08
ADAM_WD = 0.01
ADAM_STEP = 10


def _cp(*sem):
    return pltpu.CompilerParams(dimension_semantics=sem if sem else None, vmem_limit_bytes=V7X_VMEM_LIMIT)


def _tile(n, cands):
    for c in cands:
        if n % c == 0:
            return c
    return n


def _row(v):
    return v.reshape(1, -1).astype(F32)


def _mm(a, b, *, ta=False, tb=False, out_dtype=F32, residual=None, name):
    m, k = (a.shape[1], a.shape[0]) if ta else a.shape
    n = b.shape[0] if tb else b.shape[1]
    assert k == (b.shape[1] if tb else b.shape[0]), (a.shape, b.shape, ta, tb)
    tm = _tile(m, (1024, 512, 256, 128))
    tn = _tile(n, (1024, 768, 512, 384, 256, 128))
    tk = _tile(k, (512, 256, 128))
    nk = k // tk
    dn = (((0 if ta else 1,), (1 if tb else 0,)), ((), ()))
    has_res = residual is not None

    def body(*refs):
        if has_res:
            a_ref, b_ref, r_ref, o_ref, acc = refs
        else:
            a_ref, b_ref, o_ref, acc = refs
        kk = pl.program_id(2)

        @pl.when(kk == 0)
        def _():
            acc[...] = jnp.zeros_like(acc)

        acc[...] += lax.dot_general(a_ref[...].astype(BF16), b_ref[...].astype(BF16), dn,
                                    preferred_element_type=F32)

        @pl.when(kk == nk - 1)
        def _():
            r = acc[...]
            if has_res:
                r = r + r_ref[...].astype(F32)
            o_ref[...] = r.astype(out_dtype)

    a_spec = pl.BlockSpec((tk, tm), lambda i, j, kk: (kk, i)) if ta else pl.BlockSpec((tm, tk), lambda i, j, kk: (i, kk))
    b_spec = pl.BlockSpec((tn, tk), lambda i, j, kk: (j, kk)) if tb else pl.BlockSpec((tk, tn), lambda i, j, kk: (kk, j))
    in_specs = [a_spec, b_spec]
    args = [a, b]
    if has_res:
        in_specs.append(pl.BlockSpec((tm, tn), lambda i, j, kk: (i, j)))
        args.append(residual)
    return pl.pallas_call(
        body, name=name,
        grid=(m // tm, n // tn, nk),
        in_specs=in_specs,
        out_specs=pl.BlockSpec((tm, tn), lambda i, j, kk: (i, j)),
        out_shape=jax.ShapeDtypeStruct((m, n), out_dtype),
        scratch_shapes=[pltpu.VMEM((tm, tn), F32)],
        compiler_params=_cp("parallel", "parallel", "arbitrary"),
    )(*args)


def _rms_fwd(x, w, *, width=None, col=0, out_dtype=None, name):
    out_dtype = out_dtype or BF16
    s = x.shape[0]
    width = width or x.shape[1]
    tr = _tile(s, (256, 128))

    def body(x_ref, w_ref, o_ref):
        xv = x_ref[...]
        r = lax.rsqrt(jnp.mean(xv * xv, axis=-1, keepdims=True) + EPS)
        o_ref[...] = (xv * r * w_ref[...]).astype(out_dtype)

    return pl.pallas_call(
        body, name=name, grid=(s // tr,),
        in_specs=[pl.BlockSpec((tr, width), lambda i: (i, col)), pl.BlockSpec((1, width), lambda i: (0, 0))],
        out_specs=pl.BlockSpec((tr, width), lambda i: (i, 0)),
        out_shape=jax.ShapeDtypeStruct((s, width), out_dtype),
        compiler_params=_cp("parallel"),
    )(x, _row(w))


def _rms_bwd(x, w, dy, *, add=None, width=None, col=0, dy_col=0, out_dtype=F32, name):
    s = x.shape[0]
    width = width or x.shape[1]
    tr = _tile(s, (256, 128))
    has_add = add is not None

    def body(*refs):
        if has_add:
            x_ref, w_ref, dy_ref, add_ref, dx_ref, dw_ref = refs
        else:
            x_ref, w_ref, dy_ref, dx_ref, dw_ref = refs
        xv = x_ref[...]
        dyv = dy_ref[...].astype(F32)
        r = lax.rsqrt(jnp.mean(xv * xv, axis=-1, keepdims=True) + EPS)
        xh = xv * r
        g = dyv * w_ref[...]
        dx = r * (g - xh * jnp.mean(g * xh, axis=-1, keepdims=True))
        if has_add:
            dx = dx + add_ref[...]
        dx_ref[...] = dx.astype(out_dtype)

        @pl.when(pl.program_id(0) == 0)
        def _():
            dw_ref[...] = jnp.zeros_like(dw_ref)

        dw_ref[...] += jnp.sum(dyv * xh, axis=0, keepdims=True)

    in_specs = [pl.BlockSpec((tr, width), lambda i: (i, col)), pl.BlockSpec((1, width), lambda i: (0, 0)),
                pl.BlockSpec((tr, width), lambda i: (i, dy_col))]
    args = [x, _row(w), dy]
    if has_add:
        in_specs.append(pl.BlockSpec((tr, width), lambda i: (i, 0)))
        args.append(add)
    return pl.pallas_call(
        body, name=name, grid=(s // tr,),
        in_specs=in_specs,
        out_specs=[pl.BlockSpec((tr, width), lambda i: (i, 0)), pl.BlockSpec((1, width), lambda i: (0, 0))],
        out_shape=[jax.ShapeDtypeStruct((s, width), out_dtype), jax.ShapeDtypeStruct((1, width), F32)],
        compiler_params=_cp("arbitrary"),
    )(*args)


def _loss_bwd(h, w, tgt, *, name):
    s, d = h.shape
    tr = _tile(s, (256, 128))

    def body(x_ref, w_ref, t_ref, loss_ref, dx_ref, dw_ref):
        xv = x_ref[...]
        r = lax.rsqrt(jnp.mean(xv * xv, axis=-1, keepdims=True) + EPS)
        xh = xv * r
        e = xh * w_ref[...] - t_ref[...]
        part = 0.5 * jnp.sum(jnp.mean(e * e, axis=-1, keepdims=True), axis=0, keepdims=True)
        dyv = e * (1.0 / d)
        g = dyv * w_ref[...]
        dx_ref[...] = r * (g - xh * jnp.mean(g * xh, axis=-1, keepdims=True))

        @pl.when(pl.program_id(0) == 0)
        def _():
            dw_ref[...] = jnp.zeros_like(dw_ref)
            loss_ref[...] = jnp.zeros_like(loss_ref)

        dw_ref[...] += jnp.sum(dyv * xh, axis=0, keepdims=True)
        loss_ref[...] += jnp.broadcast_to(part, loss_ref.shape)

    return pl.pallas_call(
        body, name=name, grid=(s // tr,),
        in_specs=[pl.BlockSpec((tr, d), lambda i: (i, 0)), pl.BlockSpec((1, d), lambda i: (0, 0)),
                  pl.BlockSpec((tr, d), lambda i: (i, 0))],
        out_specs=[pl.BlockSpec((1, 128), lambda i: (0, 0)), pl.BlockSpec((tr, d), lambda i: (i, 0)),
                   pl.BlockSpec((1, d), lambda i: (0, 0))],
        out_shape=[jax.ShapeDtypeStruct((1, 128), F32), jax.ShapeDtypeStruct((s, d), F32),
                   jax.ShapeDtypeStruct((1, d), F32)],
        compiler_params=_cp("arbitrary"),
    )(h, _row(w), tgt)


_G0 = math.sqrt(2.0 / math.pi)
_G1 = 0.044715


def _gelu(x):
    return 0.5 * x * (1.0 + jnp.tanh(_G0 * (x + _G1 * x * x * x)))


def _gelu_and_grad(x):
    th = jnp.tanh(_G0 * (x + _G1 * x * x * x))
    val = 0.5 * x * (1.0 + th)
    grad = 0.5 * (1.0 + th) + 0.5 * x * (1.0 - th * th) * _G0 * (1.0 + 3.0 * _G1 * x * x)
    return val, grad


def _sigmoid(x):
    return 1.0 / (1.0 + jnp.exp(-x))


def _shift_down(x, k):
    if k == 0:
        return x
    rows = lax.broadcasted_iota(jnp.int32, x.shape, 0)
    return jnp.where(rows >= k, pltpu.roll(x, k, 0), 0.0)


def _shift_up(x, k):
    if k == 0:
        return x
    n = x.shape[0]
    rows = lax.broadcasted_iota(jnp.int32, x.shape, 0)
    return jnp.where(rows < n - k, pltpu.roll(x, n - k, 0), 0.0)


def _conv_rows(x, w_ref, b_ref, kw):
    y = b_ref[...] + w_ref[kw - 1:kw, :] * x
    for k in range(kw - 1):
        y = y + w_ref[k:k + 1, :] * _shift_down(x, kw - 1 - k)
    return y


def _conv_rows_bwd(x, dgc, w_ref, dw_ref, db_ref, kw):
    dx = w_ref[kw - 1:kw, :] * dgc
    dw_ref[kw - 1:kw, :] = jnp.sum(dgc * x, axis=0, keepdims=True)
    for k in range(kw - 1):
        sh = kw - 1 - k
        dx = dx + w_ref[k:k + 1, :] * _shift_up(dgc, sh)
        dw_ref[k:k + 1, :] = jnp.sum(dgc * _shift_down(x, sh), axis=0, keepdims=True)
    db_ref[...] = jnp.sum(dgc, axis=0, keepdims=True)
    return dx


def _ffn_mid_fwd(up, conv_w, conv_b, *, name):
    s = up.shape[0]
    f = up.shape[1] // 2
    tc = _tile(f, (256, 128))
    nf = f // tc

    def body(g_ref, v_ref, w_ref, b_ref, o_ref):
        gc = _conv_rows(g_ref[...], w_ref, b_ref, FFN_CONV)
        o_ref[...] = (_gelu(gc) * v_ref[...]).astype(BF16)

    return pl.pallas_call(
        body, name=name, grid=(nf,),
        in_specs=[pl.BlockSpec((s, tc), lambda j: (0, j)), pl.BlockSpec((s, tc), lambda j: (0, j + nf)),
                  pl.BlockSpec((FFN_CONV, tc), lambda j: (0, j)), pl.BlockSpec((1, tc), lambda j: (0, j))],
        out_specs=pl.BlockSpec((s, tc), lambda j: (0, j)),
        out_shape=jax.ShapeDtypeStruct((s, f), BF16),
        compiler_params=_cp("parallel"),
    )(up, up, conv_w, _row(conv_b))


def _ffn_mid_bwd(up, conv_w, conv_b, da, *, name):
    s = up.shape[0]
    f = up.shape[1] // 2
    tc = _tile(f, (256, 128))
    nf = f // tc

    def body(g_ref, v_ref, w_ref, b_ref, da_ref, dg_ref, dv_ref, dw_ref, db_ref):
        g = g_ref[...]
        gc = _conv_rows(g, w_ref, b_ref, FFN_CONV)
        gel, dgel = _gelu_and_grad(gc)
        dav = da_ref[...]
        dv_ref[...] = (dav * gel).astype(BF16)
        dgc = dav * v_ref[...] * dgel
        dg_ref[...] = _conv_rows_bwd(g, dgc, w_ref, dw_ref, db_ref, FFN_CONV).astype(BF16)

    col = lambda j: (0, j)
    return pl.pallas_call(
        body, name=name, grid=(nf,),
        in_specs=[pl.BlockSpec((s, tc), col), pl.BlockSpec((s, tc), lambda j: (0, j + nf)),
                  pl.BlockSpec((FFN_CONV, tc), col), pl.BlockSpec((1, tc), col), pl.BlockSpec((s, tc), col)],
        out_specs=[pl.BlockSpec((s, tc), col), pl.BlockSpec((s, tc), col),
                   pl.BlockSpec((FFN_CONV, tc), col), pl.BlockSpec((1, tc), col)],
        out_shape=[jax.ShapeDtypeStruct((s, f), BF16), jax.ShapeDtypeStruct((s, f), BF16),
                   jax.ShapeDtypeStruct((FFN_CONV, f), F32), jax.ShapeDtypeStruct((1, f), F32)],
        compiler_params=_cp("parallel"),
    )(up, up, conv_w, _row(conv_b), da)


def _gm_mask():
    r = lax.broadcasted_iota(jnp.int32, (BLK, BLK), 0) // CHUNK
    c = lax.broadcasted_iota(jnp.int32, (BLK, BLK), 1) // CHUNK
    return r >= c


def _gm_specs(s, gd):
    nb = s // BLK
    u_spec = pl.BlockSpec((BLK, gd), lambda g, n: (n, g))
    v_spec = pl.BlockSpec((BLK, gd), lambda g, n: (n, g + GM_GROUPS))
    vec_spec = pl.BlockSpec((1, gd), lambda g, n: (0, g))
    ws_spec = pl.BlockSpec((1, BLK, BLK), lambda g, n: (g, 0, 0))
    bs_spec = pl.BlockSpec((1, BLK, 1), lambda g, n: (g, 0, 0))
    return nb, u_spec, v_spec, vec_spec, ws_spec, bs_spec


def _gm_fwd(proj, ln_g, ln_b, ws, bs, *, name):
    s = proj.shape[0]
    gd = ln_g.shape[-1]
    w = GM_GROUPS * gd
    nb, u_spec, v_spec, vec_spec, ws_spec, bs_spec = _gm_specs(s, gd)

    def body(u_ref, v_ref, lg_ref, lb_ref, ws_ref, bs_ref, o_ref):
        ua = _gelu(u_ref[...])
        va = _gelu(v_ref[...])
        mu = jnp.mean(va, axis=-1, keepdims=True)
        vc = va - mu
        var = jnp.mean(vc * vc, axis=-1, keepdims=True)
        vn = vc * lax.rsqrt(var + EPS) * lg_ref[...] + lb_ref[...]
        wm = jnp.where(_gm_mask(), ws_ref[0], 0.0).astype(BF16)
        gate = jnp.dot(wm, vn.astype(BF16), preferred_element_type=F32) + bs_ref[0]
        o_ref[...] = (ua * gate).astype(BF16)

    return pl.pallas_call(
        body, name=name, grid=(GM_GROUPS, nb),
        in_specs=[u_spec, v_spec, vec_spec, vec_spec, ws_spec, bs_spec],
        out_specs=pl.BlockSpec((BLK, gd), lambda g, n: (n, g)),
        out_shape=jax.ShapeDtypeStruct((s, w), BF16),
        compiler_params=_cp("parallel", "parallel"),
    )(proj, proj, ln_g.reshape(1, w), ln_b.reshape(1, w), ws, bs.reshape(GM_GROUPS, BLK, 1))


def _gm_bwd(proj, ln_g, ln_b, ws, bs, dya, *, name):
    s = proj.shape[0]
    gd = ln_g.shape[-1]
    w = GM_GROUPS * gd
    nb, u_spec, v_spec, vec_spec, ws_spec, bs_spec = _gm_specs(s, gd)

    def body(u_ref, v_ref, lg_ref, lb_ref, ws_ref, bs_ref, dy_ref, du_ref, dv_ref, dlg_ref, dlb_ref, dws_ref, dbs_ref):
        ua, dua_du = _gelu_and_grad(u_ref[...])
        va, dva_dv = _gelu_and_grad(v_ref[...])
        mu = jnp.mean(va, axis=-1, keepdims=True)
        vc = va - mu
        var = jnp.mean(vc * vc, axis=-1, keepdims=True)
        rstd = lax.rsqrt(var + EPS)
        xh = vc * rstd
        vn = (xh * lg_ref[...] + lb_ref[...]).astype(BF16)
        mask = _gm_mask()
        wm = jnp.where(mask, ws_ref[0], 0.0).astype(BF16)
        gate = jnp.dot(wm, vn, preferred_element_type=F32) + bs_ref[0]
        dy = dy_ref[...]
        du_ref[...] = (dy * gate * dua_du).astype(BF16)
        dgate = dy * ua
        dgb = dgate.astype(BF16)
        dwm = lax.dot_general(dgb, vn, (((1,), (1,)), ((), ())), preferred_element_type=F32)
        dvn = lax.dot_general(wm, dgb, (((0,), (0,)), ((), ())), preferred_element_type=F32)
        dxh = dvn * lg_ref[...]
        dva = rstd * (dxh - jnp.mean(dxh, axis=-1, keepdims=True) - xh * jnp.mean(dxh * xh, axis=-1, keepdims=True))
        dv_ref[...] = (dva * dva_dv).astype(BF16)

        @pl.when(pl.program_id(1) == 0)
        def _():
            dlg_ref[...] = jnp.zeros_like(dlg_ref)
            dlb_ref[...] = jnp.zeros_like(dlb_ref)
            dws_ref[...] = jnp.zeros_like(dws_ref)
            dbs_ref[...] = jnp.zeros_like(dbs_ref)

        dlg_ref[...] += jnp.sum(dvn * xh, axis=0, keepdims=True)
        dlb_ref[...] += jnp.sum(dvn, axis=0, keepdims=True)
        dws_ref[0] += jnp.where(mask, dwm, 0.0)
        dbs_ref[0] += jnp.sum(dgate, axis=-1, keepdims=True)

    out_uv = pl.BlockSpec((BLK, gd), lambda g, n: (n, g))
    return pl.pallas_call(
        body, name=name, grid=(GM_GROUPS, nb),
        in_specs=[u_spec, v_spec, vec_spec, vec_spec, ws_spec, bs_spec, pl.BlockSpec((BLK, gd), lambda g, n: (n, g))],
        out_specs=[out_uv, out_uv, vec_spec, vec_spec, ws_spec, bs_spec],
        out_shape=[jax.ShapeDtypeStruct((s, w), BF16), jax.ShapeDtypeStruct((s, w), BF16),
                   jax.ShapeDtypeStruct((1, w), F32), jax.ShapeDtypeStruct((1, w), F32),
                   jax.ShapeDtypeStruct((GM_GROUPS, BLK, BLK), F32), jax.ShapeDtypeStruct((GM_GROUPS, BLK, 1), F32)],
        compiler_params=_cp("parallel", "arbitrary"),
    )(proj, proj, ln_g.reshape(1, w), ln_b.reshape(1, w), ws, bs.reshape(GM_GROUPS, BLK, 1), dya)


def _silu_conv_fwd(proj, conv_w, conv_b, *, col0, name):
    s = proj.shape[0]
    c = conv_w.shape[1]
    tc = _tile(c, (256, 128))
    off = col0 // tc

    def body(x_ref, w_ref, b_ref, o_ref):
        y = _conv_rows(x_ref[...], w_ref, b_ref, SSM_CONV)
        o_ref[...] = y * _sigmoid(y)

    col = lambda j: (0, j)
    return pl.pallas_call(
        body, name=name, grid=(c // tc,),
        in_specs=[pl.BlockSpec((s, tc), lambda j: (0, j + off)), pl.BlockSpec((SSM_CONV, tc), col), pl.BlockSpec((1, tc), col)],
        out_specs=pl.BlockSpec((s, tc), col),
        out_shape=jax.ShapeDtypeStruct((s, c), F32),
        compiler_params=_cp("parallel"),
    )(proj, conv_w, _row(conv_b))


def _silu_conv_bwd(proj, conv_w, conv_b, dact, *, col0, name):
    s = proj.shape[0]
    c = conv_w.shape[1]
    tc = _tile(c, (256, 128))
    off = col0 // tc

    def body(x_ref, w_ref, b_ref, d_ref, dx_ref, dw_ref, db_ref):
        x = x_ref[...]
        y = _conv_rows(x, w_ref, b_ref, SSM_CONV)
        sg = _sigmoid(y)
        dgc = d_ref[...] * sg * (1.0 + y * (1.0 - sg))
        dx_ref[...] = _conv_rows_bwd(x, dgc, w_ref, dw_ref, db_ref, SSM_CONV).astype(BF16)

    col = lambda j: (0, j)
    return pl.pallas_call(
        body, name=name, grid=(c // tc,),
        in_specs=[pl.BlockSpec((s, tc), lambda j: (0, j + off)), pl.BlockSpec((SSM_CONV, tc), col), pl.BlockSpec((1, tc), col),
                  pl.BlockSpec((s, tc), col)],
        out_specs=[pl.BlockSpec((s, tc), col), pl.BlockSpec((SSM_CONV, tc), col), pl.BlockSpec((1, tc), col)],
        out_shape=[jax.ShapeDtypeStruct((s, c), BF16), jax.ShapeDtypeStruct((SSM_CONV, c), F32),
                   jax.ShapeDtypeStruct((1, c), F32)],
        compiler_params=_cp("parallel"),
    )(proj, conv_w, _row(conv_b), dact)


def _head_select(heads):
    r = lax.broadcasted_iota(jnp.int32, (128, heads * SSM_HEAD_DIM), 0)
    c = lax.broadcasted_iota(jnp.int32, (128, heads * SSM_HEAD_DIM), 1) // SSM_HEAD_DIM
    return (r == c).astype(F32)


def _dt_fwd(dt_raw, dt_bias, *, heads, name):
    s = dt_raw.shape[0]
    d = heads * SSM_HEAD_DIM
    tr = _tile(s, (256, 128))

    def body(x_ref, b_ref, o_ref):
        pre = jnp.dot(x_ref[...] + b_ref[...], _head_select(heads), precision=HI, preferred_element_type=F32)
        o_ref[...] = jax.nn.softplus(pre)

    return pl.pallas_call(
        body, name=name, grid=(s // tr,),
        in_specs=[pl.BlockSpec((tr, 128), lambda i: (i, 0)), pl.BlockSpec((1, 128), lambda i: (0, 0))],
        out_specs=pl.BlockSpec((tr, d), lambda i: (i, 0)),
        out_shape=jax.ShapeDtypeStruct((s, d), F32),
        compiler_params=_cp("parallel"),
    )(dt_raw, dt_bias)


def _dt_bwd(dt_raw, dt_bias, zt, da_lane, dd_lane, a_row, *, heads, name):
    s = dt_raw.shape[0]
    d = heads * SSM_HEAD_DIM
    tr = _tile(s, (256, 128))
    nt = (((1,), (1,)), ((), ()))

    def body(x_ref, b_ref, z_ref, da_ref, dd_ref, a_ref, o_ref, db_ref, dal_ref, dds_ref):
        sel = _head_select(heads)
        ddt = lax.dot_general(z_ref[...], sel, nt, precision=HI, preferred_element_type=F32)
        g = ddt * _sigmoid(x_ref[...] + b_ref[...])
        o_ref[...] = g.astype(BF16)

        @pl.when(pl.program_id(0) == 0)
        def _():
            db_ref[...] = jnp.zeros_like(db_ref)
            da = lax.dot_general(da_ref[...], sel, nt, precision=HI, preferred_element_type=F32)
            dal_ref[...] = da * a_ref[...]
            dds_ref[...] = lax.dot_general(dd_ref[...], sel, nt, precision=HI, preferred_element_type=F32)

        db_ref[...] += jnp.sum(g, axis=0, keepdims=True)

    vec = pl.BlockSpec((1, 128), lambda i: (0, 0))
    lane = pl.BlockSpec((1, d), lambda i: (0, 0))
    return pl.pallas_call(
        body, name=name, grid=(s // tr,),
        in_specs=[pl.BlockSpec((tr, 128), lambda i: (i, 0)), vec, pl.BlockSpec((tr, d), lambda i: (i, 0)), lane, lane, vec],
        out_specs=[pl.BlockSpec((tr, 128), lambda i: (i, 0)), vec, vec, vec],
        out_shape=[jax.ShapeDtypeStruct((s, 128), BF16)] + [jax.ShapeDtypeStruct((1, 128), F32)] * 3,
        compiler_params=_cp("arbitrary"),
    )(dt_raw, dt_bias, zt, da_lane, dd_lane, a_row)


_NT = (((1,), (1,)), ((), ()))
_TN = (((0,), (0,)), ((), ()))


def _bdot(a, b, dn=None):
    if dn is None:
        return jnp.dot(a, b, preferred_element_type=F32)
    return lax.dot_general(a, b, dn, preferred_element_type=F32)


def _ssd_common(x_ref, b_ref, c_ref, dt_ref, a_ref):
    x = x_ref[...]
    dt = dt_ref[...]
    rows = lax.broadcasted_iota(jnp.int32, (BLK, BLK), 0)
    cols = lax.broadcasted_iota(jnp.int32, (BLK, BLK), 1)
    tl = (rows >= cols).astype(F32)
    acum = jnp.dot(tl, dt * a_ref[...], precision=HI, preferred_element_type=F32)
    alast = acum[BLK - 1:BLK, :]
    bm = b_ref[...].astype(BF16)
    cm = c_ref[...].astype(BF16)
    cb = _bdot(cm, bm, _NT)
    return x, dt, rows, cols, acum, alast, bm, cm, cb


def _ssd_decay(ap, apt, e, low):
    acol = ap[:, e * SSM_HEAD_DIM:e * SSM_HEAD_DIM + 1]
    arow = apt[e * SSM_HEAD_DIM:e * SSM_HEAD_DIM + 1, :]
    return jnp.where(low, jnp.exp(jnp.minimum(acol - arow, 0.0)), 0.0)


def _ssd_specs(s, d):
    gw = d // SSM_GROUPS
    bcol = d // SSM_STATE
    return gw, bcol


def _ssd_fwd(act, dte, a_lane, d_lane, *, name):
    s = act.shape[0]
    d = dte.shape[1]
    gw, bcol = _ssd_specs(s, d)
    npair = gw // 128
    nc = s // BLK

    def body(x_ref, b_ref, c_ref, dt_ref, a_ref, dsk_ref, y_ref, st_ref, ht):
        @pl.when(pl.program_id(1) == 0)
        def _():
            ht[...] = jnp.zeros_like(ht)

        x, dt, rows, cols, acum, alast, bm, cm, cb = _ssd_common(x_ref, b_ref, c_ref, dt_ref, a_ref)
        low = rows >= cols
        first = cols < SSM_HEAD_DIM
        xd = x * dt
        h_in = ht[...]
        st_ref[0] = h_in
        yoff = _bdot(cm, h_in.astype(BF16)) * jnp.exp(acum)
        parts = []
        for p in range(npair):
            ap = acum[:, p * 128:(p + 1) * 128]
            apt = ap.T
            xdp = xd[:, p * 128:(p + 1) * 128].astype(BF16)
            ys = [_bdot((cb * _ssd_decay(ap, apt, e, low)).astype(BF16), xdp) for e in range(2)]
            parts.append(jnp.where(first, ys[0], ys[1]))
        ydiag = parts[0] if npair == 1 else jnp.concatenate(parts, axis=1)
        y_ref[...] = ydiag + yoff + dsk_ref[...] * x
        w = (xd * jnp.exp(alast - acum)).astype(BF16)
        ht[...] = h_in * jnp.exp(alast) + _bdot(bm, w, _TN)

    blk = lambda g, c: (c, g)
    vec = pl.BlockSpec((1, gw), lambda g, c: (0, g))
    return pl.pallas_call(
        body, name=name, grid=(SSM_GROUPS, nc),
        in_specs=[pl.BlockSpec((BLK, gw), blk),
                  pl.BlockSpec((BLK, SSM_STATE), lambda g, c: (c, bcol + g)),
                  pl.BlockSpec((BLK, SSM_STATE), lambda g, c: (c, bcol + SSM_GROUPS + g)),
                  pl.BlockSpec((BLK, gw), blk), vec, vec],
        out_specs=[pl.BlockSpec((BLK, gw), blk), pl.BlockSpec((1, SSM_STATE, gw), lambda g, c: (c, 0, g))],
        out_shape=[jax.ShapeDtypeStruct((s, d), F32), jax.ShapeDtypeStruct((nc, SSM_STATE, d), F32)],
        scratch_shapes=[pltpu.VMEM((SSM_STATE, gw), F32)],
        compiler_params=_cp("parallel", "arbitrary"),
    )(act, act, act, dte, a_lane, d_lane)


def _ssd_bwd(act, dte, a_lane, d_lane, states, dy, *, name):
    s = act.shape[0]
    d = dte.shape[1]
    gw, bcol = _ssd_specs(s, d)
    npair = gw // 128
    nc = s // BLK
    gn = SSM_GROUPS * SSM_STATE

    def body(x_ref, b_ref, c_ref, dt_ref, a_ref, dsk_ref, st_ref, dy_ref,
             dx_ref, db_ref, dc_ref, zt_ref, dal_ref, ddl_ref, dht):
        @pl.when(pl.program_id(1) == 0)
        def _():
            dht[...] = jnp.zeros_like(dht)
            dal_ref[...] = jnp.zeros_like(dal_ref)
            ddl_ref[...] = jnp.zeros_like(ddl_ref)

        x, dt, rows, cols, acum, alast, bm, cm, cb = _ssd_common(x_ref, b_ref, c_ref, dt_ref, a_ref)
        low = rows >= cols
        first = cols < SSM_HEAD_DIM
        a = a_ref[...]
        xd = x * dt
        ea = jnp.exp(acum)
        wdec = jnp.exp(alast - acum)
        el = jnp.exp(alast)
        h_in = st_ref[0]
        hb = h_in.astype(BF16)
        g = dy_ref[...]
        dh = dht[...]
        dhb = dh.astype(BF16)

        yoff = _bdot(cm, hb) * ea
        geb = (g * ea).astype(BF16)
        dc = _bdot(geb, hb, _NT)
        u = _bdot(bm, dhb)
        wx = xd * wdec
        db = _bdot(wx.astype(BF16), dhb, _NT)
        dxd = wdec * u
        xwu = wx * u
        da_l = g * yoff - xwu
        dalast = jnp.sum(xwu, axis=0, keepdims=True) + el * jnp.sum(dh * h_in, axis=0, keepdims=True)
        dht[...] = dh * el + _bdot(cm, geb, _TN)

        dcb = jnp.zeros((BLK, BLK), F32)
        dxd_parts, col_parts = [], []
        for p in range(npair):
            ap = acum[:, p * 128:(p + 1) * 128]
            apt = ap.T
            xdp = xd[:, p * 128:(p + 1) * 128].astype(BF16)
            gp = g[:, p * 128:(p + 1) * 128]
            dxp = jnp.zeros((BLK, 128), F32)
            colsum = []
            for e in range(2):
                dec = _ssd_decay(ap, apt, e, low)
                m = cb * dec
                gpm = jnp.where(first if e == 0 else jnp.logical_not(first), gp, 0.0).astype(BF16)
                dm = _bdot(gpm, xdp, _NT)
                q = dm * m
                colsum.append(jnp.sum(q, axis=1, keepdims=True) - jnp.sum(q.T, axis=1, keepdims=True))
                dcb = dcb + dm * dec
                dxp = dxp + _bdot(m.astype(BF16), gpm, _TN)
            dxd_parts.append(dxp)
            col_parts.append(jnp.where(first, colsum[0], colsum[1]) * (1.0 / SSM_HEAD_DIM))
        cat = (lambda ps: ps[0] if npair == 1 else jnp.concatenate(ps, axis=1))
        dxd = dxd + cat(dxd_parts)
        da_l = da_l + cat(col_parts)
        rows_w = lax.broadcasted_iota(jnp.int32, (BLK, gw), 0)
        da_l = da_l + jnp.where(rows_w == BLK - 1, dalast, 0.0)
        dcbb = dcb.astype(BF16)
        dc_ref[...] = dc + _bdot(dcbb, bm)
        db_ref[...] = db + _bdot(dcbb, cm, _TN)
        tu = (rows <= cols).astype(F32)
        dda = jnp.dot(tu, da_l, precision=HI, preferred_element_type=F32)
        zt_ref[...] = dxd * x + dda * a
        dal_ref[...] += jnp.sum(dda * dt, axis=0, keepdims=True)
        ddl_ref[...] += jnp.sum(g * x, axis=0, keepdims=True)
        dx_ref[...] = dsk_ref[...] * g + dxd * dt

    blk = lambda g, c: (nc - 1 - c, g)
    vec = pl.BlockSpec((1, gw), lambda g, c: (0, g))
    bc_out = pl.BlockSpec((BLK, SSM_STATE), blk)
    return pl.pallas_call(
        body, name=name, grid=(SSM_GROUPS, nc),
        in_specs=[pl.BlockSpec((BLK, gw), blk),
                  pl.BlockSpec((BLK, SSM_STATE), lambda g, c: (nc - 1 - c, bcol + g)),
                  pl.BlockSpec((BLK, SSM_STATE), lambda g, c: (nc - 1 - c, bcol + SSM_GROUPS + g)),
                  pl.BlockSpec((BLK, gw), blk), vec, vec,
                  pl.BlockSpec((1, SSM_STATE, gw), lambda g, c: (nc - 1 - c, 0, g)),
                  pl.BlockSpec((BLK, gw), blk)],
        out_specs=[pl.BlockSpec((BLK, gw), blk), bc_out, bc_out, pl.BlockSpec((BLK, gw), blk), vec, vec],
        out_shape=[jax.ShapeDtypeStruct((s, d), F32), jax.ShapeDtypeStruct((s, gn), F32),
                   jax.ShapeDtypeStruct((s, gn), F32), jax.ShapeDtypeStruct((s, d), F32),
                   jax.ShapeDtypeStruct((1, d), F32), jax.ShapeDtypeStruct((1, d), F32)],
        scratch_shapes=[pltpu.VMEM((SSM_STATE, gw), F32)],
        compiler_params=_cp("parallel", "arbitrary"),
    )(act, act, act, dte, a_lane, d_lane, states, dy)


def _gnorm_fwd(y, proj, norm_w, *, zcol, name):
    s, d = y.shape
    tr = _tile(s, (256, 128))
    gw = d // SSM_GROUPS

    def body(y_ref, z_ref, w_ref, o_ref):
        z = z_ref[...]
        y2 = y_ref[...] * (z * _sigmoid(z))
        for g in range(SSM_GROUPS):
            sl = slice(g * gw, (g + 1) * gw)
            v = y2[:, sl]
            r = lax.rsqrt(jnp.mean(v * v, axis=-1, keepdims=True) + EPS)
            o_ref[:, sl] = (v * r * w_ref[:, sl]).astype(BF16)

    return pl.pallas_call(
        body, name=name, grid=(s // tr,),
        in_specs=[pl.BlockSpec((tr, d), lambda i: (i, 0)), pl.BlockSpec((tr, d), lambda i: (i, zcol)),
                  pl.BlockSpec((1, d), lambda i: (0, 0))],
        out_specs=pl.BlockSpec((tr, d), lambda i: (i, 0)),
        out_shape=jax.ShapeDtypeStruct((s, d), BF16),
        compiler_params=_cp("parallel"),
    )(y, proj, _row(norm_w))


def _gnorm_bwd(y, proj, norm_w, dout, *, zcol, dcol, name):
    s, d = y.shape
    tr = _tile(s, (256, 128))
    gw = d // SSM_GROUPS

    def body(y_ref, z_ref, w_ref, do_ref, dy_ref, dz_ref, dw_ref):
        @pl.when(pl.program_id(0) == 0)
        def _():
            dw_ref[...] = jnp.zeros_like(dw_ref)

        z = z_ref[...]
        yv = y_ref[...]
        sg = _sigmoid(z)
        sz = z * sg
        y2 = yv * sz
        for g in range(SSM_GROUPS):
            sl = slice(g * gw, (g + 1) * gw)
            v = y2[:, sl]
            do = do_ref[:, sl]
            r = lax.rsqrt(jnp.mean(v * v, axis=-1, keepdims=True) + EPS)
            xh = v * r
            gg = do * w_ref[:, sl]
            dy2 = r * (gg - xh * jnp.mean(gg * xh, axis=-1, keepdims=True))
            dw_ref[:, sl] += jnp.sum(do * xh, axis=0, keepdims=True)
            dy_ref[:, sl] = dy2 * sz[:, sl]
            dz_ref[:, sl] = (dy2 * yv[:, sl] * (sg[:, sl] * (1.0 + z[:, sl] * (1.0 - sg[:, sl])))).astype(BF16)

    return pl.pallas_call(
        body, name=name, grid=(s // tr,),
        in_specs=[pl.BlockSpec((tr, d), lambda i: (i, 0)), pl.BlockSpec((tr, d), lambda i: (i, zcol)),
                  pl.BlockSpec((1, d), lambda i: (0, 0)), pl.BlockSpec((tr, d), lambda i: (i, dcol))],
        out_specs=[pl.BlockSpec((tr, d), lambda i: (i, 0)), pl.BlockSpec((tr, d), lambda i: (i, 0)),
                   pl.BlockSpec((1, d), lambda i: (0, 0))],
        out_shape=[jax.ShapeDtypeStruct((s, d), F32), jax.ShapeDtypeStruct((s, d), BF16),
                   jax.ShapeDtypeStruct((1, d), F32)],
        compiler_params=_cp("arbitrary"),
    )(y, proj, _row(norm_w), dout)


def _lanes(v):
    return jnp.repeat(v.astype(F32), SSM_HEAD_DIM).reshape(1, -1)


def _pad128(v):
    return jnp.pad(v.astype(F32).reshape(1, -1), ((0, 0), (0, 128 - v.shape[-1])))


def _even_fwd(h, p, tag):
    d = h.shape[1]
    heads = d // SSM_HEAD_DIM
    hn = _rms_fwd(h, p["norm_mix"], name=tag + "_rms")
    proj = _mm(hn, p["w_in_main"], name=tag + "_in")
    pdt = _mm(hn, p["w_in_dt"], name=tag + "_indt")
    ya = _gm_fwd(proj, p["gm_ln_g"], p["gm_ln_b"], p["gm_ws"], p["gm_bs"], name=tag + "_gm")
    act = _silu_conv_fwd(proj, p["conv_w"], p["conv_b"], col0=3 * d, name=tag + "_conv")
    dte = _dt_fwd(pdt, _pad128(p["dt_bias"]), heads=heads, name=tag + "_dt")
    a = -jnp.exp(p["a_log"].astype(F32))
    y, states = _ssd_fwd(act, dte, _lanes(a), _lanes(p["d_skip"]), name=tag + "_ssd")
    yb = _gnorm_fwd(y, proj, p["ssm_norm_w"], zcol=2, name=tag + "_gn")
    cat = jnp.concatenate([ya, yb], axis=1)
    h1 = _mm(cat, p["w_out"], residual=h, name=tag + "_out")
    return h1, (h, hn, proj, pdt, act, dte, y, states, cat)


def _even_bwd(dh1, p, saved, tag):
    h, hn, proj, pdt, act, dte, y, states, cat = saved
    d = h.shape[1]
    heads = d // SSM_HEAD_DIM
    a = -jnp.exp(p["a_log"].astype(F32))
    g = {}
    dcat = _mm(dh1, p["w_out"], tb=True, name=tag + "_dcat")
    g["w_out"] = _mm(cat, dh1, ta=True, name=tag + "_dwout")
    du, dv, dlg, dlb, dws, dbs = _gm_bwd(proj, p["gm_ln_g"], p["gm_ln_b"], p["gm_ws"], p["gm_bs"], dcat, name=tag + "_gmb")
    g["gm_ln_g"] = dlg.reshape(GM_GROUPS, -1)
    g["gm_ln_b"] = dlb.reshape(GM_GROUPS, -1)
    g["gm_ws"] = dws
    g["gm_bs"] = dws_bs = dbs.reshape(GM_GROUPS, BLK)
    dy, dz, dnw = _gnorm_bwd(y, proj, p["ssm_norm_w"], dcat, zcol=2, dcol=1, name=tag + "_gnb")
    g["ssm_norm_w"] = dnw[0]
    dxs, db, dc, zt, dal, ddl = _ssd_bwd(act, dte, _lanes(a), _lanes(p["d_skip"]), states, dy, name=tag + "_ssdb")
    ddt, ddtb, dalog, ddsk = _dt_bwd(pdt, _pad128(p["dt_bias"]), zt, dal, ddl, _pad128(a), heads=heads, name=tag + "_dtb")
    g["dt_bias"] = ddtb[0, :heads]
    g["a_log"] = dalog[0, :heads]
    g["d_skip"] = ddsk[0, :heads]
    dact = jnp.concatenate([dxs, db, dc], axis=1)
    dxbc, dcw, dcb = _silu_conv_bwd(proj, p["conv_w"], p["conv_b"], dact, col0=3 * d, name=tag + "_convb")
    g["conv_w"] = dcw
    g["conv_b"] = dcb[0]
    dproj = jnp.concatenate([du, dv, dz, dxbc], axis=1)
    dhn = _mm(dproj, p["w_in_main"], tb=True, name=tag + "_dhn")
    dhn = _mm(ddt, p["w_in_dt"], tb=True, residual=dhn, name=tag + "_dhn2")
    g["w_in_main"] = _mm(hn, dproj, ta=True, name=tag + "_dwin")
    g["w_in_dt"] = _mm(hn, ddt, ta=True, name=tag + "_dwdt")
    dh, dnm = _rms_bwd(h, p["norm_mix"], dhn, add=dh1, name=tag + "_rmsb")
    g["norm_mix"] = dnm[0]
    return dh, g


def _ffn_fwd(h, p, tag):
    hn = _rms_fwd(h, p["norm_ffn"], name=tag + "_rms")
    up = _mm(hn, p["w_up"], name=tag + "_up")
    a = _ffn_mid_fwd(up, p["conv_w"], p["conv_b"], name=tag + "_mid")
    h2 = _mm(a, p["w_down"], residual=h, name=tag + "_down")
    return h2, (h, hn, up, a)


def _ffn_bwd(dh2, p, saved, tag):
    h, hn, up, a = saved
    g = {}
    da = _mm(dh2, p["w_down"], tb=True, name=tag + "_da")
    g["w_down"] = _mm(a, dh2, ta=True, name=tag + "_dwdown")
    dg, dv, dcw, dcb = _ffn_mid_bwd(up, p["conv_w"], p["conv_b"], da, name=tag + "_midb")
    g["conv_w"] = dcw
    g["conv_b"] = dcb[0]
    dup = jnp.concatenate([dg, dv], axis=1)
    dhn = _mm(dup, p["w_up"], tb=True, name=tag + "_dhn")
    g["w_up"] = _mm(hn, dup, ta=True, name=tag + "_dwup")
    dh, dnw = _rms_bwd(h, p["norm_ffn"], dhn, add=dh2, name=tag + "_rmsb")
    g["norm_ffn"] = dnw[0]
    return dh, g


def _rope(x, cos_p, sin_p):
    half = MLA_ROPE // 2
    lane = lax.broadcasted_iota(jnp.int32, x.shape, 1)
    swapped = jnp.where(lane < half, pltpu.roll(x, 128 - half, 1), pltpu.roll(x, half, 1))
    return x * cos_p + swapped * sin_p


def _rope_t(g, cos_p, sin_p):
    half = MLA_ROPE // 2
    gs = g * sin_p
    lane = lax.broadcasted_iota(jnp.int32, g.shape, 1)
    swapped = jnp.where(lane < half, pltpu.roll(gs, 128 - half, 1), pltpu.roll(gs, half, 1))
    return g * cos_p + swapped


def _attn_probs(qn_ref, qp_ref, kn_ref, kp_ref, cq_ref, sq_ref, ck_ref, sk_ref, tq):
    s = kn_ref.shape[0]
    scale = (MLA_NOPE + MLA_ROPE) ** -0.5
    qn = qn_ref[...].astype(BF16)
    qp = _rope(qp_ref[...], cq_ref[...], sq_ref[...]).astype(BF16)
    kn = kn_ref[...].astype(BF16)
    kp = _rope(kp_ref[...], ck_ref[...], sk_ref[...]).astype(BF16)
    sc = (_bdot(qn, kn, _NT) + _bdot(qp, kp, _NT)) * scale
    qpos = pl.program_id(1) * tq + lax.broadcasted_iota(jnp.int32, (tq, s), 0)
    kpos = lax.broadcasted_iota(jnp.int32, (tq, s), 1)
    sc = jnp.where(kpos // CHUNK <= qpos // CHUNK, sc, -jnp.inf)
    sc = sc - jnp.max(sc, axis=-1, keepdims=True)
    e = jnp.exp(sc)
    p = e / jnp.sum(e, axis=-1, keepdims=True)
    return p, qn, qp, kn, kp, scale


def _attn_in_specs(s, tq, kr_col):
    return [pl.BlockSpec((tq, 128), lambda h, i: (i, 2 * h)), pl.BlockSpec((tq, 128), lambda h, i: (i, 2 * h + 1)),
            pl.BlockSpec((s, 128), lambda h, i: (0, 2 * h)), pl.BlockSpec((s, 128), lambda h, i: (0, kr_col)),
            pl.BlockSpec((tq, 128), lambda h, i: (i, 0)), pl.BlockSpec((tq, 128), lambda h, i: (i, 0)),
            pl.BlockSpec((s, 128), lambda h, i: (0, 0)), pl.BlockSpec((s, 128), lambda h, i: (0, 0)),
            pl.BlockSpec((s, 128), lambda h, i: (0, 2 * h + 1))]


def _attn_fwd(q, kv, proj, cos_p, sin_p, *, kr_col, name):
    s = q.shape[0]
    heads = q.shape[1] // 256
    tq = _tile(s, (256, 128))

    def body(qn_ref, qp_ref, kn_ref, kp_ref, cq_ref, sq_ref, ck_ref, sk_ref, v_ref, o_ref):
        p = _attn_probs(qn_ref, qp_ref, kn_ref, kp_ref, cq_ref, sq_ref, ck_ref, sk_ref, tq)[0]
        o_ref[...] = _bdot(p.astype(BF16), v_ref[...].astype(BF16)).astype(BF16)

    return pl.pallas_call(
        body, name=name, grid=(heads, s // tq),
        in_specs=_attn_in_specs(s, tq, kr_col),
        out_specs=pl.BlockSpec((tq, 128), lambda h, i: (i, h)),
        out_shape=jax.ShapeDtypeStruct((s, heads * MLA_V), BF16),
        compiler_params=_cp("parallel", "parallel"),
    )(q, q, kv, proj, cos_p, sin_p, cos_p, sin_p, kv)


def _attn_bwd(q, kv, proj, cos_p, sin_p, do, *, kr_col, name):
    s = q.shape[0]
    heads = q.shape[1] // 256
    tq = _tile(s, (256, 128))

    def body(qn_ref, qp_ref, kn_ref, kp_ref, cq_ref, sq_ref, ck_ref, sk_ref, v_ref, do_ref, dq_ref, dkv_ref, dkp_ref):
        h = pl.program_id(0)
        i = pl.program_id(1)

        @pl.when(i == 0)
        def _():
            dkv_ref[...] = jnp.zeros_like(dkv_ref)

        @pl.when(jnp.logical_and(h == 0, i == 0))
        def _():
            dkp_ref[...] = jnp.zeros_like(dkp_ref)

        p, qn, qp, kn, kp, scale = _attn_probs(qn_ref, qp_ref, kn_ref, kp_ref, cq_ref, sq_ref, ck_ref, sk_ref, tq)
        dob = do_ref[...].astype(BF16)
        pb = p.astype(BF16)
        dv = _bdot(pb, dob, _TN)
        dp = _bdot(dob, v_ref[...].astype(BF16), _NT)
        ds = (p * (dp - jnp.sum(dp * p, axis=-1, keepdims=True)) * scale).astype(BF16)
        dq_ref[:, 0:128] = _bdot(ds, kn).astype(BF16)
        dq_ref[:, 128:256] = _rope_t(_bdot(ds, kp), cq_ref[...], sq_ref[...]).astype(BF16)
        dkv_ref[:, 0:128] += _bdot(ds, qn, _TN)
        dkv_ref[:, 128:256] += dv
        dkp_ref[...] += _rope_t(_bdot(ds, qp, _TN), ck_ref[...], sk_ref[...])

    return pl.pallas_call(
        body, name=name, grid=(heads, s // tq),
        in_specs=_attn_in_specs(s, tq, kr_col) + [pl.BlockSpec((tq, 128), lambda h, i: (i, h))],
        out_specs=[pl.BlockSpec((tq, 256), lambda h, i: (i, h)), pl.BlockSpec((s, 256), lambda h, i: (0, h)),
                   pl.BlockSpec((s, 128), lambda h, i: (0, 0))],
        out_shape=[jax.ShapeDtypeStruct((s, heads * 256), BF16), jax.ShapeDtypeStruct((s, heads * 256), F32),
                   jax.ShapeDtypeStruct((s, 128), F32)],
        compiler_params=_cp("arbitrary", "arbitrary"),
    )(q, q, kv, proj, cos_p, sin_p, cos_p, sin_p, kv, do)


def _rope_tables(positions):
    inv_freq = ROPE_THETA ** (-jnp.arange(0, MLA_ROPE, 2, dtype=F32) / MLA_ROPE)
    ang = positions.astype(F32)[:, None] * inv_freq
    cos, sin = jnp.cos(ang), jnp.sin(ang)
    zero = jnp.zeros((positions.shape[0], 128 - MLA_ROPE), F32)
    return jnp.concatenate([cos, cos, zero], axis=1), jnp.concatenate([-sin, sin, zero], axis=1)


def _odd_fwd(h, p, cos_p, sin_p, tag):
    rank = p["q_norm"].shape[0]
    hn = _rms_fwd(h, p["norm_mix"], name=tag + "_rms")
    proj = _mm(hn, p["w_in"], name=tag + "_in")
    cqn = _rms_fwd(proj, p["q_norm"], width=rank, col=0, name=tag + "_qn")
    ckvn = _rms_fwd(proj, p["kv_norm"], width=rank, col=1, name=tag + "_kvn")
    q = _mm(cqn, p["w_uq"], name=tag + "_uq")
    kv = _mm(ckvn, p["w_ukv"], name=tag + "_ukv")
    o = _attn_fwd(q, kv, proj, cos_p, sin_p, kr_col=2 * rank // 128, name=tag + "_attn")
    h1 = _mm(o, p["w_o"], residual=h, name=tag + "_o")
    return h1, (h, hn, proj, cqn, ckvn, q, kv, o)


def _odd_bwd(dh1, p, cos_p, sin_p, saved, tag):
    h, hn, proj, cqn, ckvn, q, kv, o = saved
    rank = p["q_norm"].shape[0]
    g = {}
    do = _mm(dh1, p["w_o"], tb=True, name=tag + "_do")
    g["w_o"] = _mm(o, dh1, ta=True, name=tag + "_dwo")
    dq, dkv, dkp = _attn_bwd(q, kv, proj, cos_p, sin_p, do, kr_col=2 * rank // 128, name=tag + "_attnb")
    dcqn = _mm(dq, p["w_uq"], tb=True, name=tag + "_dcqn")
    g["w_uq"] = _mm(cqn, dq, ta=True, name=tag + "_dwuq")
    dckvn = _mm(dkv, p["w_ukv"], tb=True, name=tag + "_dckvn")
    g["w_ukv"] = _mm(ckvn, dkv, ta=True, name=tag + "_dwukv")
    dcq, dqn = _rms_bwd(proj, p["q_norm"], dcqn, width=rank, col=0, out_dtype=BF16, name=tag + "_qnb")
    dckv, dkvn = _rms_bwd(proj, p["kv_norm"], dckvn, width=rank, col=1, out_dtype=BF16, name=tag + "_kvnb")
    g["q_norm"] = dqn[0]
    g["kv_norm"] = dkvn[0]
    dproj = jnp.concatenate([dcq, dckv, dkp.astype(BF16)], axis=1)
    dhn = _mm(dproj, p["w_in"], tb=True, name=tag + "_dhn")
    g["w_in"] = _mm(hn, dproj, ta=True, name=tag + "_dwin")
    dh, dnm = _rms_bwd(h, p["norm_mix"], dhn, add=dh1, name=tag + "_rmsb")
    g["norm_mix"] = dnm[0]
    return dh, g


PACK_W = 1024
N_CHIPS = 4
_MESH = pl.DeviceIdType.MESH
_ANY = pl.BlockSpec(memory_space=pl.ANY)


def _place():
    x, y, c = lax.axis_index("x"), lax.axis_index("y"), lax.axis_index("c")
    others = [(1 - x, y), (x, 1 - y), (1 - x, 1 - y)]
    return x, y, c, others


def _row_tile(r, c, itemsize):
    for cand in (512, 256, 128, 64, 32, 16, 8):
        if r % cand == 0 and cand * c * itemsize <= 2 * 1024 * 1024:
            return cand
    return r


def _slot_index(slot_axis, slot, layer, i):
    return (slot, layer, i, 0) if slot_axis == 0 else (layer, slot, i, 0)


def _cast_place(w, ck, *, slot_axis, dtype, name):
    nl, r, c = w.shape
    tr = _row_tile(r, c, 4)
    shape = (N_CHIPS, nl, r, c) if slot_axis == 0 else (nl, N_CHIPS, r, c)

    def body(ck_ref, w_ref, o_ref):
        o_ref[...] = w_ref[...].astype(dtype)

    return pl.pallas_call(
        body, name=name,
        grid_spec=pltpu.PrefetchScalarGridSpec(
            num_scalar_prefetch=1, grid=(nl, r // tr),
            in_specs=[pl.BlockSpec((None, tr, c), lambda l, i, s: (l, i, 0))],
            out_specs=pl.BlockSpec((None, None, tr, c), lambda l, i, s: _slot_index(slot_axis, s[1], l, i))),
        out_shape=jax.ShapeDtypeStruct(shape, dtype),
        compiler_params=_cp("parallel", "parallel"),
    )(ck, w)


def _region(ref, slot_axis, slot, half):
    lh = ref.shape[1 - slot_axis] // 2
    if slot_axis == 0:
        return ref.at[slot, pl.ds(half * lh, lh)]
    return ref.at[pl.ds(half * lh, lh), slot]


def _all_gather_multi(bufs, slot_axes, *, name):
    n = len(bufs)

    def body(*refs):
        outs = refs[n:2 * n]
        send_sems, recv_sems = refs[2 * n], refs[2 * n + 1]
        x, y, c, others = _place()
        k = 2 * x + y
        sibling = (x, y, 1 - c)

        def copy(a, slot, half, sem, to):
            blk = _region(outs[a], slot_axes[a], slot, half)
            return pltpu.make_async_remote_copy(src_ref=blk, dst_ref=blk, send_sem=send_sems.at[6 * a + sem],
                                                recv_sem=recv_sems.at[6 * a + sem], device_id=to, device_id_type=_MESH)

        first = [copy(a, k, c, j, (cx, cy, c)) for a in range(n) for j, (cx, cy) in enumerate(others)]
        for cp in first:
            cp.start()
        passed = []
        for a in range(n):
            for j, (cx, cy) in enumerate(others):
                copy(a, 2 * cx + cy, c, j, (cx, cy, c)).wait_recv()
                fw = copy(a, 2 * cx + cy, c, 3 + j, sibling)
                fw.start()
                passed.append(fw)
        for a in range(n):
            for j, (cx, cy) in enumerate(others):
                copy(a, 2 * cx + cy, 1 - c, 3 + j, sibling).wait_recv()
        for cp in first + passed:
            cp.wait_send()

    return pl.pallas_call(
        body, name=name,
        in_specs=[_ANY] * n, out_specs=[_ANY] * n,
        out_shape=[jax.ShapeDtypeStruct(b.shape, b.dtype) for b in bufs],
        input_output_aliases={a: a for a in range(n)},
        scratch_shapes=[pltpu.SemaphoreType.DMA((6 * n,)), pltpu.SemaphoreType.DMA((6 * n,))],
    )(*bufs)


def _half_shape(shape, slot_axis):
    shape = list(shape)
    shape[1 - slot_axis] //= 2
    return tuple(shape)


def _pair_exchange_multi(gs, slot_axes, *, name):
    n = len(gs)

    def body(*refs):
        g_refs, a_refs = refs[:n], refs[n:2 * n]
        send_sems, recv_sems = refs[2 * n], refs[2 * n + 1]
        x, y, c, _ = _place()
        copies = []
        for a in range(n):
            lh = a_refs[a].shape[1 - slot_axes[a]]
            src = g_refs[a].at[:, pl.ds((1 - c) * lh, lh)] if slot_axes[a] == 0 else g_refs[a].at[pl.ds((1 - c) * lh, lh)]
            copies.append(pltpu.make_async_remote_copy(
                src_ref=src, dst_ref=a_refs[a], send_sem=send_sems.at[a], recv_sem=recv_sems.at[a],
                device_id=(x, y, 1 - c), device_id_type=_MESH))
        for cp in copies:
            cp.start()
        for cp in copies:
            cp.wait()

    return pl.pallas_call(
        body, name=name, in_specs=[_ANY] * n, out_specs=[_ANY] * n,
        out_shape=[jax.ShapeDtypeStruct(_half_shape(g.shape, ax), g.dtype) for g, ax in zip(gs, slot_axes)],
        scratch_shapes=[pltpu.SemaphoreType.DMA((n,)), pltpu.SemaphoreType.DMA((n,))],
    )(*gs)


def _pair_add(g, a, ck, *, slot_axis, name):
    lh = a.shape[1 - slot_axis]
    r, c = a.shape[2:]
    tr = _row_tile(r, c, 4)

    def body(ck_ref, g_ref, a_ref, t_ref, own_ref):
        v = g_ref[...] + a_ref[...]
        t_ref[...] = v.astype(BF16)

        @pl.when(pl.program_id(2) == ck_ref[1])
        def _():
            own_ref[...] = v

    blk = (None, None, tr, c)
    return pl.pallas_call(
        body, name=name,
        grid_spec=pltpu.PrefetchScalarGridSpec(
            num_scalar_prefetch=1, grid=(lh, r // tr, N_CHIPS),
            in_specs=[pl.BlockSpec(blk, lambda l, i, j, s: _slot_index(slot_axis, j, s[0] * lh + l, i)),
                      pl.BlockSpec(blk, lambda l, i, j, s: _slot_index(slot_axis, j, l, i))],
            out_specs=[pl.BlockSpec(blk, lambda l, i, j, s: _slot_index(slot_axis, j, l, i)),
                       pl.BlockSpec((None, tr, c), lambda l, i, j, s: (l, i, 0))]),
        out_shape=[jax.ShapeDtypeStruct(a.shape, BF16), jax.ShapeDtypeStruct((lh, r, c), F32)],
        compiler_params=_cp("arbitrary", "arbitrary", "arbitrary"),
    )(ck, g, a)


def _chip_exchange_multi(ts, slot_axes, *, name):
    n = len(ts)

    def body(*refs):
        t_refs, b_refs = refs[:n], refs[n:2 * n]
        send_sems, recv_sems = refs[2 * n], refs[2 * n + 1]
        x, y, c, others = _place()
        copies = []
        for a in range(n):
            for j, (cx, cy) in enumerate(others):
                src = t_refs[a].at[2 * cx + cy] if slot_axes[a] == 0 else t_refs[a].at[:, 2 * cx + cy]
                copies.append(pltpu.make_async_remote_copy(
                    src_ref=src, dst_ref=b_refs[a].at[j], send_sem=send_sems.at[3 * a + j], recv_sem=recv_sems.at[3 * a + j],
                    device_id=(cx, cy, c), device_id_type=_MESH))
        for cp in copies:
            cp.start()
        for cp in copies:
            cp.wait()

    def out_shape(t, ax):
        lh = t.shape[1 - ax]
        return jax.ShapeDtypeStruct((N_CHIPS - 1, lh) + t.shape[2:], t.dtype)

    return pl.pallas_call(
        body, name=name, in_specs=[_ANY] * n, out_specs=[_ANY] * n,
        out_shape=[out_shape(t, ax) for t, ax in zip(ts, slot_axes)],
        scratch_shapes=[pltpu.SemaphoreType.DMA((3 * n,)), pltpu.SemaphoreType.DMA((3 * n,))],
    )(*ts)


def _chip_add(own, b, ck, *, name):
    lh, r, c = own.shape
    tr = _row_tile(r, c, 4)

    def body(ck_ref, o_ref, b_ref, r_ref):
        acc = o_ref[...]
        for j in range(N_CHIPS - 1):
            acc = acc + b_ref[j].astype(F32)
        r_ref[...] = acc

    return pl.pallas_call(
        body, name=name,
        grid_spec=pltpu.PrefetchScalarGridSpec(
            num_scalar_prefetch=1, grid=(lh, r // tr),
            in_specs=[pl.BlockSpec((None, tr, c), lambda l, i, s: (l, i, 0)),
                      pl.BlockSpec((N_CHIPS - 1, None, tr, c), lambda l, i, s: (0, l, i, 0))],
            out_specs=pl.BlockSpec((None, tr, c), lambda l, i, s: (s[0] * lh + l, i, 0))),
        out_shape=jax.ShapeDtypeStruct((2 * lh, r, c), F32),
        compiler_params=_cp("parallel", "parallel"),
    )(ck, own, b)


def _pair_share_multi(finals, *, name):
    n = len(finals)

    def body(*refs):
        outs = refs[n:2 * n]
        send_sems, recv_sems = refs[2 * n], refs[2 * n + 1]
        x, y, c, _ = _place()

        def copy(a, half):
            lh = outs[a].shape[0] // 2
            blk = outs[a].at[pl.ds(half * lh, lh)]
            return pltpu.make_async_remote_copy(src_ref=blk, dst_ref=blk, send_sem=send_sems.at[a], recv_sem=recv_sems.at[a],
                                                device_id=(x, y, 1 - c), device_id_type=_MESH)

        sends = [copy(a, c) for a in range(n)]
        for cp in sends:
            cp.start()
        for a in range(n):
            copy(a, 1 - c).wait_recv()
        for cp in sends:
            cp.wait_send()

    return pl.pallas_call(
        body, name=name, in_specs=[_ANY] * n, out_specs=[_ANY] * n,
        out_shape=[jax.ShapeDtypeStruct(f.shape, f.dtype) for f in finals],
        input_output_aliases={a: a for a in range(n)},
        scratch_shapes=[pltpu.SemaphoreType.DMA((n,)), pltpu.SemaphoreType.DMA((n,))],
    )(*finals)


def _col_pieces(cs, segments):
    pieces = []
    for k in range(N_CHIPS):
        for gs, ge, oi, ds in segments:
            lo, hi = max(k * cs, gs), min((k + 1) * cs, ge)
            if lo < hi:
                pieces.append((k, lo - k * cs, oi, ds + lo - gs, hi - lo))
    return pieces


def _assemble(f, layer, pieces, widths, *, name):
    _, _, r, c = f.shape
    tr = _row_tile(r, max(max(widths), N_CHIPS * c), f.dtype.itemsize)
    covered = sum(p[4] for p in pieces) == sum(widths)

    def body(f_ref, *o_refs):
        if not covered:
            for o in o_refs:
                o[...] = jnp.zeros_like(o)
        for k, s0, oi, d0, wd in pieces:
            o_refs[oi][:, d0:d0 + wd] = f_ref[k, :, s0:s0 + wd]

    return pl.pallas_call(
        body, name=name, grid=(r // tr,),
        in_specs=[pl.BlockSpec((N_CHIPS, None, tr, c), lambda i: (0, layer, i, 0))],
        out_specs=[pl.BlockSpec((tr, w), lambda i: (i, 0)) for w in widths],
        out_shape=[jax.ShapeDtypeStruct((r, w), f.dtype) for w in widths],
        compiler_params=_cp("parallel"),
    )(f)


def _split(fulls, layer, pieces, g_prev, shape, *, name):
    _, _, r, c = shape
    widths = [t.shape[1] for t in fulls]
    tr = _row_tile(r, max(max(widths), N_CHIPS * c), 4)
    nf = len(fulls)

    def body(*refs):
        g_ref = refs[-1]
        for k, s0, oi, d0, wd in pieces:
            g_ref[k, :, s0:s0 + wd] = refs[oi][:, d0:d0 + wd]

    in_specs = [pl.BlockSpec((tr, w), lambda i: (i, 0)) for w in widths]
    args = list(fulls)
    aliases = {}
    if g_prev is not None:
        in_specs.append(_ANY)
        args.append(g_prev)
        aliases = {nf: 0}
    return pl.pallas_call(
        body, name=name, grid=(r // tr,),
        in_specs=in_specs,
        out_specs=pl.BlockSpec((N_CHIPS, None, tr, c), lambda i: (0, layer, i, 0)),
        out_shape=jax.ShapeDtypeStruct(shape, F32),
        input_output_aliases=aliases,
        compiler_params=_cp("parallel"),
    )(*args)


def _all_reduce_small(v, *, name):
    r, w = v.shape
    n_dev = 8

    def body(x_ref, sum_ref, out_ref, send_sems, recv_sems, local_sem):
        x, y, c, others = _place()
        me, sibling = (x, y, c), (x, y, 1 - c)

        def rows(px, py, pc):
            return out_ref.at[pl.ds((4 * px + 2 * py + pc) * r, r), :]

        def copy(k, block, to, src=None):
            return pltpu.make_async_remote_copy(
                src_ref=rows(*block) if src is None else src, dst_ref=rows(*block),
                send_sem=send_sems.at[k], recv_sem=recv_sems.at[k], device_id=to, device_id_type=_MESH)

        mine = pltpu.make_async_copy(x_ref, rows(*me), local_sem)
        mine.start()
        first = [copy(0, me, sibling, src=x_ref)]
        first += [copy(1 + j, me, (*chip, c), src=x_ref) for j, chip in enumerate(others)]
        for cp in first:
            cp.start()
        passed = [copy(4 + j, (*chip, c), sibling) for j, chip in enumerate(others)]
        for j, chip in enumerate(others):
            copy(1 + j, (*chip, c), me).wait_recv()
            passed[j].start()
        copy(0, sibling, me).wait_recv()
        for j, chip in enumerate(others):
            copy(4 + j, (*chip, 1 - c), me).wait_recv()
        for cp in first + passed:
            cp.wait_send()
        mine.wait()
        acc = out_ref[pl.ds(0, r), :]
        for dev in range(1, n_dev):
            acc = acc + out_ref[pl.ds(dev * r, r), :]
        sum_ref[...] = acc

    vmem = pl.BlockSpec(memory_space=pltpu.VMEM)
    return pl.pallas_call(
        body, name=name, in_specs=[vmem], out_specs=[vmem, vmem],
        out_shape=[jax.ShapeDtypeStruct((r, w), F32), jax.ShapeDtypeStruct((n_dev * r, w), F32)],
        scratch_shapes=[pltpu.SemaphoreType.DMA((7,)), pltpu.SemaphoreType.DMA((7,)), pltpu.SemaphoreType.DMA],
        compiler_params=pltpu.CompilerParams(vmem_limit_bytes=V7X_VMEM_LIMIT),
    )(v)[0]


def _adamw(w, g, m, v, *, name):
    shape = w.shape
    cols = shape[-1]
    rows = max(1, math.prod(shape[:-1]))
    tr = rows
    for cand in (512, 256, 128, 64, 32, 16, 8):
        if rows % cand == 0 and cand * cols * 4 <= 2 * 1024 * 1024:
            tr = cand
            break
    c1 = 1.0 - ADAM_B1 ** ADAM_STEP
    c2 = 1.0 - ADAM_B2 ** ADAM_STEP

    def body(w_ref, g_ref, m_ref, v_ref, d_ref, mo_ref, vo_ref):
        gv = g_ref[...]
        mn = ADAM_B1 * m_ref[...] + (1.0 - ADAM_B1) * gv
        vn = ADAM_B2 * v_ref[...] + (1.0 - ADAM_B2) * (gv * gv)
        d_ref[...] = -ADAM_LR * ((mn / c1) / (jnp.sqrt(vn / c2) + ADAM_EPS) + ADAM_WD * w_ref[...])
        mo_ref[...] = mn
        vo_ref[...] = vn

    spec = pl.BlockSpec((tr, cols), lambda i: (i, 0))
    outs = pl.pallas_call(
        body, name=name, grid=(rows // tr,),
        in_specs=[spec] * 4, out_specs=[spec] * 3,
        out_shape=[jax.ShapeDtypeStruct((rows, cols), F32)] * 3,
        compiler_params=_cp("parallel"),
    )(*[t.reshape(rows, cols) for t in (w, g, m, v)])
    return [o.reshape(shape) for o in outs]


_WEIGHTS = ["norm_mix", "norm_ffn", "norm_final", "ev_w_in", "ev_gm_ln_g", "ev_gm_ln_b", "ev_gm_ws", "ev_gm_bs",
            "ev_conv_w", "ev_conv_b", "ev_dt_bias", "ev_a_log", "ev_d_skip", "ev_ssm_norm_w", "ev_w_out", "od_w_in",
            "od_q_norm", "od_kv_norm", "od_w_uq", "od_w_ukv", "od_w_o", "ff_w_up", "ff_conv_w", "ff_conv_b", "ff_w_down"]
_BIG = {"ev_w_in": -1, "ev_w_out": -2, "od_w_in": -2, "od_w_uq": -1, "od_w_ukv": -1, "od_w_o": -2,
        "ff_w_up": -1, "ff_w_down": -2}
_SMALL = {"ev_gm_ln_g": -1, "ev_gm_ln_b": -1, "ev_conv_w": -1, "od_q_norm": -1, "od_kv_norm": -1, "ff_conv_w": -1}
_SHARDED = {**_BIG, **_SMALL}
_REPLICATED = [n for n in _WEIGHTS if n not in _SHARDED]
N_CHUNKS = 4


def _from_slabs(slabs, axis):
    t = jnp.moveaxis(slabs, 0, axis - 1)
    shape = list(t.shape)
    if axis == -1:
        return t.reshape(shape[:-2] + [shape[-2] * shape[-1]])
    return t.reshape(shape[:-3] + [shape[-3] * shape[-2], shape[-1]])


def _to_slabs(full, axis):
    shape = list(full.shape)
    if axis == -1:
        t = full.reshape(shape[:-1] + [N_CHIPS, shape[-1] // N_CHIPS])
    else:
        t = full.reshape(shape[:-2] + [N_CHIPS, shape[-2] // N_CHIPS, shape[-1]])
    return jnp.moveaxis(t, axis - 1, 0)


def _reduce_scatter(gs, slot_axes, names, ck):
    theirs = _pair_exchange_multi(gs, slot_axes, name="rs_px")
    ts, owns = [], []
    for g, a, ax, n in zip(gs, theirs, slot_axes, names):
        t, own = _pair_add(g, a, ck, slot_axis=ax, name="rs_pa_" + n)
        ts.append(t)
        owns.append(own)
    bs = _chip_exchange_multi(ts, slot_axes, name="rs_cx")
    finals = [_chip_add(own, b, ck, name="rs_ca_" + n) for own, b, n in zip(owns, bs, names)]
    return _pair_share_multi(finals, name="rs_ps")


def _all_reduce(arrs, tag):
    n = sum(a.size for a in arrs)
    rows = -(-n // PACK_W)
    rows = -(-rows // 8) * 8
    flat = jnp.concatenate([a.astype(F32).reshape(-1) for a in arrs])
    flat = jnp.pad(flat, (0, rows * PACK_W - n)).reshape(rows, PACK_W)
    tot = _all_reduce_small(flat, name=tag).reshape(-1)
    res, off = [], 0
    for a in arrs:
        res.append(tot[off:off + a.size].reshape(a.shape))
        off += a.size
    return res


def _pad_cols(w, cols):
    return jnp.pad(w, ((0, 0), (0, cols - w.shape[1])))


def _as3(a):
    return a.reshape(a.shape[0], 1, a.shape[1]) if a.ndim == 2 else a


def _col_layout(name, f):
    cs = f.shape[3]
    total = N_CHIPS * cs
    if name == "ev_w_in":
        main = 4 * f.shape[2] + 2 * SSM_GROUPS * SSM_STATE
        return _col_pieces(cs, [(0, main, 0, 0), (main, total, 1, 0)]), [main, 128]
    if name == "od_w_uq":
        qk = MLA_NOPE + MLA_ROPE
        heads = total // qk
        return _col_pieces(cs, [(hd * qk, (hd + 1) * qk, 0, hd * 256) for hd in range(heads)]), [heads * 256]
    return _col_pieces(cs, [(0, total, 0, 0)]), [total]


def _rows_of(buf, j):
    return buf[j].reshape(N_CHIPS * buf.shape[2], buf.shape[3])


def _layer_params(full, gathered, layer):
    j = layer // 2
    tag = f"asm{layer}_"
    p = {"norm_mix": full["norm_mix"][layer]}
    if layer % 2 == 0:
        f_in = gathered["ev_w_in"]
        w_main, w_dt = _assemble(f_in, j, *_col_layout("ev_w_in", f_in), name=tag + "in")
        p.update(w_in_main=w_main, w_in_dt=w_dt,
                 gm_ln_g=full["ev_gm_ln_g"][j], gm_ln_b=full["ev_gm_ln_b"][j], gm_ws=full["ev_gm_ws"][j],
                 gm_bs=full["ev_gm_bs"][j], conv_w=full["ev_conv_w"][j], conv_b=full["ev_conv_b"][j],
                 dt_bias=full["ev_dt_bias"][j], a_log=full["ev_a_log"][j], d_skip=full["ev_d_skip"][j],
                 ssm_norm_w=full["ev_ssm_norm_w"][j], w_out=_rows_of(gathered["ev_w_out"], j))
    else:
        f_uq, f_ukv = gathered["od_w_uq"], gathered["od_w_ukv"]
        w_in = _rows_of(gathered["od_w_in"], j)
        p.update(w_in=_pad_cols(w_in, -(-w_in.shape[1] // 128) * 128), q_norm=full["od_q_norm"][j],
                 kv_norm=full["od_kv_norm"][j],
                 w_uq=_assemble(f_uq, j, *_col_layout("od_w_uq", f_uq), name=tag + "uq")[0],
                 w_ukv=_assemble(f_ukv, j, *_col_layout("od_w_ukv", f_ukv), name=tag + "ukv")[0],
                 w_o=_rows_of(gathered["od_w_o"], j))
    f_up = gathered["ff_w_up"]
    f = {"norm_ffn": full["norm_ffn"][layer],
         "w_up": _assemble(f_up, layer, *_col_layout("ff_w_up", f_up), name=tag + "up")[0],
         "conv_w": full["ff_conv_w"][layer], "conv_b": full["ff_conv_b"][layer],
         "w_down": _rows_of(gathered["ff_w_down"], layer)}
    return p, f


def _small_grads(g, gf, layer):
    out = {"norm_mix": g["norm_mix"], "norm_ffn": gf["norm_ffn"], "ff_conv_w": gf["conv_w"], "ff_conv_b": gf["conv_b"]}
    if layer % 2 == 0:
        out.update(ev_gm_ln_g=g["gm_ln_g"], ev_gm_ln_b=g["gm_ln_b"], ev_gm_ws=g["gm_ws"], ev_gm_bs=g["gm_bs"],
                   ev_conv_w=g["conv_w"], ev_conv_b=g["conv_b"], ev_dt_bias=g["dt_bias"], ev_a_log=g["a_log"],
                   ev_d_skip=g["d_skip"], ev_ssm_norm_w=g["ssm_norm_w"])
    else:
        out.update(od_q_norm=g["q_norm"], od_kv_norm=g["kv_norm"])
    return out


def _step(x, positions, loss_target, w, m, v):
    depth = w["norm_mix"].shape[0]
    h = x[0]
    tgt = loss_target[0]
    cos_p, sin_p = _rope_tables(positions[0])
    ck = jnp.stack([lax.axis_index("c"), 2 * lax.axis_index("x") + lax.axis_index("y")]).astype(jnp.int32)

    sharded = list(_SHARDED)
    slot_axes = [1 if _SHARDED[n] == -2 else 0 for n in sharded]
    bufs = [_cast_place(_as3(w[n]), ck, slot_axis=ax, dtype=BF16 if n in _BIG else F32, name="place_" + n)
            for n, ax in zip(sharded, slot_axes)]
    gathered = dict(zip(sharded, _all_gather_multi(bufs, slot_axes, name="ag")))
    full = {n: w[n] for n in _REPLICATED}
    for n in _SMALL:
        full[n] = _from_slabs(gathered[n], -1).reshape(w[n].shape[:-1] + (N_CHIPS * w[n].shape[-1],))

    params, saved = [], []
    for layer in range(depth):
        p, f = _layer_params(full, gathered, layer)
        if layer % 2 == 0:
            h, sv = _even_fwd(h, p, f"l{layer}m")
        else:
            h, sv = _odd_fwd(h, p, cos_p, sin_p, f"l{layer}m")
        h, svf = _ffn_fwd(h, f, f"l{layer}f")
        params.append((p, f))
        saved.append((sv, svf))

    loss, dh, dnf = _loss_bwd(h, w["norm_final"], tgt, name="loss")
    per_layer = []
    col_buf = {}
    row_parts = {n: {} for n in _BIG if _BIG[n] == -2}

    def split(name, fulls, j):
        f = gathered[name]
        col_buf[name] = _split(fulls, j, _col_layout(name, f)[0], col_buf.get(name), f.shape, name=f"split_{name}{j}")

    for layer in reversed(range(depth)):
        p, f = params[layer]
        sv, svf = saved[layer]
        j = layer // 2
        dh, gf = _ffn_bwd(dh, f, svf, f"l{layer}fb")
        split("ff_w_up", [gf["w_up"]], layer)
        row_parts["ff_w_down"][layer] = gf["w_down"]
        if layer % 2 == 0:
            dh, g = _even_bwd(dh, p, sv, f"l{layer}mb")
            split("ev_w_in", [g["w_in_main"], g["w_in_dt"]], j)
            row_parts["ev_w_out"][j] = g["w_out"]
        else:
            dh, g = _odd_bwd(dh, p, cos_p, sin_p, sv, f"l{layer}mb")
            split("od_w_uq", [g["w_uq"]], j)
            split("od_w_ukv", [g["w_ukv"]], j)
            row_parts["od_w_in"][j] = g["w_in"][:, :w["od_w_in"].shape[2]]
            row_parts["od_w_o"][j] = g["w_o"]
        per_layer.append((layer, _small_grads(g, gf, layer)))
    per_layer.sort(key=lambda t: t[0])
    local = {"norm_final": dnf[0]}
    for n in list(_SMALL) + _REPLICATED:
        if n != "norm_final":
            local[n] = jnp.stack([lg[n] for _, lg in per_layer if n in lg], axis=0)

    gs = []
    for n in sharded:
        if n in col_buf:
            gs.append(col_buf[n])
        elif n in row_parts:
            parts = row_parts[n]
            gs.append(jnp.stack([parts[i] for i in range(len(parts))], axis=0).reshape(gathered[n].shape))
        else:
            gs.append(_to_slabs(_as3(local[n]), -1))
    grads = dict(zip(sharded, _reduce_scatter(gs, slot_axes, sharded, ck)))
    grads.update(zip(_REPLICATED, _all_reduce([local[n] for n in _REPLICATED], "ar")))
    loss = lax.psum(loss[0, 0], ("x", "y", "c"))

    delta, new_m, new_v = {}, {}, {}
    for n in _WEIGHTS:
        grads[n] = grads[n].reshape(w[n].shape)
        delta[n], new_m[n], new_v[n] = _adamw(w[n], grads[n], m[n], v[n], name="adamw_" + n)
    return (loss, dh[None], *[grads[n] for n in _WEIGHTS], *[delta[n] for n in _WEIGHTS],
            *[new_m[n] for n in _WEIGHTS], *[new_v[n] for n in _WEIGHTS])


def kernel(x, positions, norm_mix, norm_ffn, norm_final, ev_w_in, ev_gm_ln_g, ev_gm_ln_b, ev_gm_ws, ev_gm_bs, ev_conv_w, ev_conv_b, ev_dt_bias, ev_a_log, ev_d_skip, ev_ssm_norm_w, ev_w_out, od_w_in, od_q_norm, od_kv_norm, od_w_uq, od_w_ukv, od_w_o, ff_w_up, ff_conv_w, ff_conv_b, ff_w_down, loss_target, m_norm_mix, m_norm_ffn, m_norm_final, m_ev_w_in, m_ev_gm_ln_g, m_ev_gm_ln_b, m_ev_gm_ws, m_ev_gm_bs, m_ev_conv_w, m_ev_conv_b, m_ev_dt_bias, m_ev_a_log, m_ev_d_skip, m_ev_ssm_norm_w, m_ev_w_out, m_od_w_in, m_od_q_norm, m_od_kv_norm, m_od_w_uq, m_od_w_ukv, m_od_w_o, m_ff_w_up, m_ff_conv_w, m_ff_conv_b, m_ff_w_down, v_norm_mix, v_norm_ffn, v_norm_final, v_ev_w_in, v_ev_gm_ln_g, v_ev_gm_ln_b, v_ev_gm_ws, v_ev_gm_bs, v_ev_conv_w, v_ev_conv_b, v_ev_dt_bias, v_ev_a_log, v_ev_d_skip, v_ev_ssm_norm_w, v_ev_w_out, v_od_w_in, v_od_q_norm, v_od_kv_norm, v_od_w_uq, v_od_w_ukv, v_od_w_o, v_ff_w_up, v_ff_conv_w, v_ff_conv_b, v_ff_w_down):
    ws = (norm_mix, norm_ffn, norm_final, ev_w_in, ev_gm_ln_g, ev_gm_ln_b, ev_gm_ws, ev_gm_bs, ev_conv_w, ev_conv_b, ev_dt_bias, ev_a_log, ev_d_skip, ev_ssm_norm_w, ev_w_out, od_w_in, od_q_norm, od_kv_norm, od_w_uq, od_w_ukv, od_w_o, ff_w_up, ff_conv_w, ff_conv_b, ff_w_down)
    ms = (m_norm_mix, m_norm_ffn, m_norm_final, m_ev_w_in, m_ev_gm_ln_g, m_ev_gm_ln_b, m_ev_gm_ws, m_ev_gm_bs, m_ev_conv_w, m_ev_conv_b, m_ev_dt_bias, m_ev_a_log, m_ev_d_skip, m_ev_ssm_norm_w, m_ev_w_out, m_od_w_in, m_od_q_norm, m_od_kv_norm, m_od_w_uq, m_od_w_ukv, m_od_w_o, m_ff_w_up, m_ff_conv_w, m_ff_conv_b, m_ff_w_down)
    vs = (v_norm_mix, v_norm_ffn, v_norm_final, v_ev_w_in, v_ev_gm_ln_g, v_ev_gm_ln_b, v_ev_gm_ws, v_ev_gm_bs, v_ev_conv_w, v_ev_conv_b, v_ev_dt_bias, v_ev_a_log, v_ev_d_skip, v_ev_ssm_norm_w, v_ev_w_out, v_od_w_in, v_od_q_norm, v_od_kv_norm, v_od_w_uq, v_od_w_ukv, v_od_w_o, v_ff_w_up, v_ff_conv_w, v_ff_conv_b, v_ff_w_down)
    return _step(x, positions, loss_target, dict(zip(_WEIGHTS, ws)), dict(zip(_WEIGHTS, ms)), dict(zip(_WEIGHTS, vs)))
```

```python
import functools
import math

import jax
import jax.numpy as jnp
from jax import lax
from jax.experimental import pallas as pl
from jax.experimental.pallas import tpu as pltpu

F32 = jnp.float32
BF16 = jnp.bfloat16
EPS = 1e-6
CHUNK = 64
BLK = 128
GM_GROUPS = 8
SSM_HEAD_DIM = 64
SSM_GROUPS = 4
SSM_STATE = 128
SSM_CONV = 4
FFN_CONV = 3
MLA_NOPE = 128
MLA_ROPE = 64
MLA_V = 128
ROPE_THETA = 10000.0
V7X_VMEM_LIMIT = 56 * 1024 * 1024
HI = lax.Precision.HIGHEST

ADAM_LR = 0.001
ADAM_B1 = 0.9
ADAM_B2 = 0.999
ADAM_EPS = 1e-08
ADAM_WD = 0.01
ADAM_STEP = 10


def _cp(*sem):
    return pltpu.CompilerParams(dimension_semantics=sem if sem else None, vmem_limit_bytes=V7X_VMEM_LIMIT)


def _tile(n, cands):
    for c in cands:
        if n % c == 0:
            return c
    return n


def _row(v):
    return v.reshape(1, -1).astype(F32)


_MM_TILES = (1536, 1408, 1280, 1152, 1024, 896, 768, 640, 512, 384, 256, 128)
_MM_VMEM_BUDGET = 40 * 1024 * 1024


def _mm(a, b, *, ta=False, tb=False, out_dtype=F32, residual=None, b_layer=None, out_slabs=None, out_layer=None, name):
    m, k = (a.shape[1], a.shape[0]) if ta else a.shape
    if b_layer is None:
        n = b.shape[0] if tb else b.shape[1]
        assert k == (b.shape[1] if tb else b.shape[0]), (a.shape, b.shape, ta, tb)
        n_unit = k_unit = None
    else:
        _, _, rows, cs = b.shape
        n = rows if tb else N_CHIPS * cs
        assert k == (N_CHIPS * cs if tb else rows), (a.shape, b.shape, ta, tb)
        n_unit, k_unit = (None, cs) if tb else (cs, None)
    if out_slabs is not None:
        n_unit = out_slabs[2][3]
        assert n == N_CHIPS * n_unit and m == out_slabs[2][2]
    tm = _tile(m, (1024, 512, 256, 128))
    tn = _tile(n_unit or n, _MM_TILES)
    tk = k if (k_unit is None and k <= 1536) else _tile(k_unit or k, (1408, 1024, 768, 512, 384, 256, 128))

    def vmem(tm_):
        bytes_ = 2 * (tm_ * tk * a.dtype.itemsize + tk * tn * b.dtype.itemsize + tm_ * tn * jnp.dtype(out_dtype).itemsize)
        bytes_ += tm_ * tn * 4 + (2 * tm_ * tn * residual.dtype.itemsize if residual is not None else 0)
        return bytes_

    while vmem(tm) > _MM_VMEM_BUDGET and tm % 256 == 0:
        tm //= 2
    nk = k // tk
    dn = (((0 if ta else 1,), (1 if tb else 0,)), ((), ()))
    has_res = residual is not None

    def body(*refs):
        a_ref, b_ref = refs[0], refs[1]
        r_ref = refs[2] if has_res else None
        o_ref, acc = refs[-2], refs[-1]
        kk = pl.program_id(2)

        @pl.when(kk == 0)
        def _():
            acc[...] = jnp.zeros_like(acc)

        acc[...] += lax.dot_general(a_ref[...].astype(BF16), b_ref[...].astype(BF16), dn,
                                    preferred_element_type=F32)

        @pl.when(kk == nk - 1)
        def _():
            r = acc[...]
            if has_res:
                r = r + r_ref[...].astype(F32)
            o_ref[...] = r.astype(out_dtype)

    a_spec = pl.BlockSpec((tk, tm), lambda i, j, kk: (kk, i)) if ta else pl.BlockSpec((tm, tk), lambda i, j, kk: (i, kk))
    if b_layer is None:
        b_spec = pl.BlockSpec((tn, tk), lambda i, j, kk: (j, kk)) if tb else pl.BlockSpec((tk, tn), lambda i, j, kk: (kk, j))
    elif tb:
        per = k_unit // tk
        b_spec = pl.BlockSpec((None, None, tn, tk), lambda i, j, kk: (kk // per, b_layer, j, kk % per))
    else:
        per = n_unit // tn
        b_spec = pl.BlockSpec((None, None, tk, tn), lambda i, j, kk: (j // per, b_layer, kk, j % per))
    in_specs = [a_spec, b_spec]
    args = [a, b]
    if has_res:
        in_specs.append(pl.BlockSpec((tm, tn), lambda i, j, kk: (i, j)))
        args.append(residual)
    aliases = {}
    if out_layer is not None:
        g_prev, layer, n_layers = out_layer
        out_spec = pl.BlockSpec((None, tm, tn), lambda i, j, kk: (layer, i, j))
        out_shape = jax.ShapeDtypeStruct((n_layers, m, n), out_dtype)
        if g_prev is not None:
            in_specs.append(_ANY)
            args.append(g_prev)
            aliases = {len(args) - 1: 0}
    elif out_slabs is None:
        out_spec = pl.BlockSpec((tm, tn), lambda i, j, kk: (i, j))
        out_shape = jax.ShapeDtypeStruct((m, n), out_dtype)
    else:
        g_prev, layer, shape = out_slabs
        per_o = n_unit // tn
        out_spec = pl.BlockSpec((None, None, tm, tn), lambda i, j, kk: (j // per_o, layer, i, j % per_o))
        out_shape = jax.ShapeDtypeStruct(shape, out_dtype)
        if g_prev is not None:
            in_specs.append(_ANY)
            args.append(g_prev)
            aliases = {len(args) - 1: 0}
    return pl.pallas_call(
        body, name=name,
        grid=(m // tm, n // tn, nk),
        in_specs=in_specs,
        out_specs=out_spec,
        out_shape=out_shape,
        input_output_aliases=aliases,
        scratch_shapes=[pltpu.VMEM((tm, tn), F32)],
        compiler_params=_cp("parallel", "parallel", "arbitrary"),
    )(*args)


def _rms_fwd(x, w, *, width=None, col=0, out_dtype=None, name):
    out_dtype = out_dtype or BF16
    s = x.shape[0]
    width = width or x.shape[1]
    tr = _tile(s, (256, 128))

    def body(x_ref, w_ref, o_ref):
        xv = x_ref[...]
        r = lax.rsqrt(jnp.mean(xv * xv, axis=-1, keepdims=True) + EPS)
        o_ref[...] = (xv * r * w_ref[...]).astype(out_dtype)

    return pl.pallas_call(
        body, name=name, grid=(s // tr,),
        in_specs=[pl.BlockSpec((tr, width), lambda i: (i, col)), pl.BlockSpec((1, width), lambda i: (0, 0))],
        out_specs=pl.BlockSpec((tr, width), lambda i: (i, 0)),
        out_shape=jax.ShapeDtypeStruct((s, width), out_dtype),
        compiler_params=_cp("parallel"),
    )(x, _row(w))


def _rms_bwd(x, w, dy, *, add=None, width=None, col=0, dy_col=0, out_dtype=F32, name):
    s = x.shape[0]
    width = width or x.shape[1]
    tr = _tile(s, (256, 128))
    has_add = add is not None

    def body(*refs):
        if has_add:
            x_ref, w_ref, dy_ref, add_ref, dx_ref, dw_ref = refs
        else:
            x_ref, w_ref, dy_ref, dx_ref, dw_ref = refs
        xv = x_ref[...]
        dyv = dy_ref[...].astype(F32)
        r = lax.rsqrt(jnp.mean(xv * xv, axis=-1, keepdims=True) + EPS)
        xh = xv * r
        g = dyv * w_ref[...]
        dx = r * (g - xh * jnp.mean(g * xh, axis=-1, keepdims=True))
        if has_add:
            dx = dx + add_ref[...]
        dx_ref[...] = dx.astype(out_dtype)

        @pl.when(pl.program_id(0) == 0)
        def _():
            dw_ref[...] = jnp.zeros_like(dw_ref)

        dw_ref[...] += jnp.sum(dyv * xh, axis=0, keepdims=True)

    in_specs = [pl.BlockSpec((tr, width), lambda i: (i, col)), pl.BlockSpec((1, width), lambda i: (0, 0)),
                pl.BlockSpec((tr, width), lambda i: (i, dy_col))]
    args = [x, _row(w), dy]
    if has_add:
        in_specs.append(pl.BlockSpec((tr, width), lambda i: (i, 0)))
        args.append(add)
    return pl.pallas_call(
        body, name=name, grid=(s // tr,),
        in_specs=in_specs,
        out_specs=[pl.BlockSpec((tr, width), lambda i: (i, 0)), pl.BlockSpec((1, width), lambda i: (0, 0))],
        out_shape=[jax.ShapeDtypeStruct((s, width), out_dtype), jax.ShapeDtypeStruct((1, width), F32)],
        compiler_params=_cp("arbitrary"),
    )(*args)


def _loss_bwd(h, w, tgt, *, name):
    s, d = h.shape
    tr = _tile(s, (256, 128))

    def body(x_ref, w_ref, t_ref, loss_ref, dx_ref, dw_ref):
        xv = x_ref[...]
        r = lax.rsqrt(jnp.mean(xv * xv, axis=-1, keepdims=True) + EPS)
        xh = xv * r
        e = xh * w_ref[...] - t_ref[...]
        part = 0.5 * jnp.sum(jnp.mean(e * e, axis=-1, keepdims=True), axis=0, keepdims=True)
        dyv = e * (1.0 / d)
        g = dyv * w_ref[...]
        dx_ref[...] = r * (g - xh * jnp.mean(g * xh, axis=-1, keepdims=True))

        @pl.when(pl.program_id(0) == 0)
        def _():
            dw_ref[...] = jnp.zeros_like(dw_ref)
            loss_ref[...] = jnp.zeros_like(loss_ref)

        dw_ref[...] += jnp.sum(dyv * xh, axis=0, keepdims=True)
        loss_ref[...] += jnp.broadcast_to(part, loss_ref.shape)

    return pl.pallas_call(
        body, name=name, grid=(s // tr,),
        in_specs=[pl.BlockSpec((tr, d), lambda i: (i, 0)), pl.BlockSpec((1, d), lambda i: (0, 0)),
                  pl.BlockSpec((tr, d), lambda i: (i, 0))],
        out_specs=[pl.BlockSpec((1, 128), lambda i: (0, 0)), pl.BlockSpec((tr, d), lambda i: (i, 0)),
                   pl.BlockSpec((1, d), lambda i: (0, 0))],
        out_shape=[jax.ShapeDtypeStruct((1, 128), F32), jax.ShapeDtypeStruct((s, d), F32),
                   jax.ShapeDtypeStruct((1, d), F32)],
        compiler_params=_cp("arbitrary"),
    )(h, _row(w), tgt)


_G0 = math.sqrt(2.0 / math.pi)
_G1 = 0.044715


def _gelu(x):
    return 0.5 * x * (1.0 + jnp.tanh(_G0 * (x + _G1 * x * x * x)))


def _gelu_and_grad(x):
    th = jnp.tanh(_G0 * (x + _G1 * x * x * x))
    val = 0.5 * x * (1.0 + th)
    grad = 0.5 * (1.0 + th) + 0.5 * x * (1.0 - th * th) * _G0 * (1.0 + 3.0 * _G1 * x * x)
    return val, grad


def _sigmoid(x):
    return 1.0 / (1.0 + jnp.exp(-x))


def _shift_down(x, k):
    if k == 0:
        return x
    rows = lax.broadcasted_iota(jnp.int32, x.shape, 0)
    return jnp.where(rows >= k, pltpu.roll(x, k, 0), 0.0)


def _shift_up(x, k):
    if k == 0:
        return x
    n = x.shape[0]
    rows = lax.broadcasted_iota(jnp.int32, x.shape, 0)
    return jnp.where(rows < n - k, pltpu.roll(x, n - k, 0), 0.0)


def _conv_rows(x, w_ref, b_ref, kw):
    y = b_ref[...] + w_ref[kw - 1:kw, :] * x
    for k in range(kw - 1):
        y = y + w_ref[k:k + 1, :] * _shift_down(x, kw - 1 - k)
    return y


def _conv_rows_bwd(x, dgc, w_ref, dw_ref, db_ref, kw):
    dx = w_ref[kw - 1:kw, :] * dgc
    dw_ref[kw - 1:kw, :] = jnp.sum(dgc * x, axis=0, keepdims=True)
    for k in range(kw - 1):
        sh = kw - 1 - k
        dx = dx + w_ref[k:k + 1, :] * _shift_up(dgc, sh)
        dw_ref[k:k + 1, :] = jnp.sum(dgc * _shift_down(x, sh), axis=0, keepdims=True)
    db_ref[...] = jnp.sum(dgc, axis=0, keepdims=True)
    return dx


def _ffn_mid_fwd(up, conv_w, conv_b, *, name):
    s = up.shape[0]
    f = up.shape[1] // 2
    tc = _tile(f, (256, 128))
    nf = f // tc

    def body(g_ref, v_ref, w_ref, b_ref, o_ref):
        gc = _conv_rows(g_ref[...], w_ref, b_ref, FFN_CONV)
        o_ref[...] = (_gelu(gc) * v_ref[...]).astype(BF16)

    return pl.pallas_call(
        body, name=name, grid=(nf,),
        in_specs=[pl.BlockSpec((s, tc), lambda j: (0, j)), pl.BlockSpec((s, tc), lambda j: (0, j + nf)),
                  pl.BlockSpec((FFN_CONV, tc), lambda j: (0, j)), pl.BlockSpec((1, tc), lambda j: (0, j))],
        out_specs=pl.BlockSpec((s, tc), lambda j: (0, j)),
        out_shape=jax.ShapeDtypeStruct((s, f), BF16),
        compiler_params=_cp("parallel"),
    )(up, up, conv_w, _row(conv_b))


def _ffn_mid_bwd(up, conv_w, conv_b, da, *, name):
    s = up.shape[0]
    f = up.shape[1] // 2
    tc = _tile(f, (256, 128))
    nf = f // tc

    def body(g_ref, v_ref, w_ref, b_ref, da_ref, dg_ref, dv_ref, dw_ref, db_ref):
        g = g_ref[...]
        gc = _conv_rows(g, w_ref, b_ref, FFN_CONV)
        gel, dgel = _gelu_and_grad(gc)
        dav = da_ref[...]
        dv_ref[...] = (dav * gel).astype(BF16)
        dgc = dav * v_ref[...] * dgel
        dg_ref[...] = _conv_rows_bwd(g, dgc, w_ref, dw_ref, db_ref, FFN_CONV).astype(BF16)

    col = lambda j: (0, j)
    return pl.pallas_call(
        body, name=name, grid=(nf,),
        in_specs=[pl.BlockSpec((s, tc), col), pl.BlockSpec((s, tc), lambda j: (0, j + nf)),
                  pl.BlockSpec((FFN_CONV, tc), col), pl.BlockSpec((1, tc), col), pl.BlockSpec((s, tc), col)],
        out_specs=[pl.BlockSpec((s, tc), col), pl.BlockSpec((s, tc), col),
                   pl.BlockSpec((FFN_CONV, tc), col), pl.BlockSpec((1, tc), col)],
        out_shape=[jax.ShapeDtypeStruct((s, f), BF16), jax.ShapeDtypeStruct((s, f), BF16),
                   jax.ShapeDtypeStruct((FFN_CONV, f), F32), jax.ShapeDtypeStruct((1, f), F32)],
        compiler_params=_cp("parallel"),
    )(up, up, conv_w, _row(conv_b), da)


def _gm_mask():
    r = lax.broadcasted_iota(jnp.int32, (BLK, BLK), 0) // CHUNK
    c = lax.broadcasted_iota(jnp.int32, (BLK, BLK), 1) // CHUNK
    return r >= c


def _gm_specs(s, gd):
    nb = s // BLK
    u_spec = pl.BlockSpec((BLK, gd), lambda g, n: (n, g))
    v_spec = pl.BlockSpec((BLK, gd), lambda g, n: (n, g + GM_GROUPS))
    vec_spec = pl.BlockSpec((1, gd), lambda g, n: (0, g))
    ws_spec = pl.BlockSpec((1, BLK, BLK), lambda g, n: (g, 0, 0))
    bs_spec = pl.BlockSpec((1, BLK, 1), lambda g, n: (g, 0, 0))
    return nb, u_spec, v_spec, vec_spec, ws_spec, bs_spec


def _gm_fwd(proj, ln_g, ln_b, ws, bs, *, name):
    s = proj.shape[0]
    gd = ln_g.shape[-1]
    w = GM_GROUPS * gd
    nb, u_spec, v_spec, vec_spec, ws_spec, bs_spec = _gm_specs(s, gd)

    def body(u_ref, v_ref, lg_ref, lb_ref, ws_ref, bs_ref, o_ref):
        ua = _gelu(u_ref[...])
        va = _gelu(v_ref[...])
        mu = jnp.mean(va, axis=-1, keepdims=True)
        vc = va - mu
        var = jnp.mean(vc * vc, axis=-1, keepdims=True)
        vn = vc * lax.rsqrt(var + EPS) * lg_ref[...] + lb_ref[...]
        wm = jnp.where(_gm_mask(), ws_ref[0], 0.0).astype(BF16)
        gate = jnp.dot(wm, vn.astype(BF16), preferred_element_type=F32) + bs_ref[0]
        o_ref[...] = (ua * gate).astype(BF16)

    return pl.pallas_call(
        body, name=name, grid=(GM_GROUPS, nb),
        in_specs=[u_spec, v_spec, vec_spec, vec_spec, ws_spec, bs_spec],
        out_specs=pl.BlockSpec((BLK, gd), lambda g, n: (n, g)),
        out_shape=jax.ShapeDtypeStruct((s, w), BF16),
        compiler_params=_cp("parallel", "parallel"),
    )(proj, proj, ln_g.reshape(1, w), ln_b.reshape(1, w), ws, bs.reshape(GM_GROUPS, BLK, 1))


def _gm_bwd(proj, ln_g, ln_b, ws, bs, dya, *, name):
    s = proj.shape[0]
    gd = ln_g.shape[-1]
    w = GM_GROUPS * gd
    nb, u_spec, v_spec, vec_spec, ws_spec, bs_spec = _gm_specs(s, gd)

    def body(u_ref, v_ref, lg_ref, lb_ref, ws_ref, bs_ref, dy_ref, du_ref, dv_ref, dlg_ref, dlb_ref, dws_ref, dbs_ref):
        ua, dua_du = _gelu_and_grad(u_ref[...])
        va, dva_dv = _gelu_and_grad(v_ref[...])
        mu = jnp.mean(va, axis=-1, keepdims=True)
        vc = va - mu
        var = jnp.mean(vc * vc, axis=-1, keepdims=True)
        rstd = lax.rsqrt(var + EPS)
        xh = vc * rstd
        vn = (xh * lg_ref[...] + lb_ref[...]).astype(BF16)
        mask = _gm_mask()
        wm = jnp.where(mask, ws_ref[0], 0.0).astype(BF16)
        gate = jnp.dot(wm, vn, preferred_element_type=F32) + bs_ref[0]
        dy = dy_ref[...]
        du_ref[...] = (dy * gate * dua_du).astype(BF16)
        dgate = dy * ua
        dgb = dgate.astype(BF16)
        dwm = lax.dot_general(dgb, vn, (((1,), (1,)), ((), ())), preferred_element_type=F32)
        dvn = lax.dot_general(wm, dgb, (((0,), (0,)), ((), ())), preferred_element_type=F32)
        dxh = dvn * lg_ref[...]
        dva = rstd * (dxh - jnp.mean(dxh, axis=-1, keepdims=True) - xh * jnp.mean(dxh * xh, axis=-1, keepdims=True))
        dv_ref[...] = (dva * dva_dv).astype(BF16)

        @pl.when(pl.program_id(1) == 0)
        def _():
            dlg_ref[...] = jnp.zeros_like(dlg_ref)
            dlb_ref[...] = jnp.zeros_like(dlb_ref)
            dws_ref[...] = jnp.zeros_like(dws_ref)
            dbs_ref[...] = jnp.zeros_like(dbs_ref)

        dlg_ref[...] += jnp.sum(dvn * xh, axis=0, keepdims=True)
        dlb_ref[...] += jnp.sum(dvn, axis=0, keepdims=True)
        dws_ref[0] += jnp.where(mask, dwm, 0.0)
        dbs_ref[0] += jnp.sum(dgate, axis=-1, keepdims=True)

    out_uv = pl.BlockSpec((BLK, gd), lambda g, n: (n, g))
    return pl.pallas_call(
        body, name=name, grid=(GM_GROUPS, nb),
        in_specs=[u_spec, v_spec, vec_spec, vec_spec, ws_spec, bs_spec, pl.BlockSpec((BLK, gd), lambda g, n: (n, g))],
        out_specs=[out_uv, out_uv, vec_spec, vec_spec, ws_spec, bs_spec],
        out_shape=[jax.ShapeDtypeStruct((s, w), BF16), jax.ShapeDtypeStruct((s, w), BF16),
                   jax.ShapeDtypeStruct((1, w), F32), jax.ShapeDtypeStruct((1, w), F32),
                   jax.ShapeDtypeStruct((GM_GROUPS, BLK, BLK), F32), jax.ShapeDtypeStruct((GM_GROUPS, BLK, 1), F32)],
        compiler_params=_cp("parallel", "arbitrary"),
    )(proj, proj, ln_g.reshape(1, w), ln_b.reshape(1, w), ws, bs.reshape(GM_GROUPS, BLK, 1), dya)


def _silu_conv_fwd(proj, conv_w, conv_b, *, col0, name):
    s = proj.shape[0]
    c = conv_w.shape[1]
    tc = _tile(c, (256, 128))
    off = col0 // tc

    def body(x_ref, w_ref, b_ref, o_ref):
        y = _conv_rows(x_ref[...], w_ref, b_ref, SSM_CONV)
        o_ref[...] = y * _sigmoid(y)

    col = lambda j: (0, j)
    return pl.pallas_call(
        body, name=name, grid=(c // tc,),
        in_specs=[pl.BlockSpec((s, tc), lambda j: (0, j + off)), pl.BlockSpec((SSM_CONV, tc), col), pl.BlockSpec((1, tc), col)],
        out_specs=pl.BlockSpec((s, tc), col),
        out_shape=jax.ShapeDtypeStruct((s, c), F32),
        compiler_params=_cp("parallel"),
    )(proj, conv_w, _row(conv_b))


def _silu_conv_bwd(proj, conv_w, conv_b, dact, *, col0, name):
    s = proj.shape[0]
    c = conv_w.shape[1]
    tc = _tile(c, (256, 128))
    off = col0 // tc

    def body(x_ref, w_ref, b_ref, d_ref, dx_ref, dw_ref, db_ref):
        x = x_ref[...]
        y = _conv_rows(x, w_ref, b_ref, SSM_CONV)
        sg = _sigmoid(y)
        dgc = d_ref[...] * sg * (1.0 + y * (1.0 - sg))
        dx_ref[...] = _conv_rows_bwd(x, dgc, w_ref, dw_ref, db_ref, SSM_CONV).astype(BF16)

    col = lambda j: (0, j)
    return pl.pallas_call(
        body, name=name, grid=(c // tc,),
        in_specs=[pl.BlockSpec((s, tc), lambda j: (0, j + off)), pl.BlockSpec((SSM_CONV, tc), col), pl.BlockSpec((1, tc), col),
                  pl.BlockSpec((s, tc), col)],
        out_specs=[pl.BlockSpec((s, tc), col), pl.BlockSpec((SSM_CONV, tc), col), pl.BlockSpec((1, tc), col)],
        out_shape=[jax.ShapeDtypeStruct((s, c), BF16), jax.ShapeDtypeStruct((SSM_CONV, c), F32),
                   jax.ShapeDtypeStruct((1, c), F32)],
        compiler_params=_cp("parallel"),
    )(proj, conv_w, _row(conv_b), dact)


def _head_select(heads):
    r = lax.broadcasted_iota(jnp.int32, (128, heads * SSM_HEAD_DIM), 0)
    c = lax.broadcasted_iota(jnp.int32, (128, heads * SSM_HEAD_DIM), 1) // SSM_HEAD_DIM
    return (r == c).astype(F32)


def _dt_fwd(dt_raw, dt_bias, *, heads, name):
    s = dt_raw.shape[0]
    d = heads * SSM_HEAD_DIM
    tr = _tile(s, (256, 128))

    def body(x_ref, b_ref, o_ref):
        pre = jnp.dot(x_ref[...] + b_ref[...], _head_select(heads), precision=HI, preferred_element_type=F32)
        o_ref[...] = jax.nn.softplus(pre)

    return pl.pallas_call(
        body, name=name, grid=(s // tr,),
        in_specs=[pl.BlockSpec((tr, 128), lambda i: (i, 0)), pl.BlockSpec((1, 128), lambda i: (0, 0))],
        out_specs=pl.BlockSpec((tr, d), lambda i: (i, 0)),
        out_shape=jax.ShapeDtypeStruct((s, d), F32),
        compiler_params=_cp("parallel"),
    )(dt_raw, dt_bias)


def _dt_bwd(dt_raw, dt_bias, zt, da_lane, dd_lane, a_row, *, heads, name):
    s = dt_raw.shape[0]
    d = heads * SSM_HEAD_DIM
    tr = _tile(s, (256, 128))
    nt = (((1,), (1,)), ((), ()))

    def body(x_ref, b_ref, z_ref, da_ref, dd_ref, a_ref, o_ref, db_ref, dal_ref, dds_ref):
        sel = _head_select(heads)
        ddt = lax.dot_general(z_ref[...], sel, nt, precision=HI, preferred_element_type=F32)
        g = ddt * _sigmoid(x_ref[...] + b_ref[...])
        o_ref[...] = g.astype(BF16)

        @pl.when(pl.program_id(0) == 0)
        def _():
            db_ref[...] = jnp.zeros_like(db_ref)
            da = lax.dot_general(da_ref[...], sel, nt, precision=HI, preferred_element_type=F32)
            dal_ref[...] = da * a_ref[...]
            dds_ref[...] = lax.dot_general(dd_ref[...], sel, nt, precision=HI, preferred_element_type=F32)

        db_ref[...] += jnp.sum(g, axis=0, keepdims=True)

    vec = pl.BlockSpec((1, 128), lambda i: (0, 0))
    lane = pl.BlockSpec((1, d), lambda i: (0, 0))
    return pl.pallas_call(
        body, name=name, grid=(s // tr,),
        in_specs=[pl.BlockSpec((tr, 128), lambda i: (i, 0)), vec, pl.BlockSpec((tr, d), lambda i: (i, 0)), lane, lane, vec],
        out_specs=[pl.BlockSpec((tr, 128), lambda i: (i, 0)), vec, vec, vec],
        out_shape=[jax.ShapeDtypeStruct((s, 128), BF16)] + [jax.ShapeDtypeStruct((1, 128), F32)] * 3,
        compiler_params=_cp("arbitrary"),
    )(dt_raw, dt_bias, zt, da_lane, dd_lane, a_row)


_NT = (((1,), (1,)), ((), ()))
_TN = (((0,), (0,)), ((), ()))


def _bdot(a, b, dn=None):
    if dn is None:
        return jnp.dot(a, b, preferred_element_type=F32)
    return lax.dot_general(a, b, dn, preferred_element_type=F32)


def _ssd_common(x_ref, b_ref, c_ref, dt_ref, a_ref):
    x = x_ref[...]
    dt = dt_ref[...]
    rows = lax.broadcasted_iota(jnp.int32, (BLK, BLK), 0)
    cols = lax.broadcasted_iota(jnp.int32, (BLK, BLK), 1)
    tl = (rows >= cols).astype(F32)
    acum = jnp.dot(tl, dt * a_ref[...], precision=HI, preferred_element_type=F32)
    alast = acum[BLK - 1:BLK, :]
    bm = b_ref[...].astype(BF16)
    cm = c_ref[...].astype(BF16)
    cb = _bdot(cm, bm, _NT)
    return x, dt, rows, cols, acum, alast, bm, cm, cb


def _ssd_decay(ap, apt, e, low):
    acol = ap[:, e * SSM_HEAD_DIM:e * SSM_HEAD_DIM + 1]
    arow = apt[e * SSM_HEAD_DIM:e * SSM_HEAD_DIM + 1, :]
    return jnp.where(low, jnp.exp(jnp.minimum(acol - arow, 0.0)), 0.0)


def _ssd_specs(s, d):
    gw = d // SSM_GROUPS
    bcol = d // SSM_STATE
    return gw, bcol


def _ssd_fwd(act, dte, a_lane, d_lane, *, name):
    s = act.shape[0]
    d = dte.shape[1]
    gw, bcol = _ssd_specs(s, d)
    npair = gw // 128
    nc = s // BLK

    def body(x_ref, b_ref, c_ref, dt_ref, a_ref, dsk_ref, y_ref, st_ref, ht):
        @pl.when(pl.program_id(1) == 0)
        def _():
            ht[...] = jnp.zeros_like(ht)

        x, dt, rows, cols, acum, alast, bm, cm, cb = _ssd_common(x_ref, b_ref, c_ref, dt_ref, a_ref)
        low = rows >= cols
        first = cols < SSM_HEAD_DIM
        xd = x * dt
        h_in = ht[...]
        st_ref[0] = h_in
        yoff = _bdot(cm, h_in.astype(BF16)) * jnp.exp(acum)
        parts = []
        for p in range(npair):
            ap = acum[:, p * 128:(p + 1) * 128]
            apt = ap.T
            xdp = xd[:, p * 128:(p + 1) * 128].astype(BF16)
            ys = [_bdot((cb * _ssd_decay(ap, apt, e, low)).astype(BF16), xdp) for e in range(2)]
            parts.append(jnp.where(first, ys[0], ys[1]))
        ydiag = parts[0] if npair == 1 else jnp.concatenate(parts, axis=1)
        y_ref[...] = ydiag + yoff + dsk_ref[...] * x
        w = (xd * jnp.exp(alast - acum)).astype(BF16)
        ht[...] = h_in * jnp.exp(alast) + _bdot(bm, w, _TN)

    blk = lambda g, c: (c, g)
    vec = pl.BlockSpec((1, gw), lambda g, c: (0, g))
    return pl.pallas_call(
        body, name=name, grid=(SSM_GROUPS, nc),
        in_specs=[pl.BlockSpec((BLK, gw), blk),
                  pl.BlockSpec((BLK, SSM_STATE), lambda g, c: (c, bcol + g)),
                  pl.BlockSpec((BLK, SSM_STATE), lambda g, c: (c, bcol + SSM_GROUPS + g)),
                  pl.BlockSpec((BLK, gw), blk), vec, vec],
        out_specs=[pl.BlockSpec((BLK, gw), blk), pl.BlockSpec((1, SSM_STATE, gw), lambda g, c: (c, 0, g))],
        out_shape=[jax.ShapeDtypeStruct((s, d), F32), jax.ShapeDtypeStruct((nc, SSM_STATE, d), F32)],
        scratch_shapes=[pltpu.VMEM((SSM_STATE, gw), F32)],
        compiler_params=_cp("parallel", "arbitrary"),
    )(act, act, act, dte, a_lane, d_lane)


def _ssd_bwd(act, dte, a_lane, d_lane, states, dy, *, name):
    s = act.shape[0]
    d = dte.shape[1]
    gw, bcol = _ssd_specs(s, d)
    npair = gw // 128
    nc = s // BLK
    gn = SSM_GROUPS * SSM_STATE

    def body(x_ref, b_ref, c_ref, dt_ref, a_ref, dsk_ref, st_ref, dy_ref,
             dx_ref, db_ref, dc_ref, zt_ref, dal_ref, ddl_ref, dht):
        @pl.when(pl.program_id(1) == 0)
        def _():
            dht[...] = jnp.zeros_like(dht)
            dal_ref[...] = jnp.zeros_like(dal_ref)
            ddl_ref[...] = jnp.zeros_like(ddl_ref)

        x, dt, rows, cols, acum, alast, bm, cm, cb = _ssd_common(x_ref, b_ref, c_ref, dt_ref, a_ref)
        low = rows >= cols
        first = cols < SSM_HEAD_DIM
        a = a_ref[...]
        xd = x * dt
        ea = jnp.exp(acum)
        wdec = jnp.exp(alast - acum)
        el = jnp.exp(alast)
        h_in = st_ref[0]
        hb = h_in.astype(BF16)
        g = dy_ref[...]
        dh = dht[...]
        dhb = dh.astype(BF16)

        yoff = _bdot(cm, hb) * ea
        geb = (g * ea).astype(BF16)
        dc = _bdot(geb, hb, _NT)
        u = _bdot(bm, dhb)
        wx = xd * wdec
        db = _bdot(wx.astype(BF16), dhb, _NT)
        dxd = wdec * u
        xwu = wx * u
        da_l = g * yoff - xwu
        dalast = jnp.sum(xwu, axis=0, keepdims=True) + el * jnp.sum(dh * h_in, axis=0, keepdims=True)
        dht[...] = dh * el + _bdot(cm, geb, _TN)

        dcb = jnp.zeros((BLK, BLK), F32)
        dxd_parts, col_parts = [], []
        for p in range(npair):
            ap = acum[:, p * 128:(p + 1) * 128]
            apt = ap.T
            xdp = xd[:, p * 128:(p + 1) * 128].astype(BF16)
            gp = g[:, p * 128:(p + 1) * 128]
            dxp = jnp.zeros((BLK, 128), F32)
            colsum = []
            for e in range(2):
                dec = _ssd_decay(ap, apt, e, low)
                m = cb * dec
                gpm = jnp.where(first if e == 0 else jnp.logical_not(first), gp, 0.0).astype(BF16)
                dm = _bdot(gpm, xdp, _NT)
                q = dm * m
                colsum.append(jnp.sum(q, axis=1, keepdims=True) - jnp.sum(q.T, axis=1, keepdims=True))
                dcb = dcb + dm * dec
                dxp = dxp + _bdot(m.astype(BF16), gpm, _TN)
            dxd_parts.append(dxp)
            col_parts.append(jnp.where(first, colsum[0], colsum[1]) * (1.0 / SSM_HEAD_DIM))
        cat = (lambda ps: ps[0] if npair == 1 else jnp.concatenate(ps, axis=1))
        dxd = dxd + cat(dxd_parts)
        da_l = da_l + cat(col_parts)
        rows_w = lax.broadcasted_iota(jnp.int32, (BLK, gw), 0)
        da_l = da_l + jnp.where(rows_w == BLK - 1, dalast, 0.0)
        dcbb = dcb.astype(BF16)
        dc_ref[...] = dc + _bdot(dcbb, bm)
        db_ref[...] = db + _bdot(dcbb, cm, _TN)
        tu = (rows <= cols).astype(F32)
        dda = jnp.dot(tu, da_l, precision=HI, preferred_element_type=F32)
        zt_ref[...] = dxd * x + dda * a
        dal_ref[...] += jnp.sum(dda * dt, axis=0, keepdims=True)
        ddl_ref[...] += jnp.sum(g * x, axis=0, keepdims=True)
        dx_ref[...] = dsk_ref[...] * g + dxd * dt

    blk = lambda g, c: (nc - 1 - c, g)
    vec = pl.BlockSpec((1, gw), lambda g, c: (0, g))
    bc_out = pl.BlockSpec((BLK, SSM_STATE), blk)
    return pl.pallas_call(
        body, name=name, grid=(SSM_GROUPS, nc),
        in_specs=[pl.BlockSpec((BLK, gw), blk),
                  pl.BlockSpec((BLK, SSM_STATE), lambda g, c: (nc - 1 - c, bcol + g)),
                  pl.BlockSpec((BLK, SSM_STATE), lambda g, c: (nc - 1 - c, bcol + SSM_GROUPS + g)),
                  pl.BlockSpec((BLK, gw), blk), vec, vec,
                  pl.BlockSpec((1, SSM_STATE, gw), lambda g, c: (nc - 1 - c, 0, g)),
                  pl.BlockSpec((BLK, gw), blk)],
        out_specs=[pl.BlockSpec((BLK, gw), blk), bc_out, bc_out, pl.BlockSpec((BLK, gw), blk), vec, vec],
        out_shape=[jax.ShapeDtypeStruct((s, d), F32), jax.ShapeDtypeStruct((s, gn), F32),
                   jax.ShapeDtypeStruct((s, gn), F32), jax.ShapeDtypeStruct((s, d), F32),
                   jax.ShapeDtypeStruct((1, d), F32), jax.ShapeDtypeStruct((1, d), F32)],
        scratch_shapes=[pltpu.VMEM((SSM_STATE, gw), F32)],
        compiler_params=_cp("parallel", "arbitrary"),
    )(act, act, act, dte, a_lane, d_lane, states, dy)


def _gnorm_fwd(y, proj, norm_w, *, zcol, name):
    s, d = y.shape
    tr = _tile(s, (256, 128))
    gw = d // SSM_GROUPS

    def body(y_ref, z_ref, w_ref, o_ref):
        z = z_ref[...]
        y2 = y_ref[...] * (z * _sigmoid(z))
        for g in range(SSM_GROUPS):
            sl = slice(g * gw, (g + 1) * gw)
            v = y2[:, sl]
            r = lax.rsqrt(jnp.mean(v * v, axis=-1, keepdims=True) + EPS)
            o_ref[:, sl] = (v * r * w_ref[:, sl]).astype(BF16)

    return pl.pallas_call(
        body, name=name, grid=(s // tr,),
        in_specs=[pl.BlockSpec((tr, d), lambda i: (i, 0)), pl.BlockSpec((tr, d), lambda i: (i, zcol)),
                  pl.BlockSpec((1, d), lambda i: (0, 0))],
        out_specs=pl.BlockSpec((tr, d), lambda i: (i, 0)),
        out_shape=jax.ShapeDtypeStruct((s, d), BF16),
        compiler_params=_cp("parallel"),
    )(y, proj, _row(norm_w))


def _gnorm_bwd(y, proj, norm_w, dout, *, zcol, dcol, name):
    s, d = y.shape
    tr = _tile(s, (256, 128))
    gw = d // SSM_GROUPS

    def body(y_ref, z_ref, w_ref, do_ref, dy_ref, dz_ref, dw_ref):
        @pl.when(pl.program_id(0) == 0)
        def _():
            dw_ref[...] = jnp.zeros_like(dw_ref)

        z = z_ref[...]
        yv = y_ref[...]
        sg = _sigmoid(z)
        sz = z * sg
        y2 = yv * sz
        for g in range(SSM_GROUPS):
            sl = slice(g * gw, (g + 1) * gw)
            v = y2[:, sl]
            do = do_ref[:, sl]
            r = lax.rsqrt(jnp.mean(v * v, axis=-1, keepdims=True) + EPS)
            xh = v * r
            gg = do * w_ref[:, sl]
            dy2 = r * (gg - xh * jnp.mean(gg * xh, axis=-1, keepdims=True))
            dw_ref[:, sl] += jnp.sum(do * xh, axis=0, keepdims=True)
            dy_ref[:, sl] = dy2 * sz[:, sl]
            dz_ref[:, sl] = (dy2 * yv[:, sl] * (sg[:, sl] * (1.0 + z[:, sl] * (1.0 - sg[:, sl])))).astype(BF16)

    return pl.pallas_call(
        body, name=name, grid=(s // tr,),
        in_specs=[pl.BlockSpec((tr, d), lambda i: (i, 0)), pl.BlockSpec((tr, d), lambda i: (i, zcol)),
                  pl.BlockSpec((1, d), lambda i: (0, 0)), pl.BlockSpec((tr, d), lambda i: (i, dcol))],
        out_specs=[pl.BlockSpec((tr, d), lambda i: (i, 0)), pl.BlockSpec((tr, d), lambda i: (i, 0)),
                   pl.BlockSpec((1, d), lambda i: (0, 0))],
        out_shape=[jax.ShapeDtypeStruct((s, d), F32), jax.ShapeDtypeStruct((s, d), BF16),
                   jax.ShapeDtypeStruct((1, d), F32)],
        compiler_params=_cp("arbitrary"),
    )(y, proj, _row(norm_w), dout)


def _lanes(v):
    return jnp.repeat(v.astype(F32), SSM_HEAD_DIM).reshape(1, -1)


def _pad128(v):
    return jnp.pad(v.astype(F32).reshape(1, -1), ((0, 0), (0, 128 - v.shape[-1])))


def _even_fwd(h, p, tag):
    d = h.shape[1]
    heads = d // SSM_HEAD_DIM
    hn = _rms_fwd(h, p["norm_mix"], name=tag + "_rms")
    proj = _mm(hn, p["w_in_main"], name=tag + "_in")
    pdt = _mm(hn, p["w_in_dt"], name=tag + "_indt")
    ya = _gm_fwd(proj, p["gm_ln_g"], p["gm_ln_b"], p["gm_ws"], p["gm_bs"], name=tag + "_gm")
    act = _silu_conv_fwd(proj, p["conv_w"], p["conv_b"], col0=3 * d, name=tag + "_conv")
    dte = _dt_fwd(pdt, _pad128(p["dt_bias"]), heads=heads, name=tag + "_dt")
    a = -jnp.exp(p["a_log"].astype(F32))
    y, states = _ssd_fwd(act, dte, _lanes(a), _lanes(p["d_skip"]), name=tag + "_ssd")
    yb = _gnorm_fwd(y, proj, p["ssm_norm_w"], zcol=2, name=tag + "_gn")
    cat = jnp.concatenate([ya, yb], axis=1)
    h1 = _mm(cat, p["w_out"], residual=h, name=tag + "_out")
    return h1, (h, hn, proj, pdt, act, dte, y, states, cat)


def _even_bwd(dh1, p, saved, tag, out_layer=None):
    h, hn, proj, pdt, act, dte, y, states, cat = saved
    d = h.shape[1]
    heads = d // SSM_HEAD_DIM
    a = -jnp.exp(p["a_log"].astype(F32))
    g = {}
    dcat = _mm(dh1, p["w_out"], tb=True, name=tag + "_dcat")
    g["w_out"] = _mm(cat, dh1, ta=True, out_layer=out_layer, name=tag + "_dwout")
    du, dv, dlg, dlb, dws, dbs = _gm_bwd(proj, p["gm_ln_g"], p["gm_ln_b"], p["gm_ws"], p["gm_bs"], dcat, name=tag + "_gmb")
    g["gm_ln_g"] = dlg.reshape(GM_GROUPS, -1)
    g["gm_ln_b"] = dlb.reshape(GM_GROUPS, -1)
    g["gm_ws"] = dws
    g["gm_bs"] = dbs.reshape(GM_GROUPS, BLK)
    dy, dz, dnw = _gnorm_bwd(y, proj, p["ssm_norm_w"], dcat, zcol=2, dcol=1, name=tag + "_gnb")
    g["ssm_norm_w"] = dnw[0]
    dxs, db, dc, zt, dal, ddl = _ssd_bwd(act, dte, _lanes(a), _lanes(p["d_skip"]), states, dy, name=tag + "_ssdb")
    ddt, ddtb, dalog, ddsk = _dt_bwd(pdt, _pad128(p["dt_bias"]), zt, dal, ddl, _pad128(a), heads=heads, name=tag + "_dtb")
    g["dt_bias"] = ddtb[0, :heads]
    g["a_log"] = dalog[0, :heads]
    g["d_skip"] = ddsk[0, :heads]
    dact = jnp.concatenate([dxs, db, dc], axis=1)
    dxbc, dcw, dcb = _silu_conv_bwd(proj, p["conv_w"], p["conv_b"], dact, col0=3 * d, name=tag + "_convb")
    g["conv_w"] = dcw
    g["conv_b"] = dcb[0]
    dproj = jnp.concatenate([du, dv, dz, dxbc], axis=1)
    dhn = _mm(dproj, p["w_in_main"], tb=True, name=tag + "_dhn")
    dhn = _mm(ddt, p["w_in_dt"], tb=True, residual=dhn, name=tag + "_dhn2")
    g["w_in_main"] = _mm(hn, dproj, ta=True, name=tag + "_dwin")
    g["w_in_dt"] = _mm(hn, ddt, ta=True, name=tag + "_dwdt")
    dh, dnm = _rms_bwd(h, p["norm_mix"], dhn, add=dh1, name=tag + "_rmsb")
    g["norm_mix"] = dnm[0]
    return dh, g


def _w_up(p):
    if "w_up_slabs" in p:
        return p["w_up_slabs"][0], {"b_layer": p["w_up_slabs"][1]}
    return p["w_up"], {}


def _ffn_fwd(h, p, tag):
    hn = _rms_fwd(h, p["norm_ffn"], name=tag + "_rms")
    w_up, kw = _w_up(p)
    up = _mm(hn, w_up, name=tag + "_up", **kw)
    a = _ffn_mid_fwd(up, p["conv_w"], p["conv_b"], name=tag + "_mid")
    h2 = _mm(a, p["w_down"], residual=h, name=tag + "_down")
    return h2, (h, hn, up, a)


def _ffn_bwd(dh2, p, saved, tag, g_up_prev=None, out_layer=None):
    h, hn, up, a = saved
    g = {}
    da = _mm(dh2, p["w_down"], tb=True, name=tag + "_da")
    g["w_down"] = _mm(a, dh2, ta=True, out_layer=out_layer, name=tag + "_dwdown")
    dg, dv, dcw, dcb = _ffn_mid_bwd(up, p["conv_w"], p["conv_b"], da, name=tag + "_midb")
    g["conv_w"] = dcw
    g["conv_b"] = dcb[0]
    dup = jnp.concatenate([dg, dv], axis=1)
    w_up, kw = _w_up(p)
    dhn = _mm(dup, w_up, tb=True, name=tag + "_dhn", **kw)
    if kw:
        g["w_up_slabs"] = _mm(hn, dup, ta=True, out_slabs=(g_up_prev, kw["b_layer"], w_up.shape), name=tag + "_dwup")
    else:
        g["w_up"] = _mm(hn, dup, ta=True, name=tag + "_dwup")
    dh, dnw = _rms_bwd(h, p["norm_ffn"], dhn, add=dh2, name=tag + "_rmsb")
    g["norm_ffn"] = dnw[0]
    return dh, g


def _rope(x, cos_p, sin_p):
    half = MLA_ROPE // 2
    lane = lax.broadcasted_iota(jnp.int32, x.shape, 1)
    swapped = jnp.where(lane < half, pltpu.roll(x, 128 - half, 1), pltpu.roll(x, half, 1))
    return x * cos_p + swapped * sin_p


def _rope_t(g, cos_p, sin_p):
    half = MLA_ROPE // 2
    gs = g * sin_p
    lane = lax.broadcasted_iota(jnp.int32, g.shape, 1)
    swapped = jnp.where(lane < half, pltpu.roll(gs, 128 - half, 1), pltpu.roll(gs, half, 1))
    return g * cos_p + swapped


ATTN_SEGMENTS = 4


def _attn_probs(qn_ref, qp_ref, kn_ref, kp_ref, cq_ref, sq_ref, ck_ref, sk_ref, tq, qb0):
    s = kn_ref.shape[0]
    scale = (MLA_NOPE + MLA_ROPE) ** -0.5
    qn = qn_ref[...].astype(BF16)
    qp = _rope(qp_ref[...], cq_ref[...], sq_ref[...]).astype(BF16)
    kn = kn_ref[...].astype(BF16)
    kp = _rope(kp_ref[...], ck_ref[...], sk_ref[...]).astype(BF16)
    sc = (_bdot(qn, kn, _NT) + _bdot(qp, kp, _NT)) * scale
    qpos = (pl.program_id(1) + qb0) * tq + lax.broadcasted_iota(jnp.int32, (tq, s), 0)
    kpos = lax.broadcasted_iota(jnp.int32, (tq, s), 1)
    sc = jnp.where(kpos // CHUNK <= qpos // CHUNK, sc, -jnp.inf)
    sc = sc - jnp.max(sc, axis=-1, keepdims=True)
    e = jnp.exp(sc)
    p = e / jnp.sum(e, axis=-1, keepdims=True)
    return p, qn, qp, kn, kp, scale


def _attn_segments(s, tq):
    nq = s // tq
    nseg = math.gcd(ATTN_SEGMENTS, nq)
    per = nq // nseg
    return [(seg * per, per, (seg + 1) * per * tq) for seg in range(nseg)]


def _attn_in_specs(s, tq, kr_col, qb0):
    return [pl.BlockSpec((tq, 128), lambda h, i: (i + qb0, 2 * h)), pl.BlockSpec((tq, 128), lambda h, i: (i + qb0, 2 * h + 1)),
            pl.BlockSpec((s, 128), lambda h, i: (0, 2 * h)), pl.BlockSpec((s, 128), lambda h, i: (0, kr_col)),
            pl.BlockSpec((tq, 128), lambda h, i: (i + qb0, 0)), pl.BlockSpec((tq, 128), lambda h, i: (i + qb0, 0)),
            pl.BlockSpec((s, 128), lambda h, i: (0, 0)), pl.BlockSpec((s, 128), lambda h, i: (0, 0)),
            pl.BlockSpec((s, 128), lambda h, i: (0, 2 * h + 1))]


def _attn_fwd(q, kv, proj, cos_p, sin_p, *, kr_col, name):
    s = q.shape[0]
    heads = q.shape[1] // 256
    tq = _tile(s, (256, 128))
    o = None
    for seg, (qb0, nqb, keys) in enumerate(_attn_segments(s, tq)):
        def body(qn_ref, qp_ref, kn_ref, kp_ref, cq_ref, sq_ref, ck_ref, sk_ref, v_ref, *rest, qb0=qb0):
            o_ref = rest[-1]
            p = _attn_probs(qn_ref, qp_ref, kn_ref, kp_ref, cq_ref, sq_ref, ck_ref, sk_ref, tq, qb0)[0]
            o_ref[...] = _bdot(p.astype(BF16), v_ref[...].astype(BF16)).astype(BF16)

        in_specs = _attn_in_specs(keys, tq, kr_col, qb0)
        args = [q, q, kv, proj, cos_p, sin_p, cos_p, sin_p, kv]
        aliases = {}
        if o is not None:
            in_specs.append(_ANY)
            args.append(o)
            aliases = {len(args) - 1: 0}
        o = pl.pallas_call(
            body, name=f"{name}{seg}", grid=(heads, nqb),
            in_specs=in_specs,
            out_specs=pl.BlockSpec((tq, 128), lambda h, i, qb0=qb0: (i + qb0, h)),
            out_shape=jax.ShapeDtypeStruct((s, heads * MLA_V), BF16),
            input_output_aliases=aliases,
            compiler_params=_cp("parallel", "parallel"),
        )(*args)
    return o


def _attn_bwd(q, kv, proj, cos_p, sin_p, do, *, kr_col, name):
    s = q.shape[0]
    heads = q.shape[1] // 256
    tq = _tile(s, (256, 128))
    dq = dkv = dkp = None
    for seg, (qb0, nqb, keys) in reversed(list(enumerate(_attn_segments(s, tq)))):
        first = dq is None

        def body(qn_ref, qp_ref, kn_ref, kp_ref, cq_ref, sq_ref, ck_ref, sk_ref, v_ref, do_ref, *rest, qb0=qb0, first=first):
            dq_ref, dkv_ref, dkp_ref = rest[-3:]
            h = pl.program_id(0)
            i = pl.program_id(1)

            @pl.when(i == 0)
            def _():
                dkv_ref[...] = jnp.zeros_like(dkv_ref) if first else rest[1][...]

            @pl.when(jnp.logical_and(h == 0, i == 0))
            def _():
                dkp_ref[...] = jnp.zeros_like(dkp_ref) if first else rest[2][...]

            p, qn, qp, kn, kp, scale = _attn_probs(qn_ref, qp_ref, kn_ref, kp_ref, cq_ref, sq_ref, ck_ref, sk_ref, tq, qb0)
            dob = do_ref[...].astype(BF16)
            pb = p.astype(BF16)
            dv = _bdot(pb, dob, _TN)
            dp = _bdot(dob, v_ref[...].astype(BF16), _NT)
            ds = (p * (dp - jnp.sum(dp * p, axis=-1, keepdims=True)) * scale).astype(BF16)
            dq_ref[:, 0:128] = _bdot(ds, kn).astype(BF16)
            dq_ref[:, 128:256] = _rope_t(_bdot(ds, kp), cq_ref[...], sq_ref[...]).astype(BF16)
            dkv_ref[:, 0:128] += _bdot(ds, qn, _TN)
            dkv_ref[:, 128:256] += dv
            dkp_ref[...] += _rope_t(_bdot(ds, qp, _TN), ck_ref[...], sk_ref[...])

        dkv_spec = pl.BlockSpec((keys, 256), lambda h, i: (0, h))
        dkp_spec = pl.BlockSpec((keys, 128), lambda h, i: (0, 0))
        in_specs = _attn_in_specs(keys, tq, kr_col, qb0) + [pl.BlockSpec((tq, 128), lambda h, i, qb0=qb0: (i + qb0, h))]
        args = [q, q, kv, proj, cos_p, sin_p, cos_p, sin_p, kv, do]
        aliases = {}
        if not first:
            in_specs += [_ANY, dkv_spec, dkp_spec]
            args += [dq, dkv, dkp]
            aliases = {len(args) - 3: 0, len(args) - 2: 1, len(args) - 1: 2}
        dq, dkv, dkp = pl.pallas_call(
            body, name=f"{name}{seg}", grid=(heads, nqb),
            in_specs=in_specs,
            out_specs=[pl.BlockSpec((tq, 256), lambda h, i, qb0=qb0: (i + qb0, h)), dkv_spec, dkp_spec],
            out_shape=[jax.ShapeDtypeStruct((s, heads * 256), BF16), jax.ShapeDtypeStruct((s, heads * 256), F32),
                       jax.ShapeDtypeStruct((s, 128), F32)],
            input_output_aliases=aliases,
            compiler_params=_cp("arbitrary", "arbitrary"),
        )(*args)
    return dq, dkv, dkp


def _rope_tables(positions):
    inv_freq = ROPE_THETA ** (-jnp.arange(0, MLA_ROPE, 2, dtype=F32) / MLA_ROPE)
    ang = positions.astype(F32)[:, None] * inv_freq
    cos, sin = jnp.cos(ang), jnp.sin(ang)
    zero = jnp.zeros((positions.shape[0], 128 - MLA_ROPE), F32)
    return jnp.concatenate([cos, cos, zero], axis=1), jnp.concatenate([-sin, sin, zero], axis=1)


def _odd_fwd(h, p, cos_p, sin_p, tag):
    rank = p["q_norm"].shape[0]
    hn = _rms_fwd(h, p["norm_mix"], name=tag + "_rms")
    proj = _mm(hn, p["w_in"], name=tag + "_in")
    cqn = _rms_fwd(proj, p["q_norm"], width=rank, col=0, name=tag + "_qn")
    ckvn = _rms_fwd(proj, p["kv_norm"], width=rank, col=1, name=tag + "_kvn")
    q = _mm(cqn, p["w_uq"], name=tag + "_uq")
    kv = _mm(ckvn, p["w_ukv"], name=tag + "_ukv")
    o = _attn_fwd(q, kv, proj, cos_p, sin_p, kr_col=2 * rank // 128, name=tag + "_attn")
    h1 = _mm(o, p["w_o"], residual=h, name=tag + "_o")
    return h1, (h, hn, proj, cqn, ckvn, q, kv, o)


def _odd_bwd(dh1, p, cos_p, sin_p, saved, tag, out_layer=None):
    h, hn, proj, cqn, ckvn, q, kv, o = saved
    rank = p["q_norm"].shape[0]
    g = {}
    do = _mm(dh1, p["w_o"], tb=True, name=tag + "_do")
    g["w_o"] = _mm(o, dh1, ta=True, out_layer=out_layer, name=tag + "_dwo")
    dq, dkv, dkp = _attn_bwd(q, kv, proj, cos_p, sin_p, do, kr_col=2 * rank // 128, name=tag + "_attnb")
    dcqn = _mm(dq, p["w_uq"], tb=True, name=tag + "_dcqn")
    g["w_uq"] = _mm(cqn, dq, ta=True, name=tag + "_dwuq")
    dckvn = _mm(dkv, p["w_ukv"], tb=True, name=tag + "_dckvn")
    g["w_ukv"] = _mm(ckvn, dkv, ta=True, name=tag + "_dwukv")
    dcq, dqn = _rms_bwd(proj, p["q_norm"], dcqn, width=rank, col=0, out_dtype=BF16, name=tag + "_qnb")
    dckv, dkvn = _rms_bwd(proj, p["kv_norm"], dckvn, width=rank, col=1, out_dtype=BF16, name=tag + "_kvnb")
    g["q_norm"] = dqn[0]
    g["kv_norm"] = dkvn[0]
    dproj = jnp.concatenate([dcq, dckv, dkp.astype(BF16)], axis=1)
    dhn = _mm(dproj, p["w_in"], tb=True, name=tag + "_dhn")
    g["w_in"] = _mm(hn, dproj, ta=True, name=tag + "_dwin")
    dh, dnm = _rms_bwd(h, p["norm_mix"], dhn, add=dh1, name=tag + "_rmsb")
    g["norm_mix"] = dnm[0]
    return dh, g


PACK_W = 1024
N_CHIPS = 4
_MESH = pl.DeviceIdType.MESH
_ANY = pl.BlockSpec(memory_space=pl.ANY)


def _place():
    x, y, c = lax.axis_index("x"), lax.axis_index("y"), lax.axis_index("c")
    others = [(1 - x, y), (x, 1 - y), (1 - x, 1 - y)]
    return x, y, c, others


def _row_tile(r, c, itemsize):
    for cand in (512, 256, 128, 64, 32, 16, 8):
        if r % cand == 0 and cand * c * itemsize <= 2 * 1024 * 1024:
            return cand
    return r


def _slot_index(slot_axis, slot, layer, i):
    return (slot, layer, i, 0) if slot_axis == 0 else (layer, slot, i, 0)


def _cast_place(w, ck, *, slot_axis, dtype, name):
    nl, r, c = w.shape
    tr = _row_tile(r, c, 4)
    shape = (N_CHIPS, nl, r, c) if slot_axis == 0 else (nl, N_CHIPS, r, c)

    def body(ck_ref, w_ref, o_ref):
        o_ref[...] = w_ref[...].astype(dtype)

    return pl.pallas_call(
        body, name=name,
        grid_spec=pltpu.PrefetchScalarGridSpec(
            num_scalar_prefetch=1, grid=(nl, r // tr),
            in_specs=[pl.BlockSpec((None, tr, c), lambda l, i, s: (l, i, 0))],
            out_specs=pl.BlockSpec((None, None, tr, c), lambda l, i, s: _slot_index(slot_axis, s[1], l, i))),
        out_shape=jax.ShapeDtypeStruct(shape, dtype),
        compiler_params=_cp("parallel", "parallel"),
    )(ck, w)


def _region(ref, slot_axis, slot, half):
    lh = ref.shape[1 - slot_axis] // 2
    if slot_axis == 0:
        return ref.at[slot, pl.ds(half * lh, lh)]
    return ref.at[pl.ds(half * lh, lh), slot]


def _all_gather_multi(bufs, slot_axes, *, name):
    n = len(bufs)

    def body(*refs):
        outs = refs[n:2 * n]
        send_sems, recv_sems = refs[2 * n], refs[2 * n + 1]
        x, y, c, others = _place()
        k = 2 * x + y
        sibling = (x, y, 1 - c)

        def copy(a, slot, half, sem, to):
            blk = _region(outs[a], slot_axes[a], slot, half)
            return pltpu.make_async_remote_copy(src_ref=blk, dst_ref=blk, send_sem=send_sems.at[6 * a + sem],
                                                recv_sem=recv_sems.at[6 * a + sem], device_id=to, device_id_type=_MESH)

        first = [copy(a, k, c, j, (cx, cy, c)) for a in range(n) for j, (cx, cy) in enumerate(others)]
        for cp in first:
            cp.start()
        passed = []
        for a in range(n):
            for j, (cx, cy) in enumerate(others):
                copy(a, 2 * cx + cy, c, j, (cx, cy, c)).wait_recv()
                fw = copy(a, 2 * cx + cy, c, 3 + j, sibling)
                fw.start()
                passed.append(fw)
        for a in range(n):
            for j, (cx, cy) in enumerate(others):
                copy(a, 2 * cx + cy, 1 - c, 3 + j, sibling).wait_recv()
        for cp in first + passed:
            cp.wait_send()

    return pl.pallas_call(
        body, name=name,
        in_specs=[_ANY] * n, out_specs=[_ANY] * n,
        out_shape=[jax.ShapeDtypeStruct(b.shape, b.dtype) for b in bufs],
        input_output_aliases={a: a for a in range(n)},
        scratch_shapes=[pltpu.SemaphoreType.DMA((6 * n,)), pltpu.SemaphoreType.DMA((6 * n,))],
    )(*bufs)


def _half_shape(shape, slot_axis):
    shape = list(shape)
    shape[1 - slot_axis] //= 2
    return tuple(shape)


def _pair_exchange_multi(gs, slot_axes, *, name):
    n = len(gs)

    def body(*refs):
        g_refs, a_refs = refs[:n], refs[n:2 * n]
        send_sems, recv_sems = refs[2 * n], refs[2 * n + 1]
        x, y, c, _ = _place()
        copies = []
        for a in range(n):
            lh = a_refs[a].shape[1 - slot_axes[a]]
            src = g_refs[a].at[:, pl.ds((1 - c) * lh, lh)] if slot_axes[a] == 0 else g_refs[a].at[pl.ds((1 - c) * lh, lh)]
            copies.append(pltpu.make_async_remote_copy(
                src_ref=src, dst_ref=a_refs[a], send_sem=send_sems.at[a], recv_sem=recv_sems.at[a],
                device_id=(x, y, 1 - c), device_id_type=_MESH))
        for cp in copies:
            cp.start()
        for cp in copies:
            cp.wait()

    return pl.pallas_call(
        body, name=name, in_specs=[_ANY] * n, out_specs=[_ANY] * n,
        out_shape=[jax.ShapeDtypeStruct(_half_shape(g.shape, ax), g.dtype) for g, ax in zip(gs, slot_axes)],
        scratch_shapes=[pltpu.SemaphoreType.DMA((n,)), pltpu.SemaphoreType.DMA((n,))],
    )(*gs)


def _pair_add(g, a, ck, *, slot_axis, name):
    lh = a.shape[1 - slot_axis]
    r, c = a.shape[2:]
    tr = _row_tile(r, c, 4)

    def body(ck_ref, g_ref, a_ref, t_ref, own_ref):
        v = g_ref[...] + a_ref[...]
        t_ref[...] = v.astype(BF16)

        @pl.when(pl.program_id(2) == ck_ref[1])
        def _():
            own_ref[...] = v

    blk = (None, None, tr, c)
    return pl.pallas_call(
        body, name=name,
        grid_spec=pltpu.PrefetchScalarGridSpec(
            num_scalar_prefetch=1, grid=(lh, r // tr, N_CHIPS),
            in_specs=[pl.BlockSpec(blk, lambda l, i, j, s: _slot_index(slot_axis, j, s[0] * lh + l, i)),
                      pl.BlockSpec(blk, lambda l, i, j, s: _slot_index(slot_axis, j, l, i))],
            out_specs=[pl.BlockSpec(blk, lambda l, i, j, s: _slot_index(slot_axis, j, l, i)),
                       pl.BlockSpec((None, tr, c), lambda l, i, j, s: (l, i, 0))]),
        out_shape=[jax.ShapeDtypeStruct(a.shape, BF16), jax.ShapeDtypeStruct((lh, r, c), F32)],
        compiler_params=_cp("arbitrary", "arbitrary", "arbitrary"),
    )(ck, g, a)


def _chip_exchange_multi(ts, slot_axes, *, name):
    n = len(ts)

    def body(*refs):
        t_refs, b_refs = refs[:n], refs[n:2 * n]
        send_sems, recv_sems = refs[2 * n], refs[2 * n + 1]
        x, y, c, others = _place()
        copies = []
        for a in range(n):
            for j, (cx, cy) in enumerate(others):
                src = t_refs[a].at[2 * cx + cy] if slot_axes[a] == 0 else t_refs[a].at[:, 2 * cx + cy]
                copies.append(pltpu.make_async_remote_copy(
                    src_ref=src, dst_ref=b_refs[a].at[j], send_sem=send_sems.at[3 * a + j], recv_sem=recv_sems.at[3 * a + j],
                    device_id=(cx, cy, c), device_id_type=_MESH))
        for cp in copies:
            cp.start()
        for cp in copies:
            cp.wait()

    def out_shape(t, ax):
        lh = t.shape[1 - ax]
        return jax.ShapeDtypeStruct((N_CHIPS - 1, lh) + t.shape[2:], t.dtype)

    return pl.pallas_call(
        body, name=name, in_specs=[_ANY] * n, out_specs=[_ANY] * n,
        out_shape=[out_shape(t, ax) for t, ax in zip(ts, slot_axes)],
        scratch_shapes=[pltpu.SemaphoreType.DMA((3 * n,)), pltpu.SemaphoreType.DMA((3 * n,))],
    )(*ts)


def _chip_add(own, b, ck, *, name):
    lh, r, c = own.shape
    tr = _row_tile(r, c, 4)

    def body(ck_ref, o_ref, b_ref, r_ref):
        acc = o_ref[...]
        for j in range(N_CHIPS - 1):
            acc = acc + b_ref[j].astype(F32)
        r_ref[...] = acc

    return pl.pallas_call(
        body, name=name,
        grid_spec=pltpu.PrefetchScalarGridSpec(
            num_scalar_prefetch=1, grid=(lh, r // tr),
            in_specs=[pl.BlockSpec((None, tr, c), lambda l, i, s: (l, i, 0)),
                      pl.BlockSpec((N_CHIPS - 1, None, tr, c), lambda l, i, s: (0, l, i, 0))],
            out_specs=pl.BlockSpec((None, tr, c), lambda l, i, s: (s[0] * lh + l, i, 0))),
        out_shape=jax.ShapeDtypeStruct((2 * lh, r, c), F32),
        compiler_params=_cp("parallel", "parallel"),
    )(ck, own, b)


def _pair_share_multi(finals, *, name):
    n = len(finals)

    def body(*refs):
        outs = refs[n:2 * n]
        send_sems, recv_sems = refs[2 * n], refs[2 * n + 1]
        x, y, c, _ = _place()

        def copy(a, half):
            lh = outs[a].shape[0] // 2
            blk = outs[a].at[pl.ds(half * lh, lh)]
            return pltpu.make_async_remote_copy(src_ref=blk, dst_ref=blk, send_sem=send_sems.at[a], recv_sem=recv_sems.at[a],
                                                device_id=(x, y, 1 - c), device_id_type=_MESH)

        sends = [copy(a, c) for a in range(n)]
        for cp in sends:
            cp.start()
        for a in range(n):
            copy(a, 1 - c).wait_recv()
        for cp in sends:
            cp.wait_send()

    return pl.pallas_call(
        body, name=name, in_specs=[_ANY] * n, out_specs=[_ANY] * n,
        out_shape=[jax.ShapeDtypeStruct(f.shape, f.dtype) for f in finals],
        input_output_aliases={a: a for a in range(n)},
        scratch_shapes=[pltpu.SemaphoreType.DMA((n,)), pltpu.SemaphoreType.DMA((n,))],
    )(*finals)


def _col_pieces(cs, segments):
    pieces = []
    for k in range(N_CHIPS):
        for gs, ge, oi, ds in segments:
            lo, hi = max(k * cs, gs), min((k + 1) * cs, ge)
            if lo < hi:
                pieces.append((k, lo - k * cs, oi, ds + lo - gs, hi - lo))
    return pieces


def _assemble(f, layer, pieces, widths, *, name):
    _, _, r, c = f.shape
    tr = _row_tile(r, max(max(widths), N_CHIPS * c), f.dtype.itemsize)
    covered = sum(p[4] for p in pieces) == sum(widths)

    def body(f_ref, *o_refs):
        if not covered:
            for o in o_refs:
                o[...] = jnp.zeros_like(o)
        for k, s0, oi, d0, wd in pieces:
            o_refs[oi][:, d0:d0 + wd] = f_ref[k, :, s0:s0 + wd]

    return pl.pallas_call(
        body, name=name, grid=(r // tr,),
        in_specs=[pl.BlockSpec((N_CHIPS, None, tr, c), lambda i: (0, layer, i, 0))],
        out_specs=[pl.BlockSpec((tr, w), lambda i: (i, 0)) for w in widths],
        out_shape=[jax.ShapeDtypeStruct((r, w), f.dtype) for w in widths],
        compiler_params=_cp("parallel"),
    )(f)


def _split(fulls, layer, pieces, g_prev, shape, *, name):
    _, _, r, c = shape
    widths = [t.shape[1] for t in fulls]
    tr = _row_tile(r, max(max(widths), N_CHIPS * c), 4)
    nf = len(fulls)

    def body(*refs):
        g_ref = refs[-1]
        for k, s0, oi, d0, wd in pieces:
            g_ref[k, :, s0:s0 + wd] = refs[oi][:, d0:d0 + wd]

    in_specs = [pl.BlockSpec((tr, w), lambda i: (i, 0)) for w in widths]
    args = list(fulls)
    aliases = {}
    if g_prev is not None:
        in_specs.append(_ANY)
        args.append(g_prev)
        aliases = {nf: 0}
    return pl.pallas_call(
        body, name=name, grid=(r // tr,),
        in_specs=in_specs,
        out_specs=pl.BlockSpec((N_CHIPS, None, tr, c), lambda i: (0, layer, i, 0)),
        out_shape=jax.ShapeDtypeStruct(shape, F32),
        input_output_aliases=aliases,
        compiler_params=_cp("parallel"),
    )(*args)


def _all_reduce_small(v, *, name):
    r, w = v.shape
    n_dev = 8

    def body(x_ref, sum_ref, out_ref, send_sems, recv_sems, local_sem):
        x, y, c, others = _place()
        me, sibling = (x, y, c), (x, y, 1 - c)

        def rows(px, py, pc):
            return out_ref.at[pl.ds((4 * px + 2 * py + pc) * r, r), :]

        def copy(k, block, to, src=None):
            return pltpu.make_async_remote_copy(
                src_ref=rows(*block) if src is None else src, dst_ref=rows(*block),
                send_sem=send_sems.at[k], recv_sem=recv_sems.at[k], device_id=to, device_id_type=_MESH)

        mine = pltpu.make_async_copy(x_ref, rows(*me), local_sem)
        mine.start()
        first = [copy(0, me, sibling, src=x_ref)]
        first += [copy(1 + j, me, (*chip, c), src=x_ref) for j, chip in enumerate(others)]
        for cp in first:
            cp.start()
        passed = [copy(4 + j, (*chip, c), sibling) for j, chip in enumerate(others)]
        for j, chip in enumerate(others):
            copy(1 + j, (*chip, c), me).wait_recv()
            passed[j].start()
        copy(0, sibling, me).wait_recv()
        for j, chip in enumerate(others):
            copy(4 + j, (*chip, 1 - c), me).wait_recv()
        for cp in first + passed:
            cp.wait_send()
        mine.wait()
        acc = out_ref[pl.ds(0, r), :]
        for dev in range(1, n_dev):
            acc = acc + out_ref[pl.ds(dev * r, r), :]
        sum_ref[...] = acc

    vmem = pl.BlockSpec(memory_space=pltpu.VMEM)
    return pl.pallas_call(
        body, name=name, in_specs=[vmem], out_specs=[vmem, vmem],
        out_shape=[jax.ShapeDtypeStruct((r, w), F32), jax.ShapeDtypeStruct((n_dev * r, w), F32)],
        scratch_shapes=[pltpu.SemaphoreType.DMA((7,)), pltpu.SemaphoreType.DMA((7,)), pltpu.SemaphoreType.DMA],
        compiler_params=pltpu.CompilerParams(vmem_limit_bytes=V7X_VMEM_LIMIT),
    )(v)[0]


def _adamw(w, g, m, v, *, name):
    shape = w.shape
    cols = shape[-1]
    rows = max(1, math.prod(shape[:-1]))
    tr = rows
    for cand in (512, 256, 128, 64, 32, 16, 8):
        if rows % cand == 0 and cand * cols * 4 <= 2 * 1024 * 1024:
            tr = cand
            break
    c1 = 1.0 - ADAM_B1 ** ADAM_STEP
    c2 = 1.0 - ADAM_B2 ** ADAM_STEP

    def body(w_ref, g_ref, m_ref, v_ref, d_ref, mo_ref, vo_ref):
        gv = g_ref[...]
        mn = ADAM_B1 * m_ref[...] + (1.0 - ADAM_B1) * gv
        vn = ADAM_B2 * v_ref[...] + (1.0 - ADAM_B2) * (gv * gv)
        d_ref[...] = -ADAM_LR * ((mn / c1) / (jnp.sqrt(vn / c2) + ADAM_EPS) + ADAM_WD * w_ref[...])
        mo_ref[...] = mn
        vo_ref[...] = vn

    spec = pl.BlockSpec((tr, cols), lambda i: (i, 0))
    outs = pl.pallas_call(
        body, name=name, grid=(rows // tr,),
        in_specs=[spec] * 4, out_specs=[spec] * 3,
        out_shape=[jax.ShapeDtypeStruct((rows, cols), F32)] * 3,
        compiler_params=_cp("parallel"),
    )(*[t.reshape(rows, cols) for t in (w, g, m, v)])
    return [o.reshape(shape) for o in outs]


_WEIGHTS = ["norm_mix", "norm_ffn", "norm_final", "ev_w_in", "ev_gm_ln_g", "ev_gm_ln_b", "ev_gm_ws", "ev_gm_bs",
            "ev_conv_w", "ev_conv_b", "ev_dt_bias", "ev_a_log", "ev_d_skip", "ev_ssm_norm_w", "ev_w_out", "od_w_in",
            "od_q_norm", "od_kv_norm", "od_w_uq", "od_w_ukv", "od_w_o", "ff_w_up", "ff_conv_w", "ff_conv_b", "ff_w_down"]
_BIG = {"ev_w_in": -1, "ev_w_out": -2, "od_w_in": -2, "od_w_uq": -1, "od_w_ukv": -1, "od_w_o": -2,
        "ff_w_up": -1, "ff_w_down": -2}
_SMALL = {"ev_gm_ln_g": -1, "ev_gm_ln_b": -1, "ev_conv_w": -1, "od_q_norm": -1, "od_kv_norm": -1, "ff_conv_w": -1}
_SHARDED = {**_BIG, **_SMALL}
_REPLICATED = [n for n in _WEIGHTS if n not in _SHARDED]
N_CHUNKS = 4


def _from_slabs(slabs, axis):
    t = jnp.moveaxis(slabs, 0, axis - 1)
    shape = list(t.shape)
    if axis == -1:
        return t.reshape(shape[:-2] + [shape[-2] * shape[-1]])
    return t.reshape(shape[:-3] + [shape[-3] * shape[-2], shape[-1]])


def _to_slabs(full, axis):
    shape = list(full.shape)
    if axis == -1:
        t = full.reshape(shape[:-1] + [N_CHIPS, shape[-1] // N_CHIPS])
    else:
        t = full.reshape(shape[:-2] + [N_CHIPS, shape[-2] // N_CHIPS, shape[-1]])
    return jnp.moveaxis(t, axis - 1, 0)


def _reduce_scatter(gs, slot_axes, names, ck):
    theirs = _pair_exchange_multi(gs, slot_axes, name="rs_px")
    ts, owns = [], []
    for g, a, ax, n in zip(gs, theirs, slot_axes, names):
        t, own = _pair_add(g, a, ck, slot_axis=ax, name="rs_pa_" + n)
        ts.append(t)
        owns.append(own)
    bs = _chip_exchange_multi(ts, slot_axes, name="rs_cx")
    finals = [_chip_add(own, b, ck, name="rs_ca_" + n) for own, b, n in zip(owns, bs, names)]
    return _pair_share_multi(finals, name="rs_ps")


def _all_reduce(arrs, tag):
    n = sum(a.size for a in arrs)
    rows = -(-n // PACK_W)
    rows = -(-rows // 8) * 8
    flat = jnp.concatenate([a.astype(F32).reshape(-1) for a in arrs])
    flat = jnp.pad(flat, (0, rows * PACK_W - n)).reshape(rows, PACK_W)
    tot = _all_reduce_small(flat, name=tag).reshape(-1)
    res, off = [], 0
    for a in arrs:
        res.append(tot[off:off + a.size].reshape(a.shape))
        off += a.size
    return res


def _pad_cols(w, cols):
    return jnp.pad(w, ((0, 0), (0, cols - w.shape[1])))


def _as3(a):
    return a.reshape(a.shape[0], 1, a.shape[1]) if a.ndim == 2 else a


def _col_layout(name, f):
    cs = f.shape[3]
    total = N_CHIPS * cs
    if name == "ev_w_in":
        main = 4 * f.shape[2] + 2 * SSM_GROUPS * SSM_STATE
        return _col_pieces(cs, [(0, main, 0, 0), (main, total, 1, 0)]), [main, 128]
    if name == "od_w_uq":
        qk = MLA_NOPE + MLA_ROPE
        heads = total // qk
        return _col_pieces(cs, [(hd * qk, (hd + 1) * qk, 0, hd * 256) for hd in range(heads)]), [heads * 256]
    return _col_pieces(cs, [(0, total, 0, 0)]), [total]


def _rows_of(buf, j):
    return buf[j].reshape(N_CHIPS * buf.shape[2], buf.shape[3])


def _layer_params(full, gathered, layer):
    j = layer // 2
    tag = f"asm{layer}_"
    p = {"norm_mix": full["norm_mix"][layer]}
    if layer % 2 == 0:
        f_in = gathered["ev_w_in"]
        w_main, w_dt = _assemble(f_in, j, *_col_layout("ev_w_in", f_in), name=tag + "in")
        p.update(w_in_main=w_main, w_in_dt=w_dt,
                 gm_ln_g=full["ev_gm_ln_g"][j], gm_ln_b=full["ev_gm_ln_b"][j], gm_ws=full["ev_gm_ws"][j],
                 gm_bs=full["ev_gm_bs"][j], conv_w=full["ev_conv_w"][j], conv_b=full["ev_conv_b"][j],
                 dt_bias=full["ev_dt_bias"][j], a_log=full["ev_a_log"][j], d_skip=full["ev_d_skip"][j],
                 ssm_norm_w=full["ev_ssm_norm_w"][j], w_out=_rows_of(gathered["ev_w_out"], j))
    else:
        f_uq, f_ukv = gathered["od_w_uq"], gathered["od_w_ukv"]
        w_in = _rows_of(gathered["od_w_in"], j)
        p.update(w_in=_pad_cols(w_in, -(-w_in.shape[1] // 128) * 128), q_norm=full["od_q_norm"][j],
                 kv_norm=full["od_kv_norm"][j],
                 w_uq=_assemble(f_uq, j, *_col_layout("od_w_uq", f_uq), name=tag + "uq")[0],
                 w_ukv=_assemble(f_ukv, j, *_col_layout("od_w_ukv", f_ukv), name=tag + "ukv")[0],
                 w_o=_rows_of(gathered["od_w_o"], j))
    f = {"norm_ffn": full["norm_ffn"][layer], "w_up_slabs": (gathered["ff_w_up"], layer),
         "conv_w": full["ff_conv_w"][layer], "conv_b": full["ff_conv_b"][layer],
         "w_down": _rows_of(gathered["ff_w_down"], layer)}
    return p, f


def _small_grads(g, gf, layer):
    out = {"norm_mix": g["norm_mix"], "norm_ffn": gf["norm_ffn"], "ff_conv_w": gf["conv_w"], "ff_conv_b": gf["conv_b"]}
    if layer % 2 == 0:
        out.update(ev_gm_ln_g=g["gm_ln_g"], ev_gm_ln_b=g["gm_ln_b"], ev_gm_ws=g["gm_ws"], ev_gm_bs=g["gm_bs"],
                   ev_conv_w=g["conv_w"], ev_conv_b=g["conv_b"], ev_dt_bias=g["dt_bias"], ev_a_log=g["a_log"],
                   ev_d_skip=g["d_skip"], ev_ssm_norm_w=g["ssm_norm_w"])
    else:
        out.update(od_q_norm=g["q_norm"], od_kv_norm=g["kv_norm"])
    return out


def _step(x, positions, loss_target, w, m, v):
    depth = w["norm_mix"].shape[0]
    h = x[0]
    tgt = loss_target[0]
    cos_p, sin_p = _rope_tables(positions[0])
    ck = jnp.stack([lax.axis_index("c"), 2 * lax.axis_index("x") + lax.axis_index("y")]).astype(jnp.int32)

    sharded = list(_SHARDED)
    slot_axes = [1 if _SHARDED[n] == -2 else 0 for n in sharded]
    bufs = [_cast_place(_as3(w[n]), ck, slot_axis=ax, dtype=BF16 if n in _BIG else F32, name="place_" + n)
            for n, ax in zip(sharded, slot_axes)]
    gathered = dict(zip(sharded, _all_gather_multi(bufs, slot_axes, name="ag")))
    full = {n: w[n] for n in _REPLICATED}
    for n in _SMALL:
        full[n] = _from_slabs(gathered[n], -1).reshape(w[n].shape[:-1] + (N_CHIPS * w[n].shape[-1],))

    params, saved = [], []
    for layer in range(depth):
        p, f = _layer_params(full, gathered, layer)
        if layer % 2 == 0:
            h, sv = _even_fwd(h, p, f"l{layer}m")
        else:
            h, sv = _odd_fwd(h, p, cos_p, sin_p, f"l{layer}m")
        h, svf = _ffn_fwd(h, f, f"l{layer}f")
        params.append((p, f))
        saved.append((sv, svf))

    loss, dh, dnf = _loss_bwd(h, w["norm_final"], tgt, name="loss")
    per_layer = []
    col_buf, row_buf = {}, {}
    row_parts = {"od_w_in": {}}

    def split(name, fulls, j):
        f = gathered[name]
        col_buf[name] = _split(fulls, j, _col_layout(name, f)[0], col_buf.get(name), f.shape, name=f"split_{name}{j}")

    for layer in reversed(range(depth)):
        p, f = params[layer]
        sv, svf = saved[layer]
        j = layer // 2
        dh, gf = _ffn_bwd(dh, f, svf, f"l{layer}fb", col_buf.get("ff_w_up"), (row_buf.get("ff_w_down"), layer, depth))
        col_buf["ff_w_up"] = gf["w_up_slabs"]
        row_buf["ff_w_down"] = gf["w_down"]
        if layer % 2 == 0:
            dh, g = _even_bwd(dh, p, sv, f"l{layer}mb", (row_buf.get("ev_w_out"), j, w["ev_w_out"].shape[0]))
            split("ev_w_in", [g["w_in_main"], g["w_in_dt"]], j)
            row_buf["ev_w_out"] = g["w_out"]
        else:
            dh, g = _odd_bwd(dh, p, cos_p, sin_p, sv, f"l{layer}mb", (row_buf.get("od_w_o"), j, w["od_w_o"].shape[0]))
            split("od_w_uq", [g["w_uq"]], j)
            split("od_w_ukv", [g["w_ukv"]], j)
            row_parts["od_w_in"][j] = g["w_in"][:, :w["od_w_in"].shape[2]]
            row_buf["od_w_o"] = g["w_o"]
        per_layer.append((layer, _small_grads(g, gf, layer)))
    per_layer.sort(key=lambda t: t[0])
    local = {"norm_final": dnf[0]}
    for n in list(_SMALL) + _REPLICATED:
        if n != "norm_final":
            local[n] = jnp.stack([lg[n] for _, lg in per_layer if n in lg], axis=0)

    gs = []
    for n in sharded:
        if n in col_buf:
            gs.append(col_buf[n])
        elif n in row_buf:
            gs.append(row_buf[n].reshape(gathered[n].shape))
        elif n in row_parts:
            parts = row_parts[n]
            gs.append(jnp.stack([parts[i] for i in range(len(parts))], axis=0).reshape(gathered[n].shape))
        else:
            gs.append(_to_slabs(_as3(local[n]), -1))
    grads = dict(zip(sharded, _reduce_scatter(gs, slot_axes, sharded, ck)))
    grads.update(zip(_REPLICATED, _all_reduce([local[n] for n in _REPLICATED], "ar")))
    loss = lax.psum(loss[0, 0], ("x", "y", "c"))

    delta, new_m, new_v = {}, {}, {}
    for n in _WEIGHTS:
        grads[n] = grads[n].reshape(w[n].shape)
        delta[n], new_m[n], new_v[n] = _adamw(w[n], grads[n], m[n], v[n], name="adamw_" + n)
    return (loss, dh[None], *[grads[n] for n in _WEIGHTS], *[delta[n] for n in _WEIGHTS],
            *[new_m[n] for n in _WEIGHTS], *[new_v[n] for n in _WEIGHTS])


def kernel(x, positions, norm_mix, norm_ffn, norm_final, ev_w_in, ev_gm_ln_g, ev_gm_ln_b, ev_gm_ws, ev_gm_bs, ev_conv_w, ev_conv_b, ev_dt_bias, ev_a_log, ev_d_skip, ev_ssm_norm_w, ev_w_out, od_w_in, od_q_norm, od_kv_norm, od_w_uq, od_w_ukv, od_w_o, ff_w_up, ff_conv_w, ff_conv_b, ff_w_down, loss_target, m_norm_mix, m_norm_ffn, m_norm_final, m_ev_w_in, m_ev_gm_ln_g, m_ev_gm_ln_b, m_ev_gm_ws, m_ev_gm_bs, m_ev_conv_w, m_ev_conv_b, m_ev_dt_bias, m_ev_a_log, m_ev_d_skip, m_ev_ssm_norm_w, m_ev_w_out, m_od_w_in, m_od_q_norm, m_od_kv_norm, m_od_w_uq, m_od_w_ukv, m_od_w_o, m_ff_w_up, m_ff_conv_w, m_ff_conv_b, m_ff_w_down, v_norm_mix, v_norm_ffn, v_norm_final, v_ev_w_in, v_ev_gm_ln_g, v_ev_gm_ln_b, v_ev_gm_ws, v_ev_gm_bs, v_ev_conv_w, v_ev_conv_b, v_ev_dt_bias, v_ev_a_log, v_ev_d_skip, v_ev_ssm_norm_w, v_ev_w_out, v_od_w_in, v_od_q_norm, v_od_kv_norm, v_od_w_uq, v_od_w_ukv, v_od_w_o, v_ff_w_up, v_ff_conv_w, v_ff_conv_b, v_ff_w_down):
    ws = (norm_mix, norm_ffn, norm_final, ev_w_in, ev_gm_ln_g, ev_gm_ln_b, ev_gm_ws, ev_gm_bs, ev_conv_w, ev_conv_b, ev_dt_bias, ev_a_log, ev_d_skip, ev_ssm_norm_w, ev_w_out, od_w_in, od_q_norm, od_kv_norm, od_w_uq, od_w_ukv, od_w_o, ff_w_up, ff_conv_w, ff_conv_b, ff_w_down)
    ms = (m_norm_mix, m_norm_ffn, m_norm_final, m_ev_w_in, m_ev_gm_ln_g, m_ev_gm_ln_b, m_ev_gm_ws, m_ev_gm_bs, m_ev_conv_w, m_ev_conv_b, m_ev_dt_bias, m_ev_a_log, m_ev_d_skip, m_ev_ssm_norm_w, m_ev_w_out, m_od_w_in, m_od_q_norm, m_od_kv_norm, m_od_w_uq, m_od_w_ukv, m_od_w_o, m_ff_w_up, m_ff_conv_w, m_ff_conv_b, m_ff_w_down)
    vs = (v_norm_mix, v_norm_ffn, v_norm_final, v_ev_w_in, v_ev_gm_ln_g, v_ev_gm_ln_b, v_ev_gm_ws, v_ev_gm_bs, v_ev_conv_w, v_ev_conv_b, v_ev_dt_bias, v_ev_a_log, v_ev_d_skip, v_ev_ssm_norm_w, v_ev_w_out, v_od_w_in, v_od_q_norm, v_od_kv_norm, v_od_w_uq, v_od_w_ukv, v_od_w_o, v_ff_w_up, v_ff_conv_w, v_ff_conv_b, v_ff_w_down)
    return _step(x, positions, loss_target, dict(zip(_WEIGHTS, ws)), dict(zip(_WEIGHTS, ms)), dict(zip(_WEIGHTS, vs)))
```

```python
import functools
import math

import jax
import jax.numpy as jnp
from jax import lax
from jax.experimental import pallas as pl
from jax.experimental.pallas import tpu as pltpu

F32 = jnp.float32
BF16 = jnp.bfloat16
EPS = 1e-6
CHUNK = 64
BLK = 128
GM_GROUPS = 8
SSM_HEAD_DIM = 64
SSM_GROUPS = 4
SSM_STATE = 128
SSM_CONV = 4
FFN_CONV = 3
MLA_NOPE = 128
MLA_ROPE = 64
MLA_V = 128
ROPE_THETA = 10000.0
V7X_VMEM_LIMIT = 56 * 1024 * 1024
HI = lax.Precision.HIGHEST

ADAM_LR = 0.001
ADAM_B1 = 0.9
ADAM_B2 = 0.999
ADAM_EPS = 1e-08
ADAM_WD = 0.01
ADAM_STEP = 10


def _cp(*sem):
    return pltpu.CompilerParams(dimension_semantics=sem if sem else None, vmem_limit_bytes=V7X_VMEM_LIMIT)


def _tile(n, cands):
    for c in cands:
        if n % c == 0:
            return c
    return n


def _row(v):
    return v.reshape(1, -1).astype(F32)


_MM_TILES = (1536, 1408, 1280, 1152, 1024, 896, 768, 640, 512, 384, 256, 128)
_MM_VMEM_BUDGET = 40 * 1024 * 1024


def _mm(a, b, *, ta=False, tb=False, out_dtype=F32, residual=None, b_layer=None, out_slabs=None, out_layer=None, name):
    m, k = (a.shape[1], a.shape[0]) if ta else a.shape
    if b_layer is None:
        n = b.shape[0] if tb else b.shape[1]
        assert k == (b.shape[1] if tb else b.shape[0]), (a.shape, b.shape, ta, tb)
        n_unit = k_unit = None
    else:
        _, _, rows, cs = b.shape
        n = rows if tb else N_CHIPS * cs
        assert k == (N_CHIPS * cs if tb else rows), (a.shape, b.shape, ta, tb)
        n_unit, k_unit = (None, cs) if tb else (cs, None)
    if out_slabs is not None:
        n_unit = out_slabs[2][3]
        assert n == N_CHIPS * n_unit and m == out_slabs[2][2]
    tm = _tile(m, (1024, 512, 256, 128))
    tn = _tile(n_unit or n, _MM_TILES)
    tk = k if (k_unit is None and k <= 2048) else _tile(k_unit or k, (1408, 1024, 768, 512, 384, 256, 128))
    nk = k // tk

    def vmem(tm_):
        bytes_ = 2 * (tm_ * tk * a.dtype.itemsize + tk * tn * b.dtype.itemsize + tm_ * tn * jnp.dtype(out_dtype).itemsize)
        bytes_ += (tm_ * tn * 4 if nk > 1 else 0) + (2 * tm_ * tn * residual.dtype.itemsize if residual is not None else 0)
        return bytes_

    while vmem(tm) > _MM_VMEM_BUDGET and tm % 256 == 0:
        tm //= 2
    dn = (((0 if ta else 1,), (1 if tb else 0,)), ((), ()))
    has_res = residual is not None

    def body(*refs):
        a_ref, b_ref = refs[0], refs[1]
        r_ref = refs[2] if has_res else None
        o_ref, acc = refs[-2], refs[-1]
        kk = pl.program_id(2)
        part = lax.dot_general(a_ref[...].astype(BF16), b_ref[...].astype(BF16), dn, preferred_element_type=F32)

        def finish(r):
            if has_res:
                r = r + r_ref[...].astype(F32)
            o_ref[...] = r.astype(out_dtype)

        if nk == 1:
            finish(part)
            return

        @pl.when(kk == 0)
        def _():
            acc[...] = part

        @pl.when(jnp.logical_and(kk > 0, kk < nk - 1))
        def _():
            acc[...] += part

        @pl.when(kk == nk - 1)
        def _():
            finish(acc[...] + part)

    a_spec = pl.BlockSpec((tk, tm), lambda i, j, kk: (kk, i)) if ta else pl.BlockSpec((tm, tk), lambda i, j, kk: (i, kk))
    if b_layer is None:
        b_spec = pl.BlockSpec((tn, tk), lambda i, j, kk: (j, kk)) if tb else pl.BlockSpec((tk, tn), lambda i, j, kk: (kk, j))
    elif tb:
        per = k_unit // tk
        b_spec = pl.BlockSpec((None, None, tn, tk), lambda i, j, kk: (kk // per, b_layer, j, kk % per))
    else:
        per = n_unit // tn
        b_spec = pl.BlockSpec((None, None, tk, tn), lambda i, j, kk: (j // per, b_layer, kk, j % per))
    in_specs = [a_spec, b_spec]
    args = [a, b]
    if has_res:
        in_specs.append(pl.BlockSpec((tm, tn), lambda i, j, kk: (i, j)))
        args.append(residual)
    aliases = {}
    if out_layer is not None:
        g_prev, layer, n_layers = out_layer
        out_spec = pl.BlockSpec((None, tm, tn), lambda i, j, kk: (layer, i, j))
        out_shape = jax.ShapeDtypeStruct((n_layers, m, n), out_dtype)
        if g_prev is not None:
            in_specs.append(_ANY)
            args.append(g_prev)
            aliases = {len(args) - 1: 0}
    elif out_slabs is None:
        out_spec = pl.BlockSpec((tm, tn), lambda i, j, kk: (i, j))
        out_shape = jax.ShapeDtypeStruct((m, n), out_dtype)
    else:
        g_prev, layer, shape = out_slabs
        per_o = n_unit // tn
        out_spec = pl.BlockSpec((None, None, tm, tn), lambda i, j, kk: (j // per_o, layer, i, j % per_o))
        out_shape = jax.ShapeDtypeStruct(shape, out_dtype)
        if g_prev is not None:
            in_specs.append(_ANY)
            args.append(g_prev)
            aliases = {len(args) - 1: 0}
    return pl.pallas_call(
        body, name=name,
        grid=(m // tm, n // tn, nk),
        in_specs=in_specs,
        out_specs=out_spec,
        out_shape=out_shape,
        input_output_aliases=aliases,
        scratch_shapes=[pltpu.VMEM((tm, tn) if nk > 1 else (8, 128), F32)],
        compiler_params=_cp("parallel", "parallel", "arbitrary"),
    )(*args)


def _rms_fwd(x, w, *, width=None, col=0, out_dtype=None, name):
    out_dtype = out_dtype or BF16
    s = x.shape[0]
    width = width or x.shape[1]
    tr = _tile(s, (256, 128))

    def body(x_ref, w_ref, o_ref):
        xv = x_ref[...]
        r = lax.rsqrt(jnp.mean(xv * xv, axis=-1, keepdims=True) + EPS)
        o_ref[...] = (xv * r * w_ref[...]).astype(out_dtype)

    return pl.pallas_call(
        body, name=name, grid=(s // tr,),
        in_specs=[pl.BlockSpec((tr, width), lambda i: (i, col)), pl.BlockSpec((1, width), lambda i: (0, 0))],
        out_specs=pl.BlockSpec((tr, width), lambda i: (i, 0)),
        out_shape=jax.ShapeDtypeStruct((s, width), out_dtype),
        compiler_params=_cp("parallel"),
    )(x, _row(w))


def _rms_bwd(x, w, dy, *, add=None, width=None, col=0, dy_col=0, out_dtype=F32, name):
    s = x.shape[0]
    width = width or x.shape[1]
    tr = _tile(s, (256, 128))
    has_add = add is not None

    def body(*refs):
        if has_add:
            x_ref, w_ref, dy_ref, add_ref, dx_ref, dw_ref = refs
        else:
            x_ref, w_ref, dy_ref, dx_ref, dw_ref = refs
        xv = x_ref[...]
        dyv = dy_ref[...].astype(F32)
        r = lax.rsqrt(jnp.mean(xv * xv, axis=-1, keepdims=True) + EPS)
        xh = xv * r
        g = dyv * w_ref[...]
        dx = r * (g - xh * jnp.mean(g * xh, axis=-1, keepdims=True))
        if has_add:
            dx = dx + add_ref[...]
        dx_ref[...] = dx.astype(out_dtype)

        @pl.when(pl.program_id(0) == 0)
        def _():
            dw_ref[...] = jnp.zeros_like(dw_ref)

        dw_ref[...] += jnp.sum(dyv * xh, axis=0, keepdims=True)

    in_specs = [pl.BlockSpec((tr, width), lambda i: (i, col)), pl.BlockSpec((1, width), lambda i: (0, 0)),
                pl.BlockSpec((tr, width), lambda i: (i, dy_col))]
    args = [x, _row(w), dy]
    if has_add:
        in_specs.append(pl.BlockSpec((tr, width), lambda i: (i, 0)))
        args.append(add)
    return pl.pallas_call(
        body, name=name, grid=(s // tr,),
        in_specs=in_specs,
        out_specs=[pl.BlockSpec((tr, width), lambda i: (i, 0)), pl.BlockSpec((1, width), lambda i: (0, 0))],
        out_shape=[jax.ShapeDtypeStruct((s, width), out_dtype), jax.ShapeDtypeStruct((1, width), F32)],
        compiler_params=_cp("arbitrary"),
    )(*args)


def _loss_bwd(h, w, tgt, *, name):
    s, d = h.shape
    tr = _tile(s, (256, 128))

    def body(x_ref, w_ref, t_ref, loss_ref, dx_ref, dw_ref):
        xv = x_ref[...]
        r = lax.rsqrt(jnp.mean(xv * xv, axis=-1, keepdims=True) + EPS)
        xh = xv * r
        e = xh * w_ref[...] - t_ref[...]
        part = 0.5 * jnp.sum(jnp.mean(e * e, axis=-1, keepdims=True), axis=0, keepdims=True)
        dyv = e * (1.0 / d)
        g = dyv * w_ref[...]
        dx_ref[...] = r * (g - xh * jnp.mean(g * xh, axis=-1, keepdims=True))

        @pl.when(pl.program_id(0) == 0)
        def _():
            dw_ref[...] = jnp.zeros_like(dw_ref)
            loss_ref[...] = jnp.zeros_like(loss_ref)

        dw_ref[...] += jnp.sum(dyv * xh, axis=0, keepdims=True)
        loss_ref[...] += jnp.broadcast_to(part, loss_ref.shape)

    return pl.pallas_call(
        body, name=name, grid=(s // tr,),
        in_specs=[pl.BlockSpec((tr, d), lambda i: (i, 0)), pl.BlockSpec((1, d), lambda i: (0, 0)),
                  pl.BlockSpec((tr, d), lambda i: (i, 0))],
        out_specs=[pl.BlockSpec((1, 128), lambda i: (0, 0)), pl.BlockSpec((tr, d), lambda i: (i, 0)),
                   pl.BlockSpec((1, d), lambda i: (0, 0))],
        out_shape=[jax.ShapeDtypeStruct((1, 128), F32), jax.ShapeDtypeStruct((s, d), F32),
                   jax.ShapeDtypeStruct((1, d), F32)],
        compiler_params=_cp("arbitrary"),
    )(h, _row(w), tgt)


_G0 = math.sqrt(2.0 / math.pi)
_G1 = 0.044715


def _gelu(x):
    return 0.5 * x * (1.0 + jnp.tanh(_G0 * (x + _G1 * x * x * x)))


def _gelu_and_grad(x):
    th = jnp.tanh(_G0 * (x + _G1 * x * x * x))
    val = 0.5 * x * (1.0 + th)
    grad = 0.5 * (1.0 + th) + 0.5 * x * (1.0 - th * th) * _G0 * (1.0 + 3.0 * _G1 * x * x)
    return val, grad


def _sigmoid(x):
    return 1.0 / (1.0 + jnp.exp(-x))


def _shift_down(x, k):
    if k == 0:
        return x
    rows = lax.broadcasted_iota(jnp.int32, x.shape, 0)
    return jnp.where(rows >= k, pltpu.roll(x, k, 0), 0.0)


def _shift_up(x, k):
    if k == 0:
        return x
    n = x.shape[0]
    rows = lax.broadcasted_iota(jnp.int32, x.shape, 0)
    return jnp.where(rows < n - k, pltpu.roll(x, n - k, 0), 0.0)


def _conv_rows(x, w_ref, b_ref, kw):
    y = b_ref[...] + w_ref[kw - 1:kw, :] * x
    for k in range(kw - 1):
        y = y + w_ref[k:k + 1, :] * _shift_down(x, kw - 1 - k)
    return y


def _conv_rows_bwd(x, dgc, w_ref, dw_ref, db_ref, kw):
    dx = w_ref[kw - 1:kw, :] * dgc
    dw_ref[kw - 1:kw, :] = jnp.sum(dgc * x, axis=0, keepdims=True)
    for k in range(kw - 1):
        sh = kw - 1 - k
        dx = dx + w_ref[k:k + 1, :] * _shift_up(dgc, sh)
        dw_ref[k:k + 1, :] = jnp.sum(dgc * _shift_down(x, sh), axis=0, keepdims=True)
    db_ref[...] = jnp.sum(dgc, axis=0, keepdims=True)
    return dx


def _ffn_mid_fwd(up, conv_w, conv_b, *, name):
    s = up.shape[0]
    f = up.shape[1] // 2
    tc = _tile(f, (256, 128))
    nf = f // tc

    def body(g_ref, v_ref, w_ref, b_ref, o_ref):
        gc = _conv_rows(g_ref[...], w_ref, b_ref, FFN_CONV)
        o_ref[...] = (_gelu(gc) * v_ref[...]).astype(BF16)

    return pl.pallas_call(
        body, name=name, grid=(nf,),
        in_specs=[pl.BlockSpec((s, tc), lambda j: (0, j)), pl.BlockSpec((s, tc), lambda j: (0, j + nf)),
                  pl.BlockSpec((FFN_CONV, tc), lambda j: (0, j)), pl.BlockSpec((1, tc), lambda j: (0, j))],
        out_specs=pl.BlockSpec((s, tc), lambda j: (0, j)),
        out_shape=jax.ShapeDtypeStruct((s, f), BF16),
        compiler_params=_cp("parallel"),
    )(up, up, conv_w, _row(conv_b))


def _ffn_mid_bwd(up, conv_w, conv_b, da, *, name):
    s = up.shape[0]
    f = up.shape[1] // 2
    tc = _tile(f, (256, 128))
    nf = f // tc

    def body(g_ref, v_ref, w_ref, b_ref, da_ref, dg_ref, dv_ref, dw_ref, db_ref):
        g = g_ref[...]
        gc = _conv_rows(g, w_ref, b_ref, FFN_CONV)
        gel, dgel = _gelu_and_grad(gc)
        dav = da_ref[...]
        dv_ref[...] = (dav * gel).astype(BF16)
        dgc = dav * v_ref[...] * dgel
        dg_ref[...] = _conv_rows_bwd(g, dgc, w_ref, dw_ref, db_ref, FFN_CONV).astype(BF16)

    col = lambda j: (0, j)
    return pl.pallas_call(
        body, name=name, grid=(nf,),
        in_specs=[pl.BlockSpec((s, tc), col), pl.BlockSpec((s, tc), lambda j: (0, j + nf)),
                  pl.BlockSpec((FFN_CONV, tc), col), pl.BlockSpec((1, tc), col), pl.BlockSpec((s, tc), col)],
        out_specs=[pl.BlockSpec((s, tc), col), pl.BlockSpec((s, tc), col),
                   pl.BlockSpec((FFN_CONV, tc), col), pl.BlockSpec((1, tc), col)],
        out_shape=[jax.ShapeDtypeStruct((s, f), BF16), jax.ShapeDtypeStruct((s, f), BF16),
                   jax.ShapeDtypeStruct((FFN_CONV, f), F32), jax.ShapeDtypeStruct((1, f), F32)],
        compiler_params=_cp("parallel"),
    )(up, up, conv_w, _row(conv_b), da)


def _gm_mask():
    r = lax.broadcasted_iota(jnp.int32, (BLK, BLK), 0) // CHUNK
    c = lax.broadcasted_iota(jnp.int32, (BLK, BLK), 1) // CHUNK
    return r >= c


def _gm_specs(s, gd):
    nb = s // BLK
    u_spec = pl.BlockSpec((BLK, gd), lambda g, n: (n, g))
    v_spec = pl.BlockSpec((BLK, gd), lambda g, n: (n, g + GM_GROUPS))
    vec_spec = pl.BlockSpec((1, gd), lambda g, n: (0, g))
    ws_spec = pl.BlockSpec((1, BLK, BLK), lambda g, n: (g, 0, 0))
    bs_spec = pl.BlockSpec((1, BLK, 1), lambda g, n: (g, 0, 0))
    return nb, u_spec, v_spec, vec_spec, ws_spec, bs_spec


def _gm_fwd(proj, ln_g, ln_b, ws, bs, *, name):
    s = proj.shape[0]
    gd = ln_g.shape[-1]
    w = GM_GROUPS * gd
    nb, u_spec, v_spec, vec_spec, ws_spec, bs_spec = _gm_specs(s, gd)

    def body(u_ref, v_ref, lg_ref, lb_ref, ws_ref, bs_ref, o_ref):
        ua = _gelu(u_ref[...])
        va = _gelu(v_ref[...])
        mu = jnp.mean(va, axis=-1, keepdims=True)
        vc = va - mu
        var = jnp.mean(vc * vc, axis=-1, keepdims=True)
        vn = vc * lax.rsqrt(var + EPS) * lg_ref[...] + lb_ref[...]
        wm = jnp.where(_gm_mask(), ws_ref[0], 0.0).astype(BF16)
        gate = jnp.dot(wm, vn.astype(BF16), preferred_element_type=F32) + bs_ref[0]
        o_ref[...] = (ua * gate).astype(BF16)

    return pl.pallas_call(
        body, name=name, grid=(GM_GROUPS, nb),
        in_specs=[u_spec, v_spec, vec_spec, vec_spec, ws_spec, bs_spec],
        out_specs=pl.BlockSpec((BLK, gd), lambda g, n: (n, g)),
        out_shape=jax.ShapeDtypeStruct((s, w), BF16),
        compiler_params=_cp("parallel", "parallel"),
    )(proj, proj, ln_g.reshape(1, w), ln_b.reshape(1, w), ws, bs.reshape(GM_GROUPS, BLK, 1))


def _gm_bwd(proj, ln_g, ln_b, ws, bs, dya, *, name):
    s = proj.shape[0]
    gd = ln_g.shape[-1]
    w = GM_GROUPS * gd
    nb, u_spec, v_spec, vec_spec, ws_spec, bs_spec = _gm_specs(s, gd)

    def body(u_ref, v_ref, lg_ref, lb_ref, ws_ref, bs_ref, dy_ref, du_ref, dv_ref, dlg_ref, dlb_ref, dws_ref, dbs_ref):
        ua, dua_du = _gelu_and_grad(u_ref[...])
        va, dva_dv = _gelu_and_grad(v_ref[...])
        mu = jnp.mean(va, axis=-1, keepdims=True)
        vc = va - mu
        var = jnp.mean(vc * vc, axis=-1, keepdims=True)
        rstd = lax.rsqrt(var + EPS)
        xh = vc * rstd
        vn = (xh * lg_ref[...] + lb_ref[...]).astype(BF16)
        mask = _gm_mask()
        wm = jnp.where(mask, ws_ref[0], 0.0).astype(BF16)
        gate = jnp.dot(wm, vn, preferred_element_type=F32) + bs_ref[0]
        dy = dy_ref[...]
        du_ref[...] = (dy * gate * dua_du).astype(BF16)
        dgate = dy * ua
        dgb = dgate.astype(BF16)
        dwm = lax.dot_general(dgb, vn, (((1,), (1,)), ((), ())), preferred_element_type=F32)
        dvn = lax.dot_general(wm, dgb, (((0,), (0,)), ((), ())), preferred_element_type=F32)
        dxh = dvn * lg_ref[...]
        dva = rstd * (dxh - jnp.mean(dxh, axis=-1, keepdims=True) - xh * jnp.mean(dxh * xh, axis=-1, keepdims=True))
        dv_ref[...] = (dva * dva_dv).astype(BF16)

        @pl.when(pl.program_id(1) == 0)
        def _():
            dlg_ref[...] = jnp.zeros_like(dlg_ref)
            dlb_ref[...] = jnp.zeros_like(dlb_ref)
            dws_ref[...] = jnp.zeros_like(dws_ref)
            dbs_ref[...] = jnp.zeros_like(dbs_ref)

        dlg_ref[...] += jnp.sum(dvn * xh, axis=0, keepdims=True)
        dlb_ref[...] += jnp.sum(dvn, axis=0, keepdims=True)
        dws_ref[0] += jnp.where(mask, dwm, 0.0)
        dbs_ref[0] += jnp.sum(dgate, axis=-1, keepdims=True)

    out_uv = pl.BlockSpec((BLK, gd), lambda g, n: (n, g))
    return pl.pallas_call(
        body, name=name, grid=(GM_GROUPS, nb),
        in_specs=[u_spec, v_spec, vec_spec, vec_spec, ws_spec, bs_spec, pl.BlockSpec((BLK, gd), lambda g, n: (n, g))],
        out_specs=[out_uv, out_uv, vec_spec, vec_spec, ws_spec, bs_spec],
        out_shape=[jax.ShapeDtypeStruct((s, w), BF16), jax.ShapeDtypeStruct((s, w), BF16),
                   jax.ShapeDtypeStruct((1, w), F32), jax.ShapeDtypeStruct((1, w), F32),
                   jax.ShapeDtypeStruct((GM_GROUPS, BLK, BLK), F32), jax.ShapeDtypeStruct((GM_GROUPS, BLK, 1), F32)],
        compiler_params=_cp("parallel", "arbitrary"),
    )(proj, proj, ln_g.reshape(1, w), ln_b.reshape(1, w), ws, bs.reshape(GM_GROUPS, BLK, 1), dya)


def _silu_conv_fwd(proj, conv_w, conv_b, *, col0, name):
    s = proj.shape[0]
    c = conv_w.shape[1]
    tc = _tile(c, (256, 128))
    off = col0 // tc

    def body(x_ref, w_ref, b_ref, o_ref):
        y = _conv_rows(x_ref[...], w_ref, b_ref, SSM_CONV)
        o_ref[...] = y * _sigmoid(y)

    col = lambda j: (0, j)
    return pl.pallas_call(
        body, name=name, grid=(c // tc,),
        in_specs=[pl.BlockSpec((s, tc), lambda j: (0, j + off)), pl.BlockSpec((SSM_CONV, tc), col), pl.BlockSpec((1, tc), col)],
        out_specs=pl.BlockSpec((s, tc), col),
        out_shape=jax.ShapeDtypeStruct((s, c), F32),
        compiler_params=_cp("parallel"),
    )(proj, conv_w, _row(conv_b))


def _silu_conv_bwd(proj, conv_w, conv_b, dact, *, col0, name):
    s = proj.shape[0]
    c = conv_w.shape[1]
    tc = _tile(c, (256, 128))
    off = col0 // tc

    def body(x_ref, w_ref, b_ref, d_ref, dx_ref, dw_ref, db_ref):
        x = x_ref[...]
        y = _conv_rows(x, w_ref, b_ref, SSM_CONV)
        sg = _sigmoid(y)
        dgc = d_ref[...] * sg * (1.0 + y * (1.0 - sg))
        dx_ref[...] = _conv_rows_bwd(x, dgc, w_ref, dw_ref, db_ref, SSM_CONV).astype(BF16)

    col = lambda j: (0, j)
    return pl.pallas_call(
        body, name=name, grid=(c // tc,),
        in_specs=[pl.BlockSpec((s, tc), lambda j: (0, j + off)), pl.BlockSpec((SSM_CONV, tc), col), pl.BlockSpec((1, tc), col),
                  pl.BlockSpec((s, tc), col)],
        out_specs=[pl.BlockSpec((s, tc), col), pl.BlockSpec((SSM_CONV, tc), col), pl.BlockSpec((1, tc), col)],
        out_shape=[jax.ShapeDtypeStruct((s, c), BF16), jax.ShapeDtypeStruct((SSM_CONV, c), F32),
                   jax.ShapeDtypeStruct((1, c), F32)],
        compiler_params=_cp("parallel"),
    )(proj, conv_w, _row(conv_b), dact)


def _head_select(heads):
    r = lax.broadcasted_iota(jnp.int32, (128, heads * SSM_HEAD_DIM), 0)
    c = lax.broadcasted_iota(jnp.int32, (128, heads * SSM_HEAD_DIM), 1) // SSM_HEAD_DIM
    return (r == c).astype(F32)


def _dt_fwd(dt_raw, dt_bias, *, heads, name):
    s = dt_raw.shape[0]
    d = heads * SSM_HEAD_DIM
    tr = _tile(s, (256, 128))

    def body(x_ref, b_ref, o_ref):
        pre = jnp.dot(x_ref[...] + b_ref[...], _head_select(heads), precision=HI, preferred_element_type=F32)
        o_ref[...] = jax.nn.softplus(pre)

    return pl.pallas_call(
        body, name=name, grid=(s // tr,),
        in_specs=[pl.BlockSpec((tr, 128), lambda i: (i, 0)), pl.BlockSpec((1, 128), lambda i: (0, 0))],
        out_specs=pl.BlockSpec((tr, d), lambda i: (i, 0)),
        out_shape=jax.ShapeDtypeStruct((s, d), F32),
        compiler_params=_cp("parallel"),
    )(dt_raw, dt_bias)


def _dt_bwd(dt_raw, dt_bias, zt, da_lane, dd_lane, a_row, *, heads, name):
    s = dt_raw.shape[0]
    d = heads * SSM_HEAD_DIM
    tr = _tile(s, (256, 128))
    nt = (((1,), (1,)), ((), ()))

    def body(x_ref, b_ref, z_ref, da_ref, dd_ref, a_ref, o_ref, db_ref, dal_ref, dds_ref):
        sel = _head_select(heads)
        ddt = lax.dot_general(z_ref[...], sel, nt, precision=HI, preferred_element_type=F32)
        g = ddt * _sigmoid(x_ref[...] + b_ref[...])
        o_ref[...] = g.astype(BF16)

        @pl.when(pl.program_id(0) == 0)
        def _():
            db_ref[...] = jnp.zeros_like(db_ref)
            da = lax.dot_general(da_ref[...], sel, nt, precision=HI, preferred_element_type=F32)
            dal_ref[...] = da * a_ref[...]
            dds_ref[...] = lax.dot_general(dd_ref[...], sel, nt, precision=HI, preferred_element_type=F32)

        db_ref[...] += jnp.sum(g, axis=0, keepdims=True)

    vec = pl.BlockSpec((1, 128), lambda i: (0, 0))
    lane = pl.BlockSpec((1, d), lambda i: (0, 0))
    return pl.pallas_call(
        body, name=name, grid=(s // tr,),
        in_specs=[pl.BlockSpec((tr, 128), lambda i: (i, 0)), vec, pl.BlockSpec((tr, d), lambda i: (i, 0)), lane, lane, vec],
        out_specs=[pl.BlockSpec((tr, 128), lambda i: (i, 0)), vec, vec, vec],
        out_shape=[jax.ShapeDtypeStruct((s, 128), BF16)] + [jax.ShapeDtypeStruct((1, 128), F32)] * 3,
        compiler_params=_cp("arbitrary"),
    )(dt_raw, dt_bias, zt, da_lane, dd_lane, a_row)


_NT = (((1,), (1,)), ((), ()))
_TN = (((0,), (0,)), ((), ()))


def _bdot(a, b, dn=None):
    if dn is None:
        return jnp.dot(a, b, preferred_element_type=F32)
    return lax.dot_general(a, b, dn, preferred_element_type=F32)


def _ssd_common(x_ref, b_ref, c_ref, dt_ref, a_ref):
    x = x_ref[...]
    dt = dt_ref[...]
    rows = lax.broadcasted_iota(jnp.int32, (BLK, BLK), 0)
    cols = lax.broadcasted_iota(jnp.int32, (BLK, BLK), 1)
    tl = (rows >= cols).astype(F32)
    acum = jnp.dot(tl, dt * a_ref[...], precision=HI, preferred_element_type=F32)
    alast = acum[BLK - 1:BLK, :]
    bm = b_ref[...].astype(BF16)
    cm = c_ref[...].astype(BF16)
    cb = _bdot(cm, bm, _NT)
    return x, dt, rows, cols, acum, alast, bm, cm, cb


def _ssd_decay(ap, apt, e, low):
    acol = ap[:, e * SSM_HEAD_DIM:e * SSM_HEAD_DIM + 1]
    arow = apt[e * SSM_HEAD_DIM:e * SSM_HEAD_DIM + 1, :]
    return jnp.where(low, jnp.exp(jnp.minimum(acol - arow, 0.0)), 0.0)


def _ssd_specs(s, d):
    gw = d // SSM_GROUPS
    bcol = d // SSM_STATE
    return gw, bcol


def _ssd_fwd(act, dte, a_lane, d_lane, *, name):
    s = act.shape[0]
    d = dte.shape[1]
    gw, bcol = _ssd_specs(s, d)
    npair = gw // 128
    nc = s // BLK

    def body(x_ref, b_ref, c_ref, dt_ref, a_ref, dsk_ref, y_ref, st_ref, ht):
        @pl.when(pl.program_id(1) == 0)
        def _():
            ht[...] = jnp.zeros_like(ht)

        x, dt, rows, cols, acum, alast, bm, cm, cb = _ssd_common(x_ref, b_ref, c_ref, dt_ref, a_ref)
        low = rows >= cols
        first = cols < SSM_HEAD_DIM
        xd = x * dt
        h_in = ht[...]
        st_ref[0] = h_in
        yoff = _bdot(cm, h_in.astype(BF16)) * jnp.exp(acum)
        parts = []
        for p in range(npair):
            ap = acum[:, p * 128:(p + 1) * 128]
            apt = ap.T
            xdp = xd[:, p * 128:(p + 1) * 128].astype(BF16)
            ys = [_bdot((cb * _ssd_decay(ap, apt, e, low)).astype(BF16), xdp) for e in range(2)]
            parts.append(jnp.where(first, ys[0], ys[1]))
        ydiag = parts[0] if npair == 1 else jnp.concatenate(parts, axis=1)
        y_ref[...] = ydiag + yoff + dsk_ref[...] * x
        w = (xd * jnp.exp(alast - acum)).astype(BF16)
        ht[...] = h_in * jnp.exp(alast) + _bdot(bm, w, _TN)

    blk = lambda g, c: (c, g)
    vec = pl.BlockSpec((1, gw), lambda g, c: (0, g))
    return pl.pallas_call(
        body, name=name, grid=(SSM_GROUPS, nc),
        in_specs=[pl.BlockSpec((BLK, gw), blk),
                  pl.BlockSpec((BLK, SSM_STATE), lambda g, c: (c, bcol + g)),
                  pl.BlockSpec((BLK, SSM_STATE), lambda g, c: (c, bcol + SSM_GROUPS + g)),
                  pl.BlockSpec((BLK, gw), blk), vec, vec],
        out_specs=[pl.BlockSpec((BLK, gw), blk), pl.BlockSpec((1, SSM_STATE, gw), lambda g, c: (c, 0, g))],
        out_shape=[jax.ShapeDtypeStruct((s, d), F32), jax.ShapeDtypeStruct((nc, SSM_STATE, d), F32)],
        scratch_shapes=[pltpu.VMEM((SSM_STATE, gw), F32)],
        compiler_params=_cp("parallel", "arbitrary"),
    )(act, act, act, dte, a_lane, d_lane)


def _ssd_bwd(act, dte, a_lane, d_lane, states, dy, *, name):
    s = act.shape[0]
    d = dte.shape[1]
    gw, bcol = _ssd_specs(s, d)
    npair = gw // 128
    nc = s // BLK
    gn = SSM_GROUPS * SSM_STATE

    def body(x_ref, b_ref, c_ref, dt_ref, a_ref, dsk_ref, st_ref, dy_ref,
             dx_ref, db_ref, dc_ref, zt_ref, dal_ref, ddl_ref, dht):
        @pl.when(pl.program_id(1) == 0)
        def _():
            dht[...] = jnp.zeros_like(dht)
            dal_ref[...] = jnp.zeros_like(dal_ref)
            ddl_ref[...] = jnp.zeros_like(ddl_ref)

        x, dt, rows, cols, acum, alast, bm, cm, cb = _ssd_common(x_ref, b_ref, c_ref, dt_ref, a_ref)
        low = rows >= cols
        first = cols < SSM_HEAD_DIM
        a = a_ref[...]
        xd = x * dt
        ea = jnp.exp(acum)
        wdec = jnp.exp(alast - acum)
        el = jnp.exp(alast)
        h_in = st_ref[0]
        hb = h_in.astype(BF16)
        g = dy_ref[...]
        dh = dht[...]
        dhb = dh.astype(BF16)

        yoff = _bdot(cm, hb) * ea
        geb = (g * ea).astype(BF16)
        dc = _bdot(geb, hb, _NT)
        u = _bdot(bm, dhb)
        wx = xd * wdec
        db = _bdot(wx.astype(BF16), dhb, _NT)
        dxd = wdec * u
        xwu = wx * u
        da_l = g * yoff - xwu
        dalast = jnp.sum(xwu, axis=0, keepdims=True) + el * jnp.sum(dh * h_in, axis=0, keepdims=True)
        dht[...] = dh * el + _bdot(cm, geb, _TN)

        dcb = jnp.zeros((BLK, BLK), F32)
        dxd_parts, col_parts = [], []
        for p in range(npair):
            ap = acum[:, p * 128:(p + 1) * 128]
            apt = ap.T
            xdp = xd[:, p * 128:(p + 1) * 128].astype(BF16)
            gp = g[:, p * 128:(p + 1) * 128]
            dxp = jnp.zeros((BLK, 128), F32)
            colsum = []
            for e in range(2):
                dec = _ssd_decay(ap, apt, e, low)
                m = cb * dec
                gpm = jnp.where(first if e == 0 else jnp.logical_not(first), gp, 0.0).astype(BF16)
                dm = _bdot(gpm, xdp, _NT)
                q = dm * m
                colsum.append(jnp.sum(q, axis=1, keepdims=True) - jnp.sum(q.T, axis=1, keepdims=True))
                dcb = dcb + dm * dec
                dxp = dxp + _bdot(m.astype(BF16), gpm, _TN)
            dxd_parts.append(dxp)
            col_parts.append(jnp.where(first, colsum[0], colsum[1]) * (1.0 / SSM_HEAD_DIM))
        cat = (lambda ps: ps[0] if npair == 1 else jnp.concatenate(ps, axis=1))
        dxd = dxd + cat(dxd_parts)
        da_l = da_l + cat(col_parts)
        rows_w = lax.broadcasted_iota(jnp.int32, (BLK, gw), 0)
        da_l = da_l + jnp.where(rows_w == BLK - 1, dalast, 0.0)
        dcbb = dcb.astype(BF16)
        dc_ref[...] = dc + _bdot(dcbb, bm)
        db_ref[...] = db + _bdot(dcbb, cm, _TN)
        tu = (rows <= cols).astype(F32)
        dda = jnp.dot(tu, da_l, precision=HI, preferred_element_type=F32)
        zt_ref[...] = dxd * x + dda * a
        dal_ref[...] += jnp.sum(dda * dt, axis=0, keepdims=True)
        ddl_ref[...] += jnp.sum(g * x, axis=0, keepdims=True)
        dx_ref[...] = dsk_ref[...] * g + dxd * dt

    blk = lambda g, c: (nc - 1 - c, g)
    vec = pl.BlockSpec((1, gw), lambda g, c: (0, g))
    bc_out = pl.BlockSpec((BLK, SSM_STATE), blk)
    return pl.pallas_call(
        body, name=name, grid=(SSM_GROUPS, nc),
        in_specs=[pl.BlockSpec((BLK, gw), blk),
                  pl.BlockSpec((BLK, SSM_STATE), lambda g, c: (nc - 1 - c, bcol + g)),
                  pl.BlockSpec((BLK, SSM_STATE), lambda g, c: (nc - 1 - c, bcol + SSM_GROUPS + g)),
                  pl.BlockSpec((BLK, gw), blk), vec, vec,
                  pl.BlockSpec((1, SSM_STATE, gw), lambda g, c: (nc - 1 - c, 0, g)),
                  pl.BlockSpec((BLK, gw), blk)],
        out_specs=[pl.BlockSpec((BLK, gw), blk), bc_out, bc_out, pl.BlockSpec((BLK, gw), blk), vec, vec],
        out_shape=[jax.ShapeDtypeStruct((s, d), F32), jax.ShapeDtypeStruct((s, gn), F32),
                   jax.ShapeDtypeStruct((s, gn), F32), jax.ShapeDtypeStruct((s, d), F32),
                   jax.ShapeDtypeStruct((1, d), F32), jax.ShapeDtypeStruct((1, d), F32)],
        scratch_shapes=[pltpu.VMEM((SSM_STATE, gw), F32)],
        compiler_params=_cp("parallel", "arbitrary"),
    )(act, act, act, dte, a_lane, d_lane, states, dy)


def _gnorm_fwd(y, proj, norm_w, *, zcol, name):
    s, d = y.shape
    tr = _tile(s, (256, 128))
    gw = d // SSM_GROUPS

    def body(y_ref, z_ref, w_ref, o_ref):
        z = z_ref[...]
        y2 = y_ref[...] * (z * _sigmoid(z))
        for g in range(SSM_GROUPS):
            sl = slice(g * gw, (g + 1) * gw)
            v = y2[:, sl]
            r = lax.rsqrt(jnp.mean(v * v, axis=-1, keepdims=True) + EPS)
            o_ref[:, sl] = (v * r * w_ref[:, sl]).astype(BF16)

    return pl.pallas_call(
        body, name=name, grid=(s // tr,),
        in_specs=[pl.BlockSpec((tr, d), lambda i: (i, 0)), pl.BlockSpec((tr, d), lambda i: (i, zcol)),
                  pl.BlockSpec((1, d), lambda i: (0, 0))],
        out_specs=pl.BlockSpec((tr, d), lambda i: (i, 0)),
        out_shape=jax.ShapeDtypeStruct((s, d), BF16),
        compiler_params=_cp("parallel"),
    )(y, proj, _row(norm_w))


def _gnorm_bwd(y, proj, norm_w, dout, *, zcol, dcol, name):
    s, d = y.shape
    tr = _tile(s, (256, 128))
    gw = d // SSM_GROUPS

    def body(y_ref, z_ref, w_ref, do_ref, dy_ref, dz_ref, dw_ref):
        @pl.when(pl.program_id(0) == 0)
        def _():
            dw_ref[...] = jnp.zeros_like(dw_ref)

        z = z_ref[...]
        yv = y_ref[...]
        sg = _sigmoid(z)
        sz = z * sg
        y2 = yv * sz
        for g in range(SSM_GROUPS):
            sl = slice(g * gw, (g + 1) * gw)
            v = y2[:, sl]
            do = do_ref[:, sl]
            r = lax.rsqrt(jnp.mean(v * v, axis=-1, keepdims=True) + EPS)
            xh = v * r
            gg = do * w_ref[:, sl]
            dy2 = r * (gg - xh * jnp.mean(gg * xh, axis=-1, keepdims=True))
            dw_ref[:, sl] += jnp.sum(do * xh, axis=0, keepdims=True)
            dy_ref[:, sl] = dy2 * sz[:, sl]
            dz_ref[:, sl] = (dy2 * yv[:, sl] * (sg[:, sl] * (1.0 + z[:, sl] * (1.0 - sg[:, sl])))).astype(BF16)

    return pl.pallas_call(
        body, name=name, grid=(s // tr,),
        in_specs=[pl.BlockSpec((tr, d), lambda i: (i, 0)), pl.BlockSpec((tr, d), lambda i: (i, zcol)),
                  pl.BlockSpec((1, d), lambda i: (0, 0)), pl.BlockSpec((tr, d), lambda i: (i, dcol))],
        out_specs=[pl.BlockSpec((tr, d), lambda i: (i, 0)), pl.BlockSpec((tr, d), lambda i: (i, 0)),
                   pl.BlockSpec((1, d), lambda i: (0, 0))],
        out_shape=[jax.ShapeDtypeStruct((s, d), F32), jax.ShapeDtypeStruct((s, d), BF16),
                   jax.ShapeDtypeStruct((1, d), F32)],
        compiler_params=_cp("arbitrary"),
    )(y, proj, _row(norm_w), dout)


def _lanes(v):
    return jnp.repeat(v.astype(F32), SSM_HEAD_DIM).reshape(1, -1)


def _pad128(v):
    return jnp.pad(v.astype(F32).reshape(1, -1), ((0, 0), (0, 128 - v.shape[-1])))


def _even_fwd(h, p, tag):
    d = h.shape[1]
    heads = d // SSM_HEAD_DIM
    hn = _rms_fwd(h, p["norm_mix"], name=tag + "_rms")
    proj = _mm(hn, p["w_in_main"], name=tag + "_in")
    pdt = _mm(hn, p["w_in_dt"], name=tag + "_indt")
    ya = _gm_fwd(proj, p["gm_ln_g"], p["gm_ln_b"], p["gm_ws"], p["gm_bs"], name=tag + "_gm")
    act = _silu_conv_fwd(proj, p["conv_w"], p["conv_b"], col0=3 * d, name=tag + "_conv")
    dte = _dt_fwd(pdt, _pad128(p["dt_bias"]), heads=heads, name=tag + "_dt")
    a = -jnp.exp(p["a_log"].astype(F32))
    y, states = _ssd_fwd(act, dte, _lanes(a), _lanes(p["d_skip"]), name=tag + "_ssd")
    yb = _gnorm_fwd(y, proj, p["ssm_norm_w"], zcol=2, name=tag + "_gn")
    cat = jnp.concatenate([ya, yb], axis=1)
    h1 = _mm(cat, p["w_out"], residual=h, name=tag + "_out")
    return h1, (h, hn, proj, pdt, act, dte, y, states, cat)


def _even_bwd(dh1, p, saved, tag, out_layer=None):
    h, hn, proj, pdt, act, dte, y, states, cat = saved
    d = h.shape[1]
    heads = d // SSM_HEAD_DIM
    a = -jnp.exp(p["a_log"].astype(F32))
    g = {}
    dcat = _mm(dh1, p["w_out"], tb=True, name=tag + "_dcat")
    g["w_out"] = _mm(cat, dh1, ta=True, out_layer=out_layer, name=tag + "_dwout")
    du, dv, dlg, dlb, dws, dbs = _gm_bwd(proj, p["gm_ln_g"], p["gm_ln_b"], p["gm_ws"], p["gm_bs"], dcat, name=tag + "_gmb")
    g["gm_ln_g"] = dlg.reshape(GM_GROUPS, -1)
    g["gm_ln_b"] = dlb.reshape(GM_GROUPS, -1)
    g["gm_ws"] = dws
    g["gm_bs"] = dbs.reshape(GM_GROUPS, BLK)
    dy, dz, dnw = _gnorm_bwd(y, proj, p["ssm_norm_w"], dcat, zcol=2, dcol=1, name=tag + "_gnb")
    g["ssm_norm_w"] = dnw[0]
    dxs, db, dc, zt, dal, ddl = _ssd_bwd(act, dte, _lanes(a), _lanes(p["d_skip"]), states, dy, name=tag + "_ssdb")
    ddt, ddtb, dalog, ddsk = _dt_bwd(pdt, _pad128(p["dt_bias"]), zt, dal, ddl, _pad128(a), heads=heads, name=tag + "_dtb")
    g["dt_bias"] = ddtb[0, :heads]
    g["a_log"] = dalog[0, :heads]
    g["d_skip"] = ddsk[0, :heads]
    dact = jnp.concatenate([dxs, db, dc], axis=1)
    dxbc, dcw, dcb = _silu_conv_bwd(proj, p["conv_w"], p["conv_b"], dact, col0=3 * d, name=tag + "_convb")
    g["conv_w"] = dcw
    g["conv_b"] = dcb[0]
    dproj = jnp.concatenate([du, dv, dz, dxbc], axis=1)
    dhn = _mm(dproj, p["w_in_main"], tb=True, name=tag + "_dhn")
    dhn = _mm(ddt, p["w_in_dt"], tb=True, residual=dhn, name=tag + "_dhn2")
    g["w_in_main"] = _mm(hn, dproj, ta=True, name=tag + "_dwin")
    g["w_in_dt"] = _mm(hn, ddt, ta=True, name=tag + "_dwdt")
    dh, dnm = _rms_bwd(h, p["norm_mix"], dhn, add=dh1, name=tag + "_rmsb")
    g["norm_mix"] = dnm[0]
    return dh, g


def _w_up(p):
    if "w_up_slabs" in p:
        return p["w_up_slabs"][0], {"b_layer": p["w_up_slabs"][1]}
    return p["w_up"], {}


def _ffn_fwd(h, p, tag):
    hn = _rms_fwd(h, p["norm_ffn"], name=tag + "_rms")
    w_up, kw = _w_up(p)
    up = _mm(hn, w_up, name=tag + "_up", **kw)
    a = _ffn_mid_fwd(up, p["conv_w"], p["conv_b"], name=tag + "_mid")
    h2 = _mm(a, p["w_down"], residual=h, name=tag + "_down")
    return h2, (h, hn, up, a)


def _ffn_bwd(dh2, p, saved, tag, g_up_prev=None, out_layer=None):
    h, hn, up, a = saved
    g = {}
    da = _mm(dh2, p["w_down"], tb=True, name=tag + "_da")
    g["w_down"] = _mm(a, dh2, ta=True, out_layer=out_layer, name=tag + "_dwdown")
    dg, dv, dcw, dcb = _ffn_mid_bwd(up, p["conv_w"], p["conv_b"], da, name=tag + "_midb")
    g["conv_w"] = dcw
    g["conv_b"] = dcb[0]
    dup = jnp.concatenate([dg, dv], axis=1)
    w_up, kw = _w_up(p)
    dhn = _mm(dup, w_up, tb=True, name=tag + "_dhn", **kw)
    if kw:
        g["w_up_slabs"] = _mm(hn, dup, ta=True, out_slabs=(g_up_prev, kw["b_layer"], w_up.shape), name=tag + "_dwup")
    else:
        g["w_up"] = _mm(hn, dup, ta=True, name=tag + "_dwup")
    dh, dnw = _rms_bwd(h, p["norm_ffn"], dhn, add=dh2, name=tag + "_rmsb")
    g["norm_ffn"] = dnw[0]
    return dh, g


def _rope(x, cos_p, sin_p):
    half = MLA_ROPE // 2
    lane = lax.broadcasted_iota(jnp.int32, x.shape, 1)
    swapped = jnp.where(lane < half, pltpu.roll(x, 128 - half, 1), pltpu.roll(x, half, 1))
    return x * cos_p + swapped * sin_p


def _rope_t(g, cos_p, sin_p):
    half = MLA_ROPE // 2
    gs = g * sin_p
    lane = lax.broadcasted_iota(jnp.int32, g.shape, 1)
    swapped = jnp.where(lane < half, pltpu.roll(gs, 128 - half, 1), pltpu.roll(gs, half, 1))
    return g * cos_p + swapped


ATTN_SEGMENTS = 4


def _attn_probs(qn_ref, qp_ref, kn_ref, kp_ref, cq_ref, sq_ref, ck_ref, sk_ref, tq, qb0):
    s = kn_ref.shape[0]
    scale = (MLA_NOPE + MLA_ROPE) ** -0.5
    qn = qn_ref[...].astype(BF16)
    qp = _rope(qp_ref[...], cq_ref[...], sq_ref[...]).astype(BF16)
    kn = kn_ref[...].astype(BF16)
    kp = _rope(kp_ref[...], ck_ref[...], sk_ref[...]).astype(BF16)
    sc = (_bdot(qn, kn, _NT) + _bdot(qp, kp, _NT)) * scale
    qpos = (pl.program_id(1) + qb0) * tq + lax.broadcasted_iota(jnp.int32, (tq, s), 0)
    kpos = lax.broadcasted_iota(jnp.int32, (tq, s), 1)
    sc = jnp.where(kpos // CHUNK <= qpos // CHUNK, sc, -jnp.inf)
    sc = sc - jnp.max(sc, axis=-1, keepdims=True)
    e = jnp.exp(sc)
    p = e / jnp.sum(e, axis=-1, keepdims=True)
    return p, qn, qp, kn, kp, scale


def _attn_segments(s, tq):
    nq = s // tq
    nseg = math.gcd(ATTN_SEGMENTS, nq)
    per = nq // nseg
    return [(seg * per, per, (seg + 1) * per * tq) for seg in range(nseg)]


def _attn_in_specs(s, tq, kr_col, qb0):
    return [pl.BlockSpec((tq, 128), lambda h, i: (i + qb0, 2 * h)), pl.BlockSpec((tq, 128), lambda h, i: (i + qb0, 2 * h + 1)),
            pl.BlockSpec((s, 128), lambda h, i: (0, 2 * h)), pl.BlockSpec((s, 128), lambda h, i: (0, kr_col)),
            pl.BlockSpec((tq, 128), lambda h, i: (i + qb0, 0)), pl.BlockSpec((tq, 128), lambda h, i: (i + qb0, 0)),
            pl.BlockSpec((s, 128), lambda h, i: (0, 0)), pl.BlockSpec((s, 128), lambda h, i: (0, 0)),
            pl.BlockSpec((s, 128), lambda h, i: (0, 2 * h + 1))]


def _attn_fwd(q, kv, proj, cos_p, sin_p, *, kr_col, name):
    s = q.shape[0]
    heads = q.shape[1] // 256
    tq = _tile(s, (256, 128))
    o = None
    for seg, (qb0, nqb, keys) in enumerate(_attn_segments(s, tq)):
        def body(qn_ref, qp_ref, kn_ref, kp_ref, cq_ref, sq_ref, ck_ref, sk_ref, v_ref, *rest, qb0=qb0):
            o_ref = rest[-1]
            p = _attn_probs(qn_ref, qp_ref, kn_ref, kp_ref, cq_ref, sq_ref, ck_ref, sk_ref, tq, qb0)[0]
            o_ref[...] = _bdot(p.astype(BF16), v_ref[...].astype(BF16)).astype(BF16)

        in_specs = _attn_in_specs(keys, tq, kr_col, qb0)
        args = [q, q, kv, proj, cos_p, sin_p, cos_p, sin_p, kv]
        aliases = {}
        if o is not None:
            in_specs.append(_ANY)
            args.append(o)
            aliases = {len(args) - 1: 0}
        o = pl.pallas_call(
            body, name=f"{name}{seg}", grid=(heads, nqb),
            in_specs=in_specs,
            out_specs=pl.BlockSpec((tq, 128), lambda h, i, qb0=qb0: (i + qb0, h)),
            out_shape=jax.ShapeDtypeStruct((s, heads * MLA_V), BF16),
            input_output_aliases=aliases,
            compiler_params=_cp("parallel", "parallel"),
        )(*args)
    return o


def _attn_bwd(q, kv, proj, cos_p, sin_p, do, *, kr_col, name):
    s = q.shape[0]
    heads = q.shape[1] // 256
    tq = _tile(s, (256, 128))
    dq = dkv = dkp = None
    for seg, (qb0, nqb, keys) in reversed(list(enumerate(_attn_segments(s, tq)))):
        first = dq is None

        def body(qn_ref, qp_ref, kn_ref, kp_ref, cq_ref, sq_ref, ck_ref, sk_ref, v_ref, do_ref, *rest, qb0=qb0, first=first):
            dq_ref, dkv_ref, dkp_ref = rest[-3:]
            h = pl.program_id(0)
            i = pl.program_id(1)

            @pl.when(i == 0)
            def _():
                dkv_ref[...] = jnp.zeros_like(dkv_ref) if first else rest[1][...]

            @pl.when(jnp.logical_and(h == 0, i == 0))
            def _():
                dkp_ref[...] = jnp.zeros_like(dkp_ref) if first else rest[2][...]

            p, qn, qp, kn, kp, scale = _attn_probs(qn_ref, qp_ref, kn_ref, kp_ref, cq_ref, sq_ref, ck_ref, sk_ref, tq, qb0)
            dob = do_ref[...].astype(BF16)
            pb = p.astype(BF16)
            dv = _bdot(pb, dob, _TN)
            dp = _bdot(dob, v_ref[...].astype(BF16), _NT)
            ds = (p * (dp - jnp.sum(dp * p, axis=-1, keepdims=True)) * scale).astype(BF16)
            dq_ref[:, 0:128] = _bdot(ds, kn).astype(BF16)
            dq_ref[:, 128:256] = _rope_t(_bdot(ds, kp), cq_ref[...], sq_ref[...]).astype(BF16)
            dkv_ref[:, 0:128] += _bdot(ds, qn, _TN)
            dkv_ref[:, 128:256] += dv
            dkp_ref[...] += _rope_t(_bdot(ds, qp, _TN), ck_ref[...], sk_ref[...])

        dkv_spec = pl.BlockSpec((keys, 256), lambda h, i: (0, h))
        dkp_spec = pl.BlockSpec((keys, 128), lambda h, i: (0, 0))
        in_specs = _attn_in_specs(keys, tq, kr_col, qb0) + [pl.BlockSpec((tq, 128), lambda h, i, qb0=qb0: (i + qb0, h))]
        args = [q, q, kv, proj, cos_p, sin_p, cos_p, sin_p, kv, do]
        aliases = {}
        if not first:
            in_specs += [_ANY, dkv_spec, dkp_spec]
            args += [dq, dkv, dkp]
            aliases = {len(args) - 3: 0, len(args) - 2: 1, len(args) - 1: 2}
        dq, dkv, dkp = pl.pallas_call(
            body, name=f"{name}{seg}", grid=(heads, nqb),
            in_specs=in_specs,
            out_specs=[pl.BlockSpec((tq, 256), lambda h, i, qb0=qb0: (i + qb0, h)), dkv_spec, dkp_spec],
            out_shape=[jax.ShapeDtypeStruct((s, heads * 256), BF16), jax.ShapeDtypeStruct((s, heads * 256), F32),
                       jax.ShapeDtypeStruct((s, 128), F32)],
            input_output_aliases=aliases,
            compiler_params=_cp("arbitrary", "arbitrary"),
        )(*args)
    return dq, dkv, dkp


def _rope_tables(positions):
    inv_freq = ROPE_THETA ** (-jnp.arange(0, MLA_ROPE, 2, dtype=F32) / MLA_ROPE)
    ang = positions.astype(F32)[:, None] * inv_freq
    cos, sin = jnp.cos(ang), jnp.sin(ang)
    zero = jnp.zeros((positions.shape[0], 128 - MLA_ROPE), F32)
    return jnp.concatenate([cos, cos, zero], axis=1), jnp.concatenate([-sin, sin, zero], axis=1)


def _odd_fwd(h, p, cos_p, sin_p, tag):
    rank = p["q_norm"].shape[0]
    hn = _rms_fwd(h, p["norm_mix"], name=tag + "_rms")
    proj = _mm(hn, p["w_in"], name=tag + "_in")
    cqn = _rms_fwd(proj, p["q_norm"], width=rank, col=0, name=tag + "_qn")
    ckvn = _rms_fwd(proj, p["kv_norm"], width=rank, col=1, name=tag + "_kvn")
    q = _mm(cqn, p["w_uq"], name=tag + "_uq")
    kv = _mm(ckvn, p["w_ukv"], name=tag + "_ukv")
    o = _attn_fwd(q, kv, proj, cos_p, sin_p, kr_col=2 * rank // 128, name=tag + "_attn")
    h1 = _mm(o, p["w_o"], residual=h, name=tag + "_o")
    return h1, (h, hn, proj, cqn, ckvn, q, kv, o)


def _odd_bwd(dh1, p, cos_p, sin_p, saved, tag, out_layer=None):
    h, hn, proj, cqn, ckvn, q, kv, o = saved
    rank = p["q_norm"].shape[0]
    g = {}
    do = _mm(dh1, p["w_o"], tb=True, name=tag + "_do")
    g["w_o"] = _mm(o, dh1, ta=True, out_layer=out_layer, name=tag + "_dwo")
    dq, dkv, dkp = _attn_bwd(q, kv, proj, cos_p, sin_p, do, kr_col=2 * rank // 128, name=tag + "_attnb")
    dcqn = _mm(dq, p["w_uq"], tb=True, name=tag + "_dcqn")
    g["w_uq"] = _mm(cqn, dq, ta=True, name=tag + "_dwuq")
    dckvn = _mm(dkv, p["w_ukv"], tb=True, name=tag + "_dckvn")
    g["w_ukv"] = _mm(ckvn, dkv, ta=True, name=tag + "_dwukv")
    dcq, dqn = _rms_bwd(proj, p["q_norm"], dcqn, width=rank, col=0, out_dtype=BF16, name=tag + "_qnb")
    dckv, dkvn = _rms_bwd(proj, p["kv_norm"], dckvn, width=rank, col=1, out_dtype=BF16, name=tag + "_kvnb")
    g["q_norm"] = dqn[0]
    g["kv_norm"] = dkvn[0]
    dproj = jnp.concatenate([dcq, dckv, dkp.astype(BF16)], axis=1)
    dhn = _mm(dproj, p["w_in"], tb=True, name=tag + "_dhn")
    g["w_in"] = _mm(hn, dproj, ta=True, name=tag + "_dwin")
    dh, dnm = _rms_bwd(h, p["norm_mix"], dhn, add=dh1, name=tag + "_rmsb")
    g["norm_mix"] = dnm[0]
    return dh, g


PACK_W = 1024
N_CHIPS = 4
_MESH = pl.DeviceIdType.MESH
_ANY = pl.BlockSpec(memory_space=pl.ANY)


def _place():
    x, y, c = lax.axis_index("x"), lax.axis_index("y"), lax.axis_index("c")
    others = [(1 - x, y), (x, 1 - y), (1 - x, 1 - y)]
    return x, y, c, others


def _row_tile(r, c, itemsize):
    for cand in (512, 256, 128, 64, 32, 16, 8):
        if r % cand == 0 and cand * c * itemsize <= 2 * 1024 * 1024:
            return cand
    return r


def _slot_index(slot_axis, slot, layer, i):
    return (slot, layer, i, 0) if slot_axis == 0 else (layer, slot, i, 0)


def _cast_place(w, ck, *, slot_axis, dtype, name):
    nl, r, c = w.shape
    tr = _row_tile(r, c, 4)
    shape = (N_CHIPS, nl, r, c) if slot_axis == 0 else (nl, N_CHIPS, r, c)

    def body(ck_ref, w_ref, o_ref):
        o_ref[...] = w_ref[...].astype(dtype)

    return pl.pallas_call(
        body, name=name,
        grid_spec=pltpu.PrefetchScalarGridSpec(
            num_scalar_prefetch=1, grid=(nl, r // tr),
            in_specs=[pl.BlockSpec((None, tr, c), lambda l, i, s: (l, i, 0))],
            out_specs=pl.BlockSpec((None, None, tr, c), lambda l, i, s: _slot_index(slot_axis, s[1], l, i))),
        out_shape=jax.ShapeDtypeStruct(shape, dtype),
        compiler_params=_cp("parallel", "parallel"),
    )(ck, w)


def _region(ref, slot_axis, slot, half):
    lh = ref.shape[1 - slot_axis] // 2
    if slot_axis == 0:
        return ref.at[slot, pl.ds(half * lh, lh)]
    return ref.at[pl.ds(half * lh, lh), slot]


def _all_gather_multi(bufs, slot_axes, *, name):
    n = len(bufs)

    def body(*refs):
        outs = refs[n:2 * n]
        send_sems, recv_sems = refs[2 * n], refs[2 * n + 1]
        x, y, c, others = _place()
        k = 2 * x + y
        sibling = (x, y, 1 - c)

        def copy(a, slot, half, sem, to):
            blk = _region(outs[a], slot_axes[a], slot, half)
            return pltpu.make_async_remote_copy(src_ref=blk, dst_ref=blk, send_sem=send_sems.at[6 * a + sem],
                                                recv_sem=recv_sems.at[6 * a + sem], device_id=to, device_id_type=_MESH)

        first = [copy(a, k, c, j, (cx, cy, c)) for a in range(n) for j, (cx, cy) in enumerate(others)]
        for cp in first:
            cp.start()
        passed = []
        for a in range(n):
            for j, (cx, cy) in enumerate(others):
                copy(a, 2 * cx + cy, c, j, (cx, cy, c)).wait_recv()
                fw = copy(a, 2 * cx + cy, c, 3 + j, sibling)
                fw.start()
                passed.append(fw)
        for a in range(n):
            for j, (cx, cy) in enumerate(others):
                copy(a, 2 * cx + cy, 1 - c, 3 + j, sibling).wait_recv()
        for cp in first + passed:
            cp.wait_send()

    return pl.pallas_call(
        body, name=name,
        in_specs=[_ANY] * n, out_specs=[_ANY] * n,
        out_shape=[jax.ShapeDtypeStruct(b.shape, b.dtype) for b in bufs],
        input_output_aliases={a: a for a in range(n)},
        scratch_shapes=[pltpu.SemaphoreType.DMA((6 * n,)), pltpu.SemaphoreType.DMA((6 * n,))],
    )(*bufs)


def _half_shape(shape, slot_axis):
    shape = list(shape)
    shape[1 - slot_axis] //= 2
    return tuple(shape)


def _pair_exchange_multi(gs, slot_axes, *, name):
    n = len(gs)

    def body(*refs):
        g_refs, a_refs = refs[:n], refs[n:2 * n]
        send_sems, recv_sems = refs[2 * n], refs[2 * n + 1]
        x, y, c, _ = _place()
        copies = []
        for a in range(n):
            lh = a_refs[a].shape[1 - slot_axes[a]]
            src = g_refs[a].at[:, pl.ds((1 - c) * lh, lh)] if slot_axes[a] == 0 else g_refs[a].at[pl.ds((1 - c) * lh, lh)]
            copies.append(pltpu.make_async_remote_copy(
                src_ref=src, dst_ref=a_refs[a], send_sem=send_sems.at[a], recv_sem=recv_sems.at[a],
                device_id=(x, y, 1 - c), device_id_type=_MESH))
        for cp in copies:
            cp.start()
        for cp in copies:
            cp.wait()

    return pl.pallas_call(
        body, name=name, in_specs=[_ANY] * n, out_specs=[_ANY] * n,
        out_shape=[jax.ShapeDtypeStruct(_half_shape(g.shape, ax), g.dtype) for g, ax in zip(gs, slot_axes)],
        scratch_shapes=[pltpu.SemaphoreType.DMA((n,)), pltpu.SemaphoreType.DMA((n,))],
    )(*gs)


def _pair_add(g, a, ck, *, slot_axis, name):
    lh = a.shape[1 - slot_axis]
    r, c = a.shape[2:]
    tr = _row_tile(r, c, 4)

    def body(ck_ref, g_ref, a_ref, t_ref, own_ref):
        v = g_ref[...] + a_ref[...]
        t_ref[...] = v.astype(BF16)

        @pl.when(pl.program_id(2) == ck_ref[1])
        def _():
            own_ref[...] = v

    blk = (None, None, tr, c)
    return pl.pallas_call(
        body, name=name,
        grid_spec=pltpu.PrefetchScalarGridSpec(
            num_scalar_prefetch=1, grid=(lh, r // tr, N_CHIPS),
            in_specs=[pl.BlockSpec(blk, lambda l, i, j, s: _slot_index(slot_axis, j, s[0] * lh + l, i)),
                      pl.BlockSpec(blk, lambda l, i, j, s: _slot_index(slot_axis, j, l, i))],
            out_specs=[pl.BlockSpec(blk, lambda l, i, j, s: _slot_index(slot_axis, j, l, i)),
                       pl.BlockSpec((None, tr, c), lambda l, i, j, s: (l, i, 0))]),
        out_shape=[jax.ShapeDtypeStruct(a.shape, BF16), jax.ShapeDtypeStruct((lh, r, c), F32)],
        compiler_params=_cp("arbitrary", "arbitrary", "arbitrary"),
    )(ck, g, a)


def _chip_exchange_multi(ts, slot_axes, *, name):
    n = len(ts)

    def body(*refs):
        t_refs, b_refs = refs[:n], refs[n:2 * n]
        send_sems, recv_sems = refs[2 * n], refs[2 * n + 1]
        x, y, c, others = _place()
        copies = []
        for a in range(n):
            for j, (cx, cy) in enumerate(others):
                src = t_refs[a].at[2 * cx + cy] if slot_axes[a] == 0 else t_refs[a].at[:, 2 * cx + cy]
                copies.append(pltpu.make_async_remote_copy(
                    src_ref=src, dst_ref=b_refs[a].at[j], send_sem=send_sems.at[3 * a + j], recv_sem=recv_sems.at[3 * a + j],
                    device_id=(cx, cy, c), device_id_type=_MESH))
        for cp in copies:
            cp.start()
        for cp in copies:
            cp.wait()

    def out_shape(t, ax):
        lh = t.shape[1 - ax]
        return jax.ShapeDtypeStruct((N_CHIPS - 1, lh) + t.shape[2:], t.dtype)

    return pl.pallas_call(
        body, name=name, in_specs=[_ANY] * n, out_specs=[_ANY] * n,
        out_shape=[out_shape(t, ax) for t, ax in zip(ts, slot_axes)],
        scratch_shapes=[pltpu.SemaphoreType.DMA((3 * n,)), pltpu.SemaphoreType.DMA((3 * n,))],
    )(*ts)


def _chip_add(own, b, ck, *, name):
    lh, r, c = own.shape
    tr = _row_tile(r, c, 4)

    def body(ck_ref, o_ref, b_ref, r_ref):
        acc = o_ref[...]
        for j in range(N_CHIPS - 1):
            acc = acc + b_ref[j].astype(F32)
        r_ref[...] = acc

    return pl.pallas_call(
        body, name=name,
        grid_spec=pltpu.PrefetchScalarGridSpec(
            num_scalar_prefetch=1, grid=(lh, r // tr),
            in_specs=[pl.BlockSpec((None, tr, c), lambda l, i, s: (l, i, 0)),
                      pl.BlockSpec((N_CHIPS - 1, None, tr, c), lambda l, i, s: (0, l, i, 0))],
            out_specs=pl.BlockSpec((None, tr, c), lambda l, i, s: (s[0] * lh + l, i, 0))),
        out_shape=jax.ShapeDtypeStruct((2 * lh, r, c), F32),
        compiler_params=_cp("parallel", "parallel"),
    )(ck, own, b)


def _pair_share_multi(finals, *, name):
    n = len(finals)

    def body(*refs):
        outs = refs[n:2 * n]
        send_sems, recv_sems = refs[2 * n], refs[2 * n + 1]
        x, y, c, _ = _place()

        def copy(a, half):
            lh = outs[a].shape[0] // 2
            blk = outs[a].at[pl.ds(half * lh, lh)]
            return pltpu.make_async_remote_copy(src_ref=blk, dst_ref=blk, send_sem=send_sems.at[a], recv_sem=recv_sems.at[a],
                                                device_id=(x, y, 1 - c), device_id_type=_MESH)

        sends = [copy(a, c) for a in range(n)]
        for cp in sends:
            cp.start()
        for a in range(n):
            copy(a, 1 - c).wait_recv()
        for cp in sends:
            cp.wait_send()

    return pl.pallas_call(
        body, name=name, in_specs=[_ANY] * n, out_specs=[_ANY] * n,
        out_shape=[jax.ShapeDtypeStruct(f.shape, f.dtype) for f in finals],
        input_output_aliases={a: a for a in range(n)},
        scratch_shapes=[pltpu.SemaphoreType.DMA((n,)), pltpu.SemaphoreType.DMA((n,))],
    )(*finals)


def _col_pieces(cs, segments):
    pieces = []
    for k in range(N_CHIPS):
        for gs, ge, oi, ds in segments:
            lo, hi = max(k * cs, gs), min((k + 1) * cs, ge)
            if lo < hi:
                pieces.append((k, lo - k * cs, oi, ds + lo - gs, hi - lo))
    return pieces


def _assemble(f, layer, pieces, widths, *, name):
    _, _, r, c = f.shape
    tr = _row_tile(r, max(max(widths), N_CHIPS * c), f.dtype.itemsize)
    covered = sum(p[4] for p in pieces) == sum(widths)

    def body(f_ref, *o_refs):
        if not covered:
            for o in o_refs:
                o[...] = jnp.zeros_like(o)
        for k, s0, oi, d0, wd in pieces:
            o_refs[oi][:, d0:d0 + wd] = f_ref[k, :, s0:s0 + wd]

    return pl.pallas_call(
        body, name=name, grid=(r // tr,),
        in_specs=[pl.BlockSpec((N_CHIPS, None, tr, c), lambda i: (0, layer, i, 0))],
        out_specs=[pl.BlockSpec((tr, w), lambda i: (i, 0)) for w in widths],
        out_shape=[jax.ShapeDtypeStruct((r, w), f.dtype) for w in widths],
        compiler_params=_cp("parallel"),
    )(f)


def _split(fulls, layer, pieces, g_prev, shape, *, name):
    _, _, r, c = shape
    widths = [t.shape[1] for t in fulls]
    tr = _row_tile(r, max(max(widths), N_CHIPS * c), 4)
    nf = len(fulls)

    def body(*refs):
        g_ref = refs[-1]
        for k, s0, oi, d0, wd in pieces:
            g_ref[k, :, s0:s0 + wd] = refs[oi][:, d0:d0 + wd]

    in_specs = [pl.BlockSpec((tr, w), lambda i: (i, 0)) for w in widths]
    args = list(fulls)
    aliases = {}
    if g_prev is not None:
        in_specs.append(_ANY)
        args.append(g_prev)
        aliases = {nf: 0}
    return pl.pallas_call(
        body, name=name, grid=(r // tr,),
        in_specs=in_specs,
        out_specs=pl.BlockSpec((N_CHIPS, None, tr, c), lambda i: (0, layer, i, 0)),
        out_shape=jax.ShapeDtypeStruct(shape, F32),
        input_output_aliases=aliases,
        compiler_params=_cp("parallel"),
    )(*args)


def _all_reduce_small(v, *, name):
    r, w = v.shape
    n_dev = 8

    def body(x_ref, sum_ref, out_ref, send_sems, recv_sems, local_sem):
        x, y, c, others = _place()
        me, sibling = (x, y, c), (x, y, 1 - c)

        def rows(px, py, pc):
            return out_ref.at[pl.ds((4 * px + 2 * py + pc) * r, r), :]

        def copy(k, block, to, src=None):
            return pltpu.make_async_remote_copy(
                src_ref=rows(*block) if src is None else src, dst_ref=rows(*block),
                send_sem=send_sems.at[k], recv_sem=recv_sems.at[k], device_id=to, device_id_type=_MESH)

        mine = pltpu.make_async_copy(x_ref, rows(*me), local_sem)
        mine.start()
        first = [copy(0, me, sibling, src=x_ref)]
        first += [copy(1 + j, me, (*chip, c), src=x_ref) for j, chip in enumerate(others)]
        for cp in first:
            cp.start()
        passed = [copy(4 + j, (*chip, c), sibling) for j, chip in enumerate(others)]
        for j, chip in enumerate(others):
            copy(1 + j, (*chip, c), me).wait_recv()
            passed[j].start()
        copy(0, sibling, me).wait_recv()
        for j, chip in enumerate(others):
            copy(4 + j, (*chip, 1 - c), me).wait_recv()
        for cp in first + passed:
            cp.wait_send()
        mine.wait()
        acc = out_ref[pl.ds(0, r), :]
        for dev in range(1, n_dev):
            acc = acc + out_ref[pl.ds(dev * r, r), :]
        sum_ref[...] = acc

    vmem = pl.BlockSpec(memory_space=pltpu.VMEM)
    return pl.pallas_call(
        body, name=name, in_specs=[vmem], out_specs=[vmem, vmem],
        out_shape=[jax.ShapeDtypeStruct((r, w), F32), jax.ShapeDtypeStruct((n_dev * r, w), F32)],
        scratch_shapes=[pltpu.SemaphoreType.DMA((7,)), pltpu.SemaphoreType.DMA((7,)), pltpu.SemaphoreType.DMA],
        compiler_params=pltpu.CompilerParams(vmem_limit_bytes=V7X_VMEM_LIMIT),
    )(v)[0]


def _adamw(w, g, m, v, *, name):
    shape = w.shape
    cols = shape[-1]
    rows = max(1, math.prod(shape[:-1]))
    tr = rows
    for cand in (512, 256, 128, 64, 32, 16, 8):
        if rows % cand == 0 and cand * cols * 4 <= 2 * 1024 * 1024:
            tr = cand
            break
    c1 = 1.0 - ADAM_B1 ** ADAM_STEP
    c2 = 1.0 - ADAM_B2 ** ADAM_STEP

    def body(w_ref, g_ref, m_ref, v_ref, go_ref, d_ref, mo_ref, vo_ref):
        gv = g_ref[...]
        go_ref[...] = gv
        mn = ADAM_B1 * m_ref[...] + (1.0 - ADAM_B1) * gv
        vn = ADAM_B2 * v_ref[...] + (1.0 - ADAM_B2) * (gv * gv)
        d_ref[...] = -ADAM_LR * ((mn / c1) / (jnp.sqrt(vn / c2) + ADAM_EPS) + ADAM_WD * w_ref[...])
        mo_ref[...] = mn
        vo_ref[...] = vn

    spec = pl.BlockSpec((tr, cols), lambda i: (i, 0))
    outs = pl.pallas_call(
        body, name=name, grid=(rows // tr,),
        in_specs=[spec] * 4, out_specs=[spec] * 4,
        out_shape=[jax.ShapeDtypeStruct((rows, cols), F32)] * 4,
        compiler_params=_cp("parallel"),
    )(*[t.reshape(rows, cols) for t in (w, g, m, v)])
    return [o.reshape(shape) for o in outs]


_WEIGHTS = ["norm_mix", "norm_ffn", "norm_final", "ev_w_in", "ev_gm_ln_g", "ev_gm_ln_b", "ev_gm_ws", "ev_gm_bs",
            "ev_conv_w", "ev_conv_b", "ev_dt_bias", "ev_a_log", "ev_d_skip", "ev_ssm_norm_w", "ev_w_out", "od_w_in",
            "od_q_norm", "od_kv_norm", "od_w_uq", "od_w_ukv", "od_w_o", "ff_w_up", "ff_conv_w", "ff_conv_b", "ff_w_down"]
_BIG = {"ev_w_in": -1, "ev_w_out": -2, "od_w_in": -2, "od_w_uq": -1, "od_w_ukv": -1, "od_w_o": -2,
        "ff_w_up": -1, "ff_w_down": -2}
_SMALL = {"ev_gm_ln_g": -1, "ev_gm_ln_b": -1, "ev_conv_w": -1, "od_q_norm": -1, "od_kv_norm": -1, "ff_conv_w": -1}
_SHARDED = {**_BIG, **_SMALL}
_REPLICATED = [n for n in _WEIGHTS if n not in _SHARDED]
N_CHUNKS = 4


def _from_slabs(slabs, axis):
    t = jnp.moveaxis(slabs, 0, axis - 1)
    shape = list(t.shape)
    if axis == -1:
        return t.reshape(shape[:-2] + [shape[-2] * shape[-1]])
    return t.reshape(shape[:-3] + [shape[-3] * shape[-2], shape[-1]])


def _to_slabs(full, axis):
    shape = list(full.shape)
    if axis == -1:
        t = full.reshape(shape[:-1] + [N_CHIPS, shape[-1] // N_CHIPS])
    else:
        t = full.reshape(shape[:-2] + [N_CHIPS, shape[-2] // N_CHIPS, shape[-1]])
    return jnp.moveaxis(t, axis - 1, 0)


def _reduce_scatter(gs, slot_axes, names, ck):
    theirs = _pair_exchange_multi(gs, slot_axes, name="rs_px")
    ts, owns = [], []
    for g, a, ax, n in zip(gs, theirs, slot_axes, names):
        t, own = _pair_add(g, a, ck, slot_axis=ax, name="rs_pa_" + n)
        ts.append(t)
        owns.append(own)
    bs = _chip_exchange_multi(ts, slot_axes, name="rs_cx")
    finals = [_chip_add(own, b, ck, name="rs_ca_" + n) for own, b, n in zip(owns, bs, names)]
    return _pair_share_multi(finals, name="rs_ps")


def _all_reduce(arrs, tag):
    n = sum(a.size for a in arrs)
    rows = -(-n // PACK_W)
    rows = -(-rows // 8) * 8
    flat = jnp.concatenate([a.astype(F32).reshape(-1) for a in arrs])
    flat = jnp.pad(flat, (0, rows * PACK_W - n)).reshape(rows, PACK_W)
    tot = _all_reduce_small(flat, name=tag).reshape(-1)
    res, off = [], 0
    for a in arrs:
        res.append(tot[off:off + a.size].reshape(a.shape))
        off += a.size
    return res


def _pad_cols(w, cols):
    return jnp.pad(w, ((0, 0), (0, cols - w.shape[1])))


def _as3(a):
    return a.reshape(a.shape[0], 1, a.shape[1]) if a.ndim == 2 else a


def _col_layout(name, f):
    cs = f.shape[3]
    total = N_CHIPS * cs
    if name == "ev_w_in":
        main = 4 * f.shape[2] + 2 * SSM_GROUPS * SSM_STATE
        return _col_pieces(cs, [(0, main, 0, 0), (main, total, 1, 0)]), [main, 128]
    if name == "od_w_uq":
        qk = MLA_NOPE + MLA_ROPE
        heads = total // qk
        return _col_pieces(cs, [(hd * qk, (hd + 1) * qk, 0, hd * 256) for hd in range(heads)]), [heads * 256]
    return _col_pieces(cs, [(0, total, 0, 0)]), [total]


def _rows_of(buf, j):
    return buf[j].reshape(N_CHIPS * buf.shape[2], buf.shape[3])


def _layer_params(full, gathered, layer):
    j = layer // 2
    tag = f"asm{layer}_"
    p = {"norm_mix": full["norm_mix"][layer]}
    if layer % 2 == 0:
        f_in = gathered["ev_w_in"]
        w_main, w_dt = _assemble(f_in, j, *_col_layout("ev_w_in", f_in), name=tag + "in")
        p.update(w_in_main=w_main, w_in_dt=w_dt,
                 gm_ln_g=full["ev_gm_ln_g"][j], gm_ln_b=full["ev_gm_ln_b"][j], gm_ws=full["ev_gm_ws"][j],
                 gm_bs=full["ev_gm_bs"][j], conv_w=full["ev_conv_w"][j], conv_b=full["ev_conv_b"][j],
                 dt_bias=full["ev_dt_bias"][j], a_log=full["ev_a_log"][j], d_skip=full["ev_d_skip"][j],
                 ssm_norm_w=full["ev_ssm_norm_w"][j], w_out=_rows_of(gathered["ev_w_out"], j))
    else:
        f_uq, f_ukv = gathered["od_w_uq"], gathered["od_w_ukv"]
        w_in = _rows_of(gathered["od_w_in"], j)
        p.update(w_in=_pad_cols(w_in, -(-w_in.shape[1] // 128) * 128), q_norm=full["od_q_norm"][j],
                 kv_norm=full["od_kv_norm"][j],
                 w_uq=_assemble(f_uq, j, *_col_layout("od_w_uq", f_uq), name=tag + "uq")[0],
                 w_ukv=_assemble(f_ukv, j, *_col_layout("od_w_ukv", f_ukv), name=tag + "ukv")[0],
                 w_o=_rows_of(gathered["od_w_o"], j))
    f = {"norm_ffn": full["norm_ffn"][layer], "w_up_slabs": (gathered["ff_w_up"], layer),
         "conv_w": full["ff_conv_w"][layer], "conv_b": full["ff_conv_b"][layer],
         "w_down": _rows_of(gathered["ff_w_down"], layer)}
    return p, f


def _small_grads(g, gf, layer):
    out = {"norm_mix": g["norm_mix"], "norm_ffn": gf["norm_ffn"], "ff_conv_w": gf["conv_w"], "ff_conv_b": gf["conv_b"]}
    if layer % 2 == 0:
        out.update(ev_gm_ln_g=g["gm_ln_g"], ev_gm_ln_b=g["gm_ln_b"], ev_gm_ws=g["gm_ws"], ev_gm_bs=g["gm_bs"],
                   ev_conv_w=g["conv_w"], ev_conv_b=g["conv_b"], ev_dt_bias=g["dt_bias"], ev_a_log=g["a_log"],
                   ev_d_skip=g["d_skip"], ev_ssm_norm_w=g["ssm_norm_w"])
    else:
        out.update(od_q_norm=g["q_norm"], od_kv_norm=g["kv_norm"])
    return out


def _step(x, positions, loss_target, w, m, v):
    depth = w["norm_mix"].shape[0]
    h = x[0]
    tgt = loss_target[0]
    cos_p, sin_p = _rope_tables(positions[0])
    ck = jnp.stack([lax.axis_index("c"), 2 * lax.axis_index("x") + lax.axis_index("y")]).astype(jnp.int32)

    sharded = list(_SHARDED)
    slot_axes = [1 if _SHARDED[n] == -2 else 0 for n in sharded]
    bufs = [_cast_place(_as3(w[n]), ck, slot_axis=ax, dtype=BF16 if n in _BIG else F32, name="place_" + n)
            for n, ax in zip(sharded, slot_axes)]
    gathered = dict(zip(sharded, _all_gather_multi(bufs, slot_axes, name="ag")))
    full = {n: w[n] for n in _REPLICATED}
    for n in _SMALL:
        full[n] = _from_slabs(gathered[n], -1).reshape(w[n].shape[:-1] + (N_CHIPS * w[n].shape[-1],))

    params, saved = [], []
    for layer in range(depth):
        p, f = _layer_params(full, gathered, layer)
        if layer % 2 == 0:
            h, sv = _even_fwd(h, p, f"l{layer}m")
        else:
            h, sv = _odd_fwd(h, p, cos_p, sin_p, f"l{layer}m")
        h, svf = _ffn_fwd(h, f, f"l{layer}f")
        params.append((p, f))
        saved.append((sv, svf))

    loss, dh, dnf = _loss_bwd(h, w["norm_final"], tgt, name="loss")
    per_layer = []
    col_buf, row_buf = {}, {}
    row_parts = {"od_w_in": {}}

    def split(name, fulls, j):
        f = gathered[name]
        col_buf[name] = _split(fulls, j, _col_layout(name, f)[0], col_buf.get(name), f.shape, name=f"split_{name}{j}")

    for layer in reversed(range(depth)):
        p, f = params[layer]
        sv, svf = saved[layer]
        j = layer // 2
        dh, gf = _ffn_bwd(dh, f, svf, f"l{layer}fb", col_buf.get("ff_w_up"), (row_buf.get("ff_w_down"), layer, depth))
        col_buf["ff_w_up"] = gf["w_up_slabs"]
        row_buf["ff_w_down"] = gf["w_down"]
        if layer % 2 == 0:
            dh, g = _even_bwd(dh, p, sv, f"l{layer}mb", (row_buf.get("ev_w_out"), j, w["ev_w_out"].shape[0]))
            split("ev_w_in", [g["w_in_main"], g["w_in_dt"]], j)
            row_buf["ev_w_out"] = g["w_out"]
        else:
            dh, g = _odd_bwd(dh, p, cos_p, sin_p, sv, f"l{layer}mb", (row_buf.get("od_w_o"), j, w["od_w_o"].shape[0]))
            split("od_w_uq", [g["w_uq"]], j)
            split("od_w_ukv", [g["w_ukv"]], j)
            row_parts["od_w_in"][j] = g["w_in"][:, :w["od_w_in"].shape[2]]
            row_buf["od_w_o"] = g["w_o"]
        per_layer.append((layer, _small_grads(g, gf, layer)))
    per_layer.sort(key=lambda t: t[0])
    local = {"norm_final": dnf[0]}
    for n in list(_SMALL) + _REPLICATED:
        if n != "norm_final":
            local[n] = jnp.stack([lg[n] for _, lg in per_layer if n in lg], axis=0)

    gs = []
    for n in sharded:
        if n in col_buf:
            gs.append(col_buf[n])
        elif n in row_buf:
            gs.append(row_buf[n].reshape(gathered[n].shape))
        elif n in row_parts:
            parts = row_parts[n]
            gs.append(jnp.stack([parts[i] for i in range(len(parts))], axis=0).reshape(gathered[n].shape))
        else:
            gs.append(_to_slabs(_as3(local[n]), -1))
    grads = dict(zip(sharded, _reduce_scatter(gs, slot_axes, sharded, ck)))
    grads.update(zip(_REPLICATED, _all_reduce([local[n] for n in _REPLICATED], "ar")))
    loss = lax.psum(loss[0, 0], ("x", "y", "c"))

    delta, new_m, new_v = {}, {}, {}
    for n in _WEIGHTS:
        grads[n] = grads[n].reshape(w[n].shape)
        grads[n], delta[n], new_m[n], new_v[n] = _adamw(w[n], grads[n], m[n], v[n], name="adamw_" + n)
    return (loss, dh[None], *[grads[n] for n in _WEIGHTS], *[delta[n] for n in _WEIGHTS],
            *[new_m[n] for n in _WEIGHTS], *[new_v[n] for n in _WEIGHTS])


def kernel(x, positions, norm_mix, norm_ffn, norm_final, ev_w_in, ev_gm_ln_g, ev_gm_ln_b, ev_gm_ws, ev_gm_bs, ev_conv_w, ev_conv_b, ev_dt_bias, ev_a_log, ev_d_skip, ev_ssm_norm_w, ev_w_out, od_w_in, od_q_norm, od_kv_norm, od_w_uq, od_w_ukv, od_w_o, ff_w_up, ff_conv_w, ff_conv_b, ff_w_down, loss_target, m_norm_mix, m_norm_ffn, m_norm_final, m_ev_w_in, m_ev_gm_ln_g, m_ev_gm_ln_b, m_ev_gm_ws, m_ev_gm_bs, m_ev_conv_w, m_ev_conv_b, m_ev_dt_bias, m_ev_a_log, m_ev_d_skip, m_ev_ssm_norm_w, m_ev_w_out, m_od_w_in, m_od_q_norm, m_od_kv_norm, m_od_w_uq, m_od_w_ukv, m_od_w_o, m_ff_w_up, m_ff_conv_w, m_ff_conv_b, m_ff_w_down, v_norm_mix, v_norm_ffn, v_norm_final, v_ev_w_in, v_ev_gm_ln_g, v_ev_gm_ln_b, v_ev_gm_ws, v_ev_gm_bs, v_ev_conv_w, v_ev_conv_b, v_ev_dt_bias, v_ev_a_log, v_ev_d_skip, v_ev_ssm_norm_w, v_ev_w_out, v_od_w_in, v_od_q_norm, v_od_kv_norm, v_od_w_uq, v_od_w_ukv, v_od_w_o, v_ff_w_up, v_ff_conv_w, v_ff_conv_b, v_ff_w_down):
    ws = (norm_mix, norm_ffn, norm_final, ev_w_in, ev_gm_ln_g, ev_gm_ln_b, ev_gm_ws, ev_gm_bs, ev_conv_w, ev_conv_b, ev_dt_bias, ev_a_log, ev_d_skip, ev_ssm_norm_w, ev_w_out, od_w_in, od_q_norm, od_kv_norm, od_w_uq, od_w_ukv, od_w_o, ff_w_up, ff_conv_w, ff_conv_b, ff_w_down)
    ms = (m_norm_mix, m_norm_ffn, m_norm_final, m_ev_w_in, m_ev_gm_ln_g, m_ev_gm_ln_b, m_ev_gm_ws, m_ev_gm_bs, m_ev_conv_w, m_ev_conv_b, m_ev_dt_bias, m_ev_a_log, m_ev_d_skip, m_ev_ssm_norm_w, m_ev_w_out, m_od_w_in, m_od_q_norm, m_od_kv_norm, m_od_w_uq, m_od_w_ukv, m_od_w_o, m_ff_w_up, m_ff_conv_w, m_ff_conv_b, m_ff_w_down)
    vs = (v_norm_mix, v_norm_ffn, v_norm_final, v_ev_w_in, v_ev_gm_ln_g, v_ev_gm_ln_b, v_ev_gm_ws, v_ev_gm_bs, v_ev_conv_w, v_ev_conv_b, v_ev_dt_bias, v_ev_a_log, v_ev_d_skip, v_ev_ssm_norm_w, v_ev_w_out, v_od_w_in, v_od_q_norm, v_od_kv_norm, v_od_w_uq, v_od_w_ukv, v_od_w_o, v_ff_w_up, v_ff_conv_w, v_ff_conv_b, v_ff_w_down)
    return _step(x, positions, loss_target, dict(zip(_WEIGHTS, ws)), dict(zip(_WEIGHTS, ms)), dict(zip(_WEIGHTS, vs)))
```

```python
import functools
import math

import jax
import jax.numpy as jnp
from jax import lax
from jax.experimental import pallas as pl
from jax.experimental.pallas import tpu as pltpu

F32 = jnp.float32
BF16 = jnp.bfloat16
EPS = 1e-6
CHUNK = 64
BLK = 128
GM_GROUPS = 8
SSM_HEAD_DIM = 64
SSM_GROUPS = 4
SSM_STATE = 128
SSM_CONV = 4
FFN_CONV = 3
MLA_NOPE = 128
MLA_ROPE = 64
MLA_V = 128
ROPE_THETA = 10000.0
V7X_VMEM_LIMIT = 56 * 1024 * 1024
HI = lax.Precision.HIGHEST

ADAM_LR = 0.001
ADAM_B1 = 0.9
ADAM_B2 = 0.999
ADAM_EPS = 1e-08
ADAM_WD = 0.01
ADAM_STEP = 10


def _cp(*sem):
    return pltpu.CompilerParams(dimension_semantics=sem if sem else None, vmem_limit_bytes=V7X_VMEM_LIMIT)


def _tile(n, cands):
    for c in cands:
        if n % c == 0:
            return c
    return n


def _row(v):
    return v.reshape(1, -1).astype(F32)


_MM_TILES = (1536, 1408, 1280, 1152, 1024, 896, 768, 640, 512, 384, 256, 128)
_MM_VMEM_BUDGET = 40 * 1024 * 1024


def _mm(a, b, *, ta=False, tb=False, out_dtype=F32, residual=None, b_layer=None, out_slabs=None, out_layer=None, name):
    m, k = (a.shape[1], a.shape[0]) if ta else a.shape
    if b_layer is None:
        n = b.shape[0] if tb else b.shape[1]
        assert k == (b.shape[1] if tb else b.shape[0]), (a.shape, b.shape, ta, tb)
        n_unit = k_unit = None
    else:
        _, _, rows, cs = b.shape
        n = rows if tb else N_CHIPS * cs
        assert k == (N_CHIPS * cs if tb else rows), (a.shape, b.shape, ta, tb)
        n_unit, k_unit = (None, cs) if tb else (cs, None)
    if out_slabs is not None:
        n_unit = out_slabs[2][3]
        assert n == N_CHIPS * n_unit and m == out_slabs[2][2]
    tm = _tile(m, (1024, 512, 256, 128))
    tn = _tile(n_unit or n, _MM_TILES)
    tk = k if (k_unit is None and k <= 2048) else _tile(k_unit or k, (1408, 1024, 768, 512, 384, 256, 128))
    nk = k // tk

    def vmem(tm_):
        bytes_ = 2 * (tm_ * tk * a.dtype.itemsize + tk * tn * b.dtype.itemsize + tm_ * tn * jnp.dtype(out_dtype).itemsize)
        bytes_ += (tm_ * tn * 4 if nk > 1 else 0) + (2 * tm_ * tn * residual.dtype.itemsize if residual is not None else 0)
        return bytes_

    while vmem(tm) > _MM_VMEM_BUDGET and tm % 256 == 0:
        tm //= 2
    dn = (((0 if ta else 1,), (1 if tb else 0,)), ((), ()))
    has_res = residual is not None

    def body(*refs):
        a_ref, b_ref = refs[0], refs[1]
        r_ref = refs[2] if has_res else None
        o_ref, acc = refs[-2], refs[-1]
        kk = pl.program_id(2)
        part = lax.dot_general(a_ref[...].astype(BF16), b_ref[...].astype(BF16), dn, preferred_element_type=F32)

        def finish(r):
            if has_res:
                r = r + r_ref[...].astype(F32)
            o_ref[...] = r.astype(out_dtype)

        if nk == 1:
            finish(part)
            return

        @pl.when(kk == 0)
        def _():
            acc[...] = part

        @pl.when(jnp.logical_and(kk > 0, kk < nk - 1))
        def _():
            acc[...] += part

        @pl.when(kk == nk - 1)
        def _():
            finish(acc[...] + part)

    a_spec = pl.BlockSpec((tk, tm), lambda i, j, kk: (kk, i)) if ta else pl.BlockSpec((tm, tk), lambda i, j, kk: (i, kk))
    if b_layer is None:
        b_spec = pl.BlockSpec((tn, tk), lambda i, j, kk: (j, kk)) if tb else pl.BlockSpec((tk, tn), lambda i, j, kk: (kk, j))
    elif tb:
        per = k_unit // tk
        b_spec = pl.BlockSpec((None, None, tn, tk), lambda i, j, kk: (kk // per, b_layer, j, kk % per))
    else:
        per = n_unit // tn
        b_spec = pl.BlockSpec((None, None, tk, tn), lambda i, j, kk: (j // per, b_layer, kk, j % per))
    in_specs = [a_spec, b_spec]
    args = [a, b]
    if has_res:
        in_specs.append(pl.BlockSpec((tm, tn), lambda i, j, kk: (i, j)))
        args.append(residual)
    aliases = {}
    if out_layer is not None:
        g_prev, layer, n_layers = out_layer
        out_spec = pl.BlockSpec((None, tm, tn), lambda i, j, kk: (layer, i, j))
        out_shape = jax.ShapeDtypeStruct((n_layers, m, n), out_dtype)
        if g_prev is not None:
            in_specs.append(_ANY)
            args.append(g_prev)
            aliases = {len(args) - 1: 0}
    elif out_slabs is None:
        out_spec = pl.BlockSpec((tm, tn), lambda i, j, kk: (i, j))
        out_shape = jax.ShapeDtypeStruct((m, n), out_dtype)
    else:
        g_prev, layer, shape = out_slabs
        per_o = n_unit // tn
        out_spec = pl.BlockSpec((None, None, tm, tn), lambda i, j, kk: (j // per_o, layer, i, j % per_o))
        out_shape = jax.ShapeDtypeStruct(shape, out_dtype)
        if g_prev is not None:
            in_specs.append(_ANY)
            args.append(g_prev)
            aliases = {len(args) - 1: 0}
    return pl.pallas_call(
        body, name=name,
        grid=(m // tm, n // tn, nk),
        in_specs=in_specs,
        out_specs=out_spec,
        out_shape=out_shape,
        input_output_aliases=aliases,
        scratch_shapes=[pltpu.VMEM((tm, tn) if nk > 1 else (8, 128), F32)],
        compiler_params=_cp("parallel", "parallel", "arbitrary"),
    )(*args)


def _rms_fwd(x, w, *, width=None, col=0, out_dtype=None, name):
    out_dtype = out_dtype or BF16
    s = x.shape[0]
    width = width or x.shape[1]
    tr = _tile(s, (256, 128))

    def body(x_ref, w_ref, o_ref):
        xv = x_ref[...]
        r = lax.rsqrt(jnp.mean(xv * xv, axis=-1, keepdims=True) + EPS)
        o_ref[...] = (xv * r * w_ref[...]).astype(out_dtype)

    return pl.pallas_call(
        body, name=name, grid=(s // tr,),
        in_specs=[pl.BlockSpec((tr, width), lambda i: (i, col)), pl.BlockSpec((1, width), lambda i: (0, 0))],
        out_specs=pl.BlockSpec((tr, width), lambda i: (i, 0)),
        out_shape=jax.ShapeDtypeStruct((s, width), out_dtype),
        compiler_params=_cp("parallel"),
    )(x, _row(w))


def _rms_bwd(x, w, dy, *, add=None, width=None, col=0, dy_col=0, out_dtype=F32, name):
    s = x.shape[0]
    width = width or x.shape[1]
    tr = _tile(s, (256, 128))
    has_add = add is not None

    def body(*refs):
        if has_add:
            x_ref, w_ref, dy_ref, add_ref, dx_ref, dw_ref = refs
        else:
            x_ref, w_ref, dy_ref, dx_ref, dw_ref = refs
        xv = x_ref[...]
        dyv = dy_ref[...].astype(F32)
        r = lax.rsqrt(jnp.mean(xv * xv, axis=-1, keepdims=True) + EPS)
        xh = xv * r
        g = dyv * w_ref[...]
        dx = r * (g - xh * jnp.mean(g * xh, axis=-1, keepdims=True))
        if has_add:
            dx = dx + add_ref[...]
        dx_ref[...] = dx.astype(out_dtype)

        @pl.when(pl.program_id(0) == 0)
        def _():
            dw_ref[...] = jnp.zeros_like(dw_ref)

        dw_ref[...] += jnp.sum(dyv * xh, axis=0, keepdims=True)

    in_specs = [pl.BlockSpec((tr, width), lambda i: (i, col)), pl.BlockSpec((1, width), lambda i: (0, 0)),
                pl.BlockSpec((tr, width), lambda i: (i, dy_col))]
    args = [x, _row(w), dy]
    if has_add:
        in_specs.append(pl.BlockSpec((tr, width), lambda i: (i, 0)))
        args.append(add)
    return pl.pallas_call(
        body, name=name, grid=(s // tr,),
        in_specs=in_specs,
        out_specs=[pl.BlockSpec((tr, width), lambda i: (i, 0)), pl.BlockSpec((1, width), lambda i: (0, 0))],
        out_shape=[jax.ShapeDtypeStruct((s, width), out_dtype), jax.ShapeDtypeStruct((1, width), F32)],
        compiler_params=_cp("arbitrary"),
    )(*args)


def _loss_bwd(h, w, tgt, *, name):
    s, d = h.shape
    tr = _tile(s, (256, 128))

    def body(x_ref, w_ref, t_ref, loss_ref, dx_ref, dw_ref):
        xv = x_ref[...]
        r = lax.rsqrt(jnp.mean(xv * xv, axis=-1, keepdims=True) + EPS)
        xh = xv * r
        e = xh * w_ref[...] - t_ref[...]
        part = 0.5 * jnp.sum(jnp.mean(e * e, axis=-1, keepdims=True), axis=0, keepdims=True)
        dyv = e * (1.0 / d)
        g = dyv * w_ref[...]
        dx_ref[...] = r * (g - xh * jnp.mean(g * xh, axis=-1, keepdims=True))

        @pl.when(pl.program_id(0) == 0)
        def _():
            dw_ref[...] = jnp.zeros_like(dw_ref)
            loss_ref[...] = jnp.zeros_like(loss_ref)

        dw_ref[...] += jnp.sum(dyv * xh, axis=0, keepdims=True)
        loss_ref[...] += jnp.broadcast_to(part, loss_ref.shape)

    return pl.pallas_call(
        body, name=name, grid=(s // tr,),
        in_specs=[pl.BlockSpec((tr, d), lambda i: (i, 0)), pl.BlockSpec((1, d), lambda i: (0, 0)),
                  pl.BlockSpec((tr, d), lambda i: (i, 0))],
        out_specs=[pl.BlockSpec((1, 128), lambda i: (0, 0)), pl.BlockSpec((tr, d), lambda i: (i, 0)),
                   pl.BlockSpec((1, d), lambda i: (0, 0))],
        out_shape=[jax.ShapeDtypeStruct((1, 128), F32), jax.ShapeDtypeStruct((s, d), F32),
                   jax.ShapeDtypeStruct((1, d), F32)],
        compiler_params=_cp("arbitrary"),
    )(h, _row(w), tgt)


_G0 = math.sqrt(2.0 / math.pi)
_G1 = 0.044715


def _gelu(x):
    return 0.5 * x * (1.0 + jnp.tanh(_G0 * (x + _G1 * x * x * x)))


def _gelu_and_grad(x):
    th = jnp.tanh(_G0 * (x + _G1 * x * x * x))
    val = 0.5 * x * (1.0 + th)
    grad = 0.5 * (1.0 + th) + 0.5 * x * (1.0 - th * th) * _G0 * (1.0 + 3.0 * _G1 * x * x)
    return val, grad


def _sigmoid(x):
    return 1.0 / (1.0 + jnp.exp(-x))


def _shift_down(x, k):
    if k == 0:
        return x
    rows = lax.broadcasted_iota(jnp.int32, x.shape, 0)
    return jnp.where(rows >= k, pltpu.roll(x, k, 0), 0.0)


def _shift_up(x, k):
    if k == 0:
        return x
    n = x.shape[0]
    rows = lax.broadcasted_iota(jnp.int32, x.shape, 0)
    return jnp.where(rows < n - k, pltpu.roll(x, n - k, 0), 0.0)


def _conv_rows(x, w_ref, b_ref, kw):
    y = b_ref[...] + w_ref[kw - 1:kw, :] * x
    for k in range(kw - 1):
        y = y + w_ref[k:k + 1, :] * _shift_down(x, kw - 1 - k)
    return y


def _conv_rows_bwd(x, dgc, w_ref, dw_ref, db_ref, kw):
    dx = w_ref[kw - 1:kw, :] * dgc
    dw_ref[kw - 1:kw, :] = jnp.sum(dgc * x, axis=0, keepdims=True)
    for k in range(kw - 1):
        sh = kw - 1 - k
        dx = dx + w_ref[k:k + 1, :] * _shift_up(dgc, sh)
        dw_ref[k:k + 1, :] = jnp.sum(dgc * _shift_down(x, sh), axis=0, keepdims=True)
    db_ref[...] = jnp.sum(dgc, axis=0, keepdims=True)
    return dx


def _ffn_mid_fwd(up, conv_w, conv_b, *, name):
    s = up.shape[0]
    f = up.shape[1] // 2
    tc = _tile(f, (256, 128))
    nf = f // tc

    def body(g_ref, v_ref, w_ref, b_ref, o_ref):
        gc = _conv_rows(g_ref[...], w_ref, b_ref, FFN_CONV)
        o_ref[...] = (_gelu(gc) * v_ref[...]).astype(BF16)

    return pl.pallas_call(
        body, name=name, grid=(nf,),
        in_specs=[pl.BlockSpec((s, tc), lambda j: (0, j)), pl.BlockSpec((s, tc), lambda j: (0, j + nf)),
                  pl.BlockSpec((FFN_CONV, tc), lambda j: (0, j)), pl.BlockSpec((1, tc), lambda j: (0, j))],
        out_specs=pl.BlockSpec((s, tc), lambda j: (0, j)),
        out_shape=jax.ShapeDtypeStruct((s, f), BF16),
        compiler_params=_cp("parallel"),
    )(up, up, conv_w, _row(conv_b))


def _ffn_mid_bwd(up, conv_w, conv_b, da, *, name):
    s = up.shape[0]
    f = up.shape[1] // 2
    tc = _tile(f, (256, 128))
    nf = f // tc

    def body(g_ref, v_ref, w_ref, b_ref, da_ref, dg_ref, dv_ref, dw_ref, db_ref):
        g = g_ref[...]
        gc = _conv_rows(g, w_ref, b_ref, FFN_CONV)
        gel, dgel = _gelu_and_grad(gc)
        dav = da_ref[...]
        dv_ref[...] = (dav * gel).astype(BF16)
        dgc = dav * v_ref[...] * dgel
        dg_ref[...] = _conv_rows_bwd(g, dgc, w_ref, dw_ref, db_ref, FFN_CONV).astype(BF16)

    col = lambda j: (0, j)
    return pl.pallas_call(
        body, name=name, grid=(nf,),
        in_specs=[pl.BlockSpec((s, tc), col), pl.BlockSpec((s, tc), lambda j: (0, j + nf)),
                  pl.BlockSpec((FFN_CONV, tc), col), pl.BlockSpec((1, tc), col), pl.BlockSpec((s, tc), col)],
        out_specs=[pl.BlockSpec((s, tc), col), pl.BlockSpec((s, tc), col),
                   pl.BlockSpec((FFN_CONV, tc), col), pl.BlockSpec((1, tc), col)],
        out_shape=[jax.ShapeDtypeStruct((s, f), BF16), jax.ShapeDtypeStruct((s, f), BF16),
                   jax.ShapeDtypeStruct((FFN_CONV, f), F32), jax.ShapeDtypeStruct((1, f), F32)],
        compiler_params=_cp("parallel"),
    )(up, up, conv_w, _row(conv_b), da)


def _gm_mask():
    r = lax.broadcasted_iota(jnp.int32, (BLK, BLK), 0) // CHUNK
    c = lax.broadcasted_iota(jnp.int32, (BLK, BLK), 1) // CHUNK
    return r >= c


def _gm_specs(s, gd):
    nb = s // BLK
    u_spec = pl.BlockSpec((BLK, gd), lambda g, n: (n, g))
    v_spec = pl.BlockSpec((BLK, gd), lambda g, n: (n, g + GM_GROUPS))
    vec_spec = pl.BlockSpec((1, gd), lambda g, n: (0, g))
    ws_spec = pl.BlockSpec((1, BLK, BLK), lambda g, n: (g, 0, 0))
    bs_spec = pl.BlockSpec((1, BLK, 1), lambda g, n: (g, 0, 0))
    return nb, u_spec, v_spec, vec_spec, ws_spec, bs_spec


def _gm_fwd(proj, ln_g, ln_b, ws, bs, *, name):
    s = proj.shape[0]
    gd = ln_g.shape[-1]
    w = GM_GROUPS * gd
    nb, u_spec, v_spec, vec_spec, ws_spec, bs_spec = _gm_specs(s, gd)

    def body(u_ref, v_ref, lg_ref, lb_ref, ws_ref, bs_ref, o_ref):
        ua = _gelu(u_ref[...])
        va = _gelu(v_ref[...])
        mu = jnp.mean(va, axis=-1, keepdims=True)
        vc = va - mu
        var = jnp.mean(vc * vc, axis=-1, keepdims=True)
        vn = vc * lax.rsqrt(var + EPS) * lg_ref[...] + lb_ref[...]
        wm = jnp.where(_gm_mask(), ws_ref[0], 0.0).astype(BF16)
        gate = jnp.dot(wm, vn.astype(BF16), preferred_element_type=F32) + bs_ref[0]
        o_ref[...] = (ua * gate).astype(BF16)

    return pl.pallas_call(
        body, name=name, grid=(GM_GROUPS, nb),
        in_specs=[u_spec, v_spec, vec_spec, vec_spec, ws_spec, bs_spec],
        out_specs=pl.BlockSpec((BLK, gd), lambda g, n: (n, g)),
        out_shape=jax.ShapeDtypeStruct((s, w), BF16),
        compiler_params=_cp("parallel", "parallel"),
    )(proj, proj, ln_g.reshape(1, w), ln_b.reshape(1, w), ws, bs.reshape(GM_GROUPS, BLK, 1))


def _gm_bwd(proj, ln_g, ln_b, ws, bs, dya, *, name):
    s = proj.shape[0]
    gd = ln_g.shape[-1]
    w = GM_GROUPS * gd
    nb, u_spec, v_spec, vec_spec, ws_spec, bs_spec = _gm_specs(s, gd)

    def body(u_ref, v_ref, lg_ref, lb_ref, ws_ref, bs_ref, dy_ref, du_ref, dv_ref, dlg_ref, dlb_ref, dws_ref, dbs_ref):
        ua, dua_du = _gelu_and_grad(u_ref[...])
        va, dva_dv = _gelu_and_grad(v_ref[...])
        mu = jnp.mean(va, axis=-1, keepdims=True)
        vc = va - mu
        var = jnp.mean(vc * vc, axis=-1, keepdims=True)
        rstd = lax.rsqrt(var + EPS)
        xh = vc * rstd
        vn = (xh * lg_ref[...] + lb_ref[...]).astype(BF16)
        mask = _gm_mask()
        wm = jnp.where(mask, ws_ref[0], 0.0).astype(BF16)
        gate = jnp.dot(wm, vn, preferred_element_type=F32) + bs_ref[0]
        dy = dy_ref[...]
        du_ref[...] = (dy * gate * dua_du).astype(BF16)
        dgate = dy * ua
        dgb = dgate.astype(BF16)
        dwm = lax.dot_general(dgb, vn, (((1,), (1,)), ((), ())), preferred_element_type=F32)
        dvn = lax.dot_general(wm, dgb, (((0,), (0,)), ((), ())), preferred_element_type=F32)
        dxh = dvn * lg_ref[...]
        dva = rstd * (dxh - jnp.mean(dxh, axis=-1, keepdims=True) - xh * jnp.mean(dxh * xh, axis=-1, keepdims=True))
        dv_ref[...] = (dva * dva_dv).astype(BF16)

        @pl.when(pl.program_id(1) == 0)
        def _():
            dlg_ref[...] = jnp.zeros_like(dlg_ref)
            dlb_ref[...] = jnp.zeros_like(dlb_ref)
            dws_ref[...] = jnp.zeros_like(dws_ref)
            dbs_ref[...] = jnp.zeros_like(dbs_ref)

        dlg_ref[...] += jnp.sum(dvn * xh, axis=0, keepdims=True)
        dlb_ref[...] += jnp.sum(dvn, axis=0, keepdims=True)
        dws_ref[0] += jnp.where(mask, dwm, 0.0)
        dbs_ref[0] += jnp.sum(dgate, axis=-1, keepdims=True)

    out_uv = pl.BlockSpec((BLK, gd), lambda g, n: (n, g))
    return pl.pallas_call(
        body, name=name, grid=(GM_GROUPS, nb),
        in_specs=[u_spec, v_spec, vec_spec, vec_spec, ws_spec, bs_spec, pl.BlockSpec((BLK, gd), lambda g, n: (n, g))],
        out_specs=[out_uv, out_uv, vec_spec, vec_spec, ws_spec, bs_spec],
        out_shape=[jax.ShapeDtypeStruct((s, w), BF16), jax.ShapeDtypeStruct((s, w), BF16),
                   jax.ShapeDtypeStruct((1, w), F32), jax.ShapeDtypeStruct((1, w), F32),
                   jax.ShapeDtypeStruct((GM_GROUPS, BLK, BLK), F32), jax.ShapeDtypeStruct((GM_GROUPS, BLK, 1), F32)],
        compiler_params=_cp("parallel", "arbitrary"),
    )(proj, proj, ln_g.reshape(1, w), ln_b.reshape(1, w), ws, bs.reshape(GM_GROUPS, BLK, 1), dya)


def _silu_conv_fwd(proj, conv_w, conv_b, *, col0, name):
    s = proj.shape[0]
    c = conv_w.shape[1]
    tc = _tile(c, (256, 128))
    off = col0 // tc

    def body(x_ref, w_ref, b_ref, o_ref):
        y = _conv_rows(x_ref[...], w_ref, b_ref, SSM_CONV)
        o_ref[...] = y * _sigmoid(y)

    col = lambda j: (0, j)
    return pl.pallas_call(
        body, name=name, grid=(c // tc,),
        in_specs=[pl.BlockSpec((s, tc), lambda j: (0, j + off)), pl.BlockSpec((SSM_CONV, tc), col), pl.BlockSpec((1, tc), col)],
        out_specs=pl.BlockSpec((s, tc), col),
        out_shape=jax.ShapeDtypeStruct((s, c), F32),
        compiler_params=_cp("parallel"),
    )(proj, conv_w, _row(conv_b))


def _silu_conv_bwd(proj, conv_w, conv_b, dact, *, col0, name):
    s = proj.shape[0]
    c = conv_w.shape[1]
    tc = _tile(c, (256, 128))
    off = col0 // tc

    def body(x_ref, w_ref, b_ref, d_ref, dx_ref, dw_ref, db_ref):
        x = x_ref[...]
        y = _conv_rows(x, w_ref, b_ref, SSM_CONV)
        sg = _sigmoid(y)
        dgc = d_ref[...] * sg * (1.0 + y * (1.0 - sg))
        dx_ref[...] = _conv_rows_bwd(x, dgc, w_ref, dw_ref, db_ref, SSM_CONV).astype(BF16)

    col = lambda j: (0, j)
    return pl.pallas_call(
        body, name=name, grid=(c // tc,),
        in_specs=[pl.BlockSpec((s, tc), lambda j: (0, j + off)), pl.BlockSpec((SSM_CONV, tc), col), pl.BlockSpec((1, tc), col),
                  pl.BlockSpec((s, tc), col)],
        out_specs=[pl.BlockSpec((s, tc), col), pl.BlockSpec((SSM_CONV, tc), col), pl.BlockSpec((1, tc), col)],
        out_shape=[jax.ShapeDtypeStruct((s, c), BF16), jax.ShapeDtypeStruct((SSM_CONV, c), F32),
                   jax.ShapeDtypeStruct((1, c), F32)],
        compiler_params=_cp("parallel"),
    )(proj, conv_w, _row(conv_b), dact)


def _head_select(heads):
    r = lax.broadcasted_iota(jnp.int32, (128, heads * SSM_HEAD_DIM), 0)
    c = lax.broadcasted_iota(jnp.int32, (128, heads * SSM_HEAD_DIM), 1) // SSM_HEAD_DIM
    return (r == c).astype(F32)


def _dt_fwd(dt_raw, dt_bias, *, heads, name):
    s = dt_raw.shape[0]
    d = heads * SSM_HEAD_DIM
    tr = _tile(s, (256, 128))

    def body(x_ref, b_ref, o_ref):
        pre = jnp.dot(x_ref[...] + b_ref[...], _head_select(heads), precision=HI, preferred_element_type=F32)
        o_ref[...] = jax.nn.softplus(pre)

    return pl.pallas_call(
        body, name=name, grid=(s // tr,),
        in_specs=[pl.BlockSpec((tr, 128), lambda i: (i, 0)), pl.BlockSpec((1, 128), lambda i: (0, 0))],
        out_specs=pl.BlockSpec((tr, d), lambda i: (i, 0)),
        out_shape=jax.ShapeDtypeStruct((s, d), F32),
        compiler_params=_cp("parallel"),
    )(dt_raw, dt_bias)


def _dt_bwd(dt_raw, dt_bias, zt, da_lane, dd_lane, a_row, *, heads, name):
    s = dt_raw.shape[0]
    d = heads * SSM_HEAD_DIM
    tr = _tile(s, (256, 128))
    nt = (((1,), (1,)), ((), ()))

    def body(x_ref, b_ref, z_ref, da_ref, dd_ref, a_ref, o_ref, db_ref, dal_ref, dds_ref):
        sel = _head_select(heads)
        ddt = lax.dot_general(z_ref[...], sel, nt, precision=HI, preferred_element_type=F32)
        g = ddt * _sigmoid(x_ref[...] + b_ref[...])
        o_ref[...] = g.astype(BF16)

        @pl.when(pl.program_id(0) == 0)
        def _():
            db_ref[...] = jnp.zeros_like(db_ref)
            da = lax.dot_general(da_ref[...], sel, nt, precision=HI, preferred_element_type=F32)
            dal_ref[...] = da * a_ref[...]
            dds_ref[...] = lax.dot_general(dd_ref[...], sel, nt, precision=HI, preferred_element_type=F32)

        db_ref[...] += jnp.sum(g, axis=0, keepdims=True)

    vec = pl.BlockSpec((1, 128), lambda i: (0, 0))
    lane = pl.BlockSpec((1, d), lambda i: (0, 0))
    return pl.pallas_call(
        body, name=name, grid=(s // tr,),
        in_specs=[pl.BlockSpec((tr, 128), lambda i: (i, 0)), vec, pl.BlockSpec((tr, d), lambda i: (i, 0)), lane, lane, vec],
        out_specs=[pl.BlockSpec((tr, 128), lambda i: (i, 0)), vec, vec, vec],
        out_shape=[jax.ShapeDtypeStruct((s, 128), BF16)] + [jax.ShapeDtypeStruct((1, 128), F32)] * 3,
        compiler_params=_cp("arbitrary"),
    )(dt_raw, dt_bias, zt, da_lane, dd_lane, a_row)


_NT = (((1,), (1,)), ((), ()))
_TN = (((0,), (0,)), ((), ()))


def _bdot(a, b, dn=None):
    if dn is None:
        return jnp.dot(a, b, preferred_element_type=F32)
    return lax.dot_general(a, b, dn, preferred_element_type=F32)


def _ssd_common(x_ref, b_ref, c_ref, dt_ref, a_ref):
    x = x_ref[...]
    dt = dt_ref[...]
    rows = lax.broadcasted_iota(jnp.int32, (BLK, BLK), 0)
    cols = lax.broadcasted_iota(jnp.int32, (BLK, BLK), 1)
    tl = (rows >= cols).astype(F32)
    acum = jnp.dot(tl, dt * a_ref[...], precision=HI, preferred_element_type=F32)
    alast = acum[BLK - 1:BLK, :]
    bm = b_ref[...].astype(BF16)
    cm = c_ref[...].astype(BF16)
    cb = _bdot(cm, bm, _NT)
    return x, dt, rows, cols, acum, alast, bm, cm, cb


def _ssd_decay(ap, apt, e, low):
    acol = ap[:, e * SSM_HEAD_DIM:e * SSM_HEAD_DIM + 1]
    arow = apt[e * SSM_HEAD_DIM:e * SSM_HEAD_DIM + 1, :]
    return jnp.where(low, jnp.exp(jnp.minimum(acol - arow, 0.0)), 0.0)


def _ssd_specs(s, d):
    gw = d // SSM_GROUPS
    bcol = d // SSM_STATE
    return gw, bcol


def _ssd_fwd(act, dte, a_lane, d_lane, *, name):
    s = act.shape[0]
    d = dte.shape[1]
    gw, bcol = _ssd_specs(s, d)
    npair = gw // 128
    nc = s // BLK

    def body(x_ref, b_ref, c_ref, dt_ref, a_ref, dsk_ref, y_ref, st_ref, ht):
        @pl.when(pl.program_id(1) == 0)
        def _():
            ht[...] = jnp.zeros_like(ht)

        x, dt, rows, cols, acum, alast, bm, cm, cb = _ssd_common(x_ref, b_ref, c_ref, dt_ref, a_ref)
        low = rows >= cols
        first = cols < SSM_HEAD_DIM
        xd = x * dt
        h_in = ht[...]
        st_ref[0] = h_in
        yoff = _bdot(cm, h_in.astype(BF16)) * jnp.exp(acum)
        parts = []
        for p in range(npair):
            ap = acum[:, p * 128:(p + 1) * 128]
            apt = ap.T
            xdp = xd[:, p * 128:(p + 1) * 128].astype(BF16)
            ys = [_bdot((cb * _ssd_decay(ap, apt, e, low)).astype(BF16), xdp) for e in range(2)]
            parts.append(jnp.where(first, ys[0], ys[1]))
        ydiag = parts[0] if npair == 1 else jnp.concatenate(parts, axis=1)
        y_ref[...] = ydiag + yoff + dsk_ref[...] * x
        w = (xd * jnp.exp(alast - acum)).astype(BF16)
        ht[...] = h_in * jnp.exp(alast) + _bdot(bm, w, _TN)

    blk = lambda g, c: (c, g)
    vec = pl.BlockSpec((1, gw), lambda g, c: (0, g))
    return pl.pallas_call(
        body, name=name, grid=(SSM_GROUPS, nc),
        in_specs=[pl.BlockSpec((BLK, gw), blk),
                  pl.BlockSpec((BLK, SSM_STATE), lambda g, c: (c, bcol + g)),
                  pl.BlockSpec((BLK, SSM_STATE), lambda g, c: (c, bcol + SSM_GROUPS + g)),
                  pl.BlockSpec((BLK, gw), blk), vec, vec],
        out_specs=[pl.BlockSpec((BLK, gw), blk), pl.BlockSpec((1, SSM_STATE, gw), lambda g, c: (c, 0, g))],
        out_shape=[jax.ShapeDtypeStruct((s, d), F32), jax.ShapeDtypeStruct((nc, SSM_STATE, d), F32)],
        scratch_shapes=[pltpu.VMEM((SSM_STATE, gw), F32)],
        compiler_params=_cp("parallel", "arbitrary"),
    )(act, act, act, dte, a_lane, d_lane)


def _ssd_bwd(act, dte, a_lane, d_lane, states, dy, *, name):
    s = act.shape[0]
    d = dte.shape[1]
    gw, bcol = _ssd_specs(s, d)
    npair = gw // 128
    nc = s // BLK
    gn = SSM_GROUPS * SSM_STATE

    def body(x_ref, b_ref, c_ref, dt_ref, a_ref, dsk_ref, st_ref, dy_ref,
             dx_ref, db_ref, dc_ref, zt_ref, dal_ref, ddl_ref, dht):
        @pl.when(pl.program_id(1) == 0)
        def _():
            dht[...] = jnp.zeros_like(dht)
            dal_ref[...] = jnp.zeros_like(dal_ref)
            ddl_ref[...] = jnp.zeros_like(ddl_ref)

        x, dt, rows, cols, acum, alast, bm, cm, cb = _ssd_common(x_ref, b_ref, c_ref, dt_ref, a_ref)
        low = rows >= cols
        first = cols < SSM_HEAD_DIM
        a = a_ref[...]
        xd = x * dt
        ea = jnp.exp(acum)
        wdec = jnp.exp(alast - acum)
        el = jnp.exp(alast)
        h_in = st_ref[0]
        hb = h_in.astype(BF16)
        g = dy_ref[...]
        dh = dht[...]
        dhb = dh.astype(BF16)

        yoff = _bdot(cm, hb) * ea
        geb = (g * ea).astype(BF16)
        dc = _bdot(geb, hb, _NT)
        u = _bdot(bm, dhb)
        wx = xd * wdec
        db = _bdot(wx.astype(BF16), dhb, _NT)
        dxd = wdec * u
        xwu = wx * u
        da_l = g * yoff - xwu
        dalast = jnp.sum(xwu, axis=0, keepdims=True) + el * jnp.sum(dh * h_in, axis=0, keepdims=True)
        dht[...] = dh * el + _bdot(cm, geb, _TN)

        dcb = jnp.zeros((BLK, BLK), F32)
        dxd_parts, col_parts = [], []
        for p in range(npair):
            ap = acum[:, p * 128:(p + 1) * 128]
            apt = ap.T
            xdp = xd[:, p * 128:(p + 1) * 128].astype(BF16)
            gp = g[:, p * 128:(p + 1) * 128]
            dxp = jnp.zeros((BLK, 128), F32)
            colsum = []
            for e in range(2):
                dec = _ssd_decay(ap, apt, e, low)
                m = cb * dec
                gpm = jnp.where(first if e == 0 else jnp.logical_not(first), gp, 0.0).astype(BF16)
                dm = _bdot(gpm, xdp, _NT)
                q = dm * m
                colsum.append(jnp.sum(q, axis=1, keepdims=True) - jnp.sum(q.T, axis=1, keepdims=True))
                dcb = dcb + dm * dec
                dxp = dxp + _bdot(m.astype(BF16), gpm, _TN)
            dxd_parts.append(dxp)
            col_parts.append(jnp.where(first, colsum[0], colsum[1]) * (1.0 / SSM_HEAD_DIM))
        cat = (lambda ps: ps[0] if npair == 1 else jnp.concatenate(ps, axis=1))
        dxd = dxd + cat(dxd_parts)
        da_l = da_l + cat(col_parts)
        rows_w = lax.broadcasted_iota(jnp.int32, (BLK, gw), 0)
        da_l = da_l + jnp.where(rows_w == BLK - 1, dalast, 0.0)
        dcbb = dcb.astype(BF16)
        dc_ref[...] = dc + _bdot(dcbb, bm)
        db_ref[...] = db + _bdot(dcbb, cm, _TN)
        tu = (rows <= cols).astype(F32)
        dda = jnp.dot(tu, da_l, precision=HI, preferred_element_type=F32)
        zt_ref[...] = dxd * x + dda * a
        dal_ref[...] += jnp.sum(dda * dt, axis=0, keepdims=True)
        ddl_ref[...] += jnp.sum(g * x, axis=0, keepdims=True)
        dx_ref[...] = dsk_ref[...] * g + dxd * dt

    blk = lambda g, c: (nc - 1 - c, g)
    vec = pl.BlockSpec((1, gw), lambda g, c: (0, g))
    bc_out = pl.BlockSpec((BLK, SSM_STATE), blk)
    return pl.pallas_call(
        body, name=name, grid=(SSM_GROUPS, nc),
        in_specs=[pl.BlockSpec((BLK, gw), blk),
                  pl.BlockSpec((BLK, SSM_STATE), lambda g, c: (nc - 1 - c, bcol + g)),
                  pl.BlockSpec((BLK, SSM_STATE), lambda g, c: (nc - 1 - c, bcol + SSM_GROUPS + g)),
                  pl.BlockSpec((BLK, gw), blk), vec, vec,
                  pl.BlockSpec((1, SSM_STATE, gw), lambda g, c: (nc - 1 - c, 0, g)),
                  pl.BlockSpec((BLK, gw), blk)],
        out_specs=[pl.BlockSpec((BLK, gw), blk), bc_out, bc_out, pl.BlockSpec((BLK, gw), blk), vec, vec],
        out_shape=[jax.ShapeDtypeStruct((s, d), F32), jax.ShapeDtypeStruct((s, gn), F32),
                   jax.ShapeDtypeStruct((s, gn), F32), jax.ShapeDtypeStruct((s, d), F32),
                   jax.ShapeDtypeStruct((1, d), F32), jax.ShapeDtypeStruct((1, d), F32)],
        scratch_shapes=[pltpu.VMEM((SSM_STATE, gw), F32)],
        compiler_params=_cp("parallel", "arbitrary"),
    )(act, act, act, dte, a_lane, d_lane, states, dy)


def _gnorm_fwd(y, proj, norm_w, *, zcol, name):
    s, d = y.shape
    tr = _tile(s, (256, 128))
    gw = d // SSM_GROUPS

    def body(y_ref, z_ref, w_ref, o_ref):
        z = z_ref[...]
        y2 = y_ref[...] * (z * _sigmoid(z))
        for g in range(SSM_GROUPS):
            sl = slice(g * gw, (g + 1) * gw)
            v = y2[:, sl]
            r = lax.rsqrt(jnp.mean(v * v, axis=-1, keepdims=True) + EPS)
            o_ref[:, sl] = (v * r * w_ref[:, sl]).astype(BF16)

    return pl.pallas_call(
        body, name=name, grid=(s // tr,),
        in_specs=[pl.BlockSpec((tr, d), lambda i: (i, 0)), pl.BlockSpec((tr, d), lambda i: (i, zcol)),
                  pl.BlockSpec((1, d), lambda i: (0, 0))],
        out_specs=pl.BlockSpec((tr, d), lambda i: (i, 0)),
        out_shape=jax.ShapeDtypeStruct((s, d), BF16),
        compiler_params=_cp("parallel"),
    )(y, proj, _row(norm_w))


def _gnorm_bwd(y, proj, norm_w, dout, *, zcol, dcol, name):
    s, d = y.shape
    tr = _tile(s, (256, 128))
    gw = d // SSM_GROUPS

    def body(y_ref, z_ref, w_ref, do_ref, dy_ref, dz_ref, dw_ref):
        @pl.when(pl.program_id(0) == 0)
        def _():
            dw_ref[...] = jnp.zeros_like(dw_ref)

        z = z_ref[...]
        yv = y_ref[...]
        sg = _sigmoid(z)
        sz = z * sg
        y2 = yv * sz
        for g in range(SSM_GROUPS):
            sl = slice(g * gw, (g + 1) * gw)
            v = y2[:, sl]
            do = do_ref[:, sl]
            r = lax.rsqrt(jnp.mean(v * v, axis=-1, keepdims=True) + EPS)
            xh = v * r
            gg = do * w_ref[:, sl]
            dy2 = r * (gg - xh * jnp.mean(gg * xh, axis=-1, keepdims=True))
            dw_ref[:, sl] += jnp.sum(do * xh, axis=0, keepdims=True)
            dy_ref[:, sl] = dy2 * sz[:, sl]
            dz_ref[:, sl] = (dy2 * yv[:, sl] * (sg[:, sl] * (1.0 + z[:, sl] * (1.0 - sg[:, sl])))).astype(BF16)

    return pl.pallas_call(
        body, name=name, grid=(s // tr,),
        in_specs=[pl.BlockSpec((tr, d), lambda i: (i, 0)), pl.BlockSpec((tr, d), lambda i: (i, zcol)),
                  pl.BlockSpec((1, d), lambda i: (0, 0)), pl.BlockSpec((tr, d), lambda i: (i, dcol))],
        out_specs=[pl.BlockSpec((tr, d), lambda i: (i, 0)), pl.BlockSpec((tr, d), lambda i: (i, 0)),
                   pl.BlockSpec((1, d), lambda i: (0, 0))],
        out_shape=[jax.ShapeDtypeStruct((s, d), F32), jax.ShapeDtypeStruct((s, d), BF16),
                   jax.ShapeDtypeStruct((1, d), F32)],
        compiler_params=_cp("arbitrary"),
    )(y, proj, _row(norm_w), dout)


def _lanes(v):
    return jnp.repeat(v.astype(F32), SSM_HEAD_DIM).reshape(1, -1)


def _pad128(v):
    return jnp.pad(v.astype(F32).reshape(1, -1), ((0, 0), (0, 128 - v.shape[-1])))


def _even_fwd(h, p, tag):
    d = h.shape[1]
    heads = d // SSM_HEAD_DIM
    hn = _rms_fwd(h, p["norm_mix"], name=tag + "_rms")
    proj = _mm(hn, p["w_in_main"], name=tag + "_in")
    pdt = _mm(hn, p["w_in_dt"], name=tag + "_indt")
    ya = _gm_fwd(proj, p["gm_ln_g"], p["gm_ln_b"], p["gm_ws"], p["gm_bs"], name=tag + "_gm")
    act = _silu_conv_fwd(proj, p["conv_w"], p["conv_b"], col0=3 * d, name=tag + "_conv")
    dte = _dt_fwd(pdt, _pad128(p["dt_bias"]), heads=heads, name=tag + "_dt")
    a = -jnp.exp(p["a_log"].astype(F32))
    y, states = _ssd_fwd(act, dte, _lanes(a), _lanes(p["d_skip"]), name=tag + "_ssd")
    yb = _gnorm_fwd(y, proj, p["ssm_norm_w"], zcol=2, name=tag + "_gn")
    cat = jnp.concatenate([ya, yb], axis=1)
    h1 = _mm(cat, p["w_out"], residual=h, name=tag + "_out")
    return h1, (h, hn, proj, pdt, act, dte, y, states, cat)


def _even_bwd(dh1, p, saved, tag, out_layer=None):
    h, hn, proj, pdt, act, dte, y, states, cat = saved
    d = h.shape[1]
    heads = d // SSM_HEAD_DIM
    a = -jnp.exp(p["a_log"].astype(F32))
    g = {}
    dcat = _mm(dh1, p["w_out"], tb=True, name=tag + "_dcat")
    g["w_out"] = _mm(cat, dh1, ta=True, out_layer=out_layer, name=tag + "_dwout")
    du, dv, dlg, dlb, dws, dbs = _gm_bwd(proj, p["gm_ln_g"], p["gm_ln_b"], p["gm_ws"], p["gm_bs"], dcat, name=tag + "_gmb")
    g["gm_ln_g"] = dlg.reshape(GM_GROUPS, -1)
    g["gm_ln_b"] = dlb.reshape(GM_GROUPS, -1)
    g["gm_ws"] = dws
    g["gm_bs"] = dbs.reshape(GM_GROUPS, BLK)
    dy, dz, dnw = _gnorm_bwd(y, proj, p["ssm_norm_w"], dcat, zcol=2, dcol=1, name=tag + "_gnb")
    g["ssm_norm_w"] = dnw[0]
    dxs, db, dc, zt, dal, ddl = _ssd_bwd(act, dte, _lanes(a), _lanes(p["d_skip"]), states, dy, name=tag + "_ssdb")
    ddt, ddtb, dalog, ddsk = _dt_bwd(pdt, _pad128(p["dt_bias"]), zt, dal, ddl, _pad128(a), heads=heads, name=tag + "_dtb")
    g["dt_bias"] = ddtb[0, :heads]
    g["a_log"] = dalog[0, :heads]
    g["d_skip"] = ddsk[0, :heads]
    dact = jnp.concatenate([dxs, db, dc], axis=1)
    dxbc, dcw, dcb = _silu_conv_bwd(proj, p["conv_w"], p["conv_b"], dact, col0=3 * d, name=tag + "_convb")
    g["conv_w"] = dcw
    g["conv_b"] = dcb[0]
    dproj = jnp.concatenate([du, dv, dz, dxbc], axis=1)
    dhn = _mm(dproj, p["w_in_main"], tb=True, name=tag + "_dhn")
    dhn = _mm(ddt, p["w_in_dt"], tb=True, residual=dhn, name=tag + "_dhn2")
    g["w_in_main"] = _mm(hn, dproj, ta=True, name=tag + "_dwin")
    g["w_in_dt"] = _mm(hn, ddt, ta=True, name=tag + "_dwdt")
    dh, dnm = _rms_bwd(h, p["norm_mix"], dhn, add=dh1, name=tag + "_rmsb")
    g["norm_mix"] = dnm[0]
    return dh, g


def _w_up(p):
    if "w_up_slabs" in p:
        return p["w_up_slabs"][0], {"b_layer": p["w_up_slabs"][1]}
    return p["w_up"], {}


def _ffn_fwd(h, p, tag):
    hn = _rms_fwd(h, p["norm_ffn"], name=tag + "_rms")
    w_up, kw = _w_up(p)
    up = _mm(hn, w_up, name=tag + "_up", **kw)
    a = _ffn_mid_fwd(up, p["conv_w"], p["conv_b"], name=tag + "_mid")
    h2 = _mm(a, p["w_down"], residual=h, name=tag + "_down")
    return h2, (h, hn, up, a)


def _ffn_bwd(dh2, p, saved, tag, g_up_prev=None, out_layer=None):
    h, hn, up, a = saved
    g = {}
    da = _mm(dh2, p["w_down"], tb=True, name=tag + "_da")
    g["w_down"] = _mm(a, dh2, ta=True, out_layer=out_layer, name=tag + "_dwdown")
    dg, dv, dcw, dcb = _ffn_mid_bwd(up, p["conv_w"], p["conv_b"], da, name=tag + "_midb")
    g["conv_w"] = dcw
    g["conv_b"] = dcb[0]
    dup = jnp.concatenate([dg, dv], axis=1)
    w_up, kw = _w_up(p)
    dhn = _mm(dup, w_up, tb=True, name=tag + "_dhn", **kw)
    if kw:
        g["w_up_slabs"] = _mm(hn, dup, ta=True, out_slabs=(g_up_prev, kw["b_layer"], w_up.shape), name=tag + "_dwup")
    else:
        g["w_up"] = _mm(hn, dup, ta=True, name=tag + "_dwup")
    dh, dnw = _rms_bwd(h, p["norm_ffn"], dhn, add=dh2, name=tag + "_rmsb")
    g["norm_ffn"] = dnw[0]
    return dh, g


def _rope(x, cos_p, sin_p):
    half = MLA_ROPE // 2
    lane = lax.broadcasted_iota(jnp.int32, x.shape, 1)
    swapped = jnp.where(lane < half, pltpu.roll(x, 128 - half, 1), pltpu.roll(x, half, 1))
    return x * cos_p + swapped * sin_p


def _rope_t(g, cos_p, sin_p):
    half = MLA_ROPE // 2
    gs = g * sin_p
    lane = lax.broadcasted_iota(jnp.int32, g.shape, 1)
    swapped = jnp.where(lane < half, pltpu.roll(gs, 128 - half, 1), pltpu.roll(gs, half, 1))
    return g * cos_p + swapped


ATTN_SEGMENTS = 4


def _attn_probs(qn_ref, qp_ref, kn_ref, kp_ref, cq_ref, sq_ref, ck_ref, sk_ref, tq, qb0):
    s = kn_ref.shape[0]
    scale = (MLA_NOPE + MLA_ROPE) ** -0.5
    qn = qn_ref[...].astype(BF16)
    qp = _rope(qp_ref[...], cq_ref[...], sq_ref[...]).astype(BF16)
    kn = kn_ref[...].astype(BF16)
    kp = _rope(kp_ref[...], ck_ref[...], sk_ref[...]).astype(BF16)
    sc = (_bdot(qn, kn, _NT) + _bdot(qp, kp, _NT)) * scale
    qpos = (pl.program_id(1) + qb0) * tq + lax.broadcasted_iota(jnp.int32, (tq, s), 0)
    kpos = lax.broadcasted_iota(jnp.int32, (tq, s), 1)
    sc = jnp.where(kpos // CHUNK <= qpos // CHUNK, sc, -jnp.inf)
    sc = sc - jnp.max(sc, axis=-1, keepdims=True)
    e = jnp.exp(sc)
    p = e / jnp.sum(e, axis=-1, keepdims=True)
    return p, qn, qp, kn, kp, scale


def _attn_segments(s, tq):
    nq = s // tq
    nseg = math.gcd(ATTN_SEGMENTS, nq)
    per = nq // nseg
    return [(seg * per, per, (seg + 1) * per * tq) for seg in range(nseg)]


def _attn_in_specs(s, tq, kr_col, qb0):
    return [pl.BlockSpec((tq, 128), lambda h, i: (i + qb0, 2 * h)), pl.BlockSpec((tq, 128), lambda h, i: (i + qb0, 2 * h + 1)),
            pl.BlockSpec((s, 128), lambda h, i: (0, 2 * h)), pl.BlockSpec((s, 128), lambda h, i: (0, kr_col)),
            pl.BlockSpec((tq, 128), lambda h, i: (i + qb0, 0)), pl.BlockSpec((tq, 128), lambda h, i: (i + qb0, 0)),
            pl.BlockSpec((s, 128), lambda h, i: (0, 0)), pl.BlockSpec((s, 128), lambda h, i: (0, 0)),
            pl.BlockSpec((s, 128), lambda h, i: (0, 2 * h + 1))]


def _attn_fwd(q, kv, proj, cos_p, sin_p, *, kr_col, name):
    s = q.shape[0]
    heads = q.shape[1] // 256
    tq = _tile(s, (256, 128))
    o = None
    for seg, (qb0, nqb, keys) in enumerate(_attn_segments(s, tq)):
        def body(qn_ref, qp_ref, kn_ref, kp_ref, cq_ref, sq_ref, ck_ref, sk_ref, v_ref, *rest, qb0=qb0):
            o_ref = rest[-1]
            p = _attn_probs(qn_ref, qp_ref, kn_ref, kp_ref, cq_ref, sq_ref, ck_ref, sk_ref, tq, qb0)[0]
            o_ref[...] = _bdot(p.astype(BF16), v_ref[...].astype(BF16)).astype(BF16)

        in_specs = _attn_in_specs(keys, tq, kr_col, qb0)
        args = [q, q, kv, proj, cos_p, sin_p, cos_p, sin_p, kv]
        aliases = {}
        if o is not None:
            in_specs.append(_ANY)
            args.append(o)
            aliases = {len(args) - 1: 0}
        o = pl.pallas_call(
            body, name=f"{name}{seg}", grid=(heads, nqb),
            in_specs=in_specs,
            out_specs=pl.BlockSpec((tq, 128), lambda h, i, qb0=qb0: (i + qb0, h)),
            out_shape=jax.ShapeDtypeStruct((s, heads * MLA_V), BF16),
            input_output_aliases=aliases,
            compiler_params=_cp("parallel", "parallel"),
        )(*args)
    return o


def _attn_bwd(q, kv, proj, cos_p, sin_p, do, *, kr_col, name):
    s = q.shape[0]
    heads = q.shape[1] // 256
    tq = _tile(s, (256, 128))
    dq = dkv = dkp = None
    for seg, (qb0, nqb, keys) in reversed(list(enumerate(_attn_segments(s, tq)))):
        first = dq is None

        def body(qn_ref, qp_ref, kn_ref, kp_ref, cq_ref, sq_ref, ck_ref, sk_ref, v_ref, do_ref, *rest, qb0=qb0, first=first):
            dq_ref, dkv_ref, dkp_ref = rest[-3:]
            h = pl.program_id(0)
            i = pl.program_id(1)

            @pl.when(i == 0)
            def _():
                dkv_ref[...] = jnp.zeros_like(dkv_ref) if first else rest[1][...]

            @pl.when(jnp.logical_and(h == 0, i == 0))
            def _():
                dkp_ref[...] = jnp.zeros_like(dkp_ref) if first else rest[2][...]

            p, qn, qp, kn, kp, scale = _attn_probs(qn_ref, qp_ref, kn_ref, kp_ref, cq_ref, sq_ref, ck_ref, sk_ref, tq, qb0)
            dob = do_ref[...].astype(BF16)
            pb = p.astype(BF16)
            dv = _bdot(pb, dob, _TN)
            dp = _bdot(dob, v_ref[...].astype(BF16), _NT)
            ds = (p * (dp - jnp.sum(dp * p, axis=-1, keepdims=True)) * scale).astype(BF16)
            dq_ref[:, 0:128] = _bdot(ds, kn).astype(BF16)
            dq_ref[:, 128:256] = _rope_t(_bdot(ds, kp), cq_ref[...], sq_ref[...]).astype(BF16)
            dkv_ref[:, 0:128] += _bdot(ds, qn, _TN)
            dkv_ref[:, 128:256] += dv
            dkp_ref[...] += _rope_t(_bdot(ds, qp, _TN), ck_ref[...], sk_ref[...])

        dkv_spec = pl.BlockSpec((keys, 256), lambda h, i: (0, h))
        dkp_spec = pl.BlockSpec((keys, 128), lambda h, i: (0, 0))
        in_specs = _attn_in_specs(keys, tq, kr_col, qb0) + [pl.BlockSpec((tq, 128), lambda h, i, qb0=qb0: (i + qb0, h))]
        args = [q, q, kv, proj, cos_p, sin_p, cos_p, sin_p, kv, do]
        aliases = {}
        if not first:
            in_specs += [_ANY, dkv_spec, dkp_spec]
            args += [dq, dkv, dkp]
            aliases = {len(args) - 3: 0, len(args) - 2: 1, len(args) - 1: 2}
        dq, dkv, dkp = pl.pallas_call(
            body, name=f"{name}{seg}", grid=(heads, nqb),
            in_specs=in_specs,
            out_specs=[pl.BlockSpec((tq, 256), lambda h, i, qb0=qb0: (i + qb0, h)), dkv_spec, dkp_spec],
            out_shape=[jax.ShapeDtypeStruct((s, heads * 256), BF16), jax.ShapeDtypeStruct((s, heads * 256), F32),
                       jax.ShapeDtypeStruct((s, 128), F32)],
            input_output_aliases=aliases,
            compiler_params=_cp("arbitrary", "arbitrary"),
        )(*args)
    return dq, dkv, dkp


def _rope_tables(positions):
    inv_freq = ROPE_THETA ** (-jnp.arange(0, MLA_ROPE, 2, dtype=F32) / MLA_ROPE)
    ang = positions.astype(F32)[:, None] * inv_freq
    cos, sin = jnp.cos(ang), jnp.sin(ang)
    zero = jnp.zeros((positions.shape[0], 128 - MLA_ROPE), F32)
    return jnp.concatenate([cos, cos, zero], axis=1), jnp.concatenate([-sin, sin, zero], axis=1)


def _odd_fwd(h, p, cos_p, sin_p, tag):
    rank = p["q_norm"].shape[0]
    hn = _rms_fwd(h, p["norm_mix"], name=tag + "_rms")
    proj = _mm(hn, p["w_in"], name=tag + "_in")
    cqn = _rms_fwd(proj, p["q_norm"], width=rank, col=0, name=tag + "_qn")
    ckvn = _rms_fwd(proj, p["kv_norm"], width=rank, col=1, name=tag + "_kvn")
    q = _mm(cqn, p["w_uq"], name=tag + "_uq")
    kv = _mm(ckvn, p["w_ukv"], name=tag + "_ukv")
    o = _attn_fwd(q, kv, proj, cos_p, sin_p, kr_col=2 * rank // 128, name=tag + "_attn")
    h1 = _mm(o, p["w_o"], residual=h, name=tag + "_o")
    return h1, (h, hn, proj, cqn, ckvn, q, kv, o)


def _odd_bwd(dh1, p, cos_p, sin_p, saved, tag, out_layer=None):
    h, hn, proj, cqn, ckvn, q, kv, o = saved
    rank = p["q_norm"].shape[0]
    g = {}
    do = _mm(dh1, p["w_o"], tb=True, name=tag + "_do")
    g["w_o"] = _mm(o, dh1, ta=True, out_layer=out_layer, name=tag + "_dwo")
    dq, dkv, dkp = _attn_bwd(q, kv, proj, cos_p, sin_p, do, kr_col=2 * rank // 128, name=tag + "_attnb")
    dcqn = _mm(dq, p["w_uq"], tb=True, name=tag + "_dcqn")
    g["w_uq"] = _mm(cqn, dq, ta=True, name=tag + "_dwuq")
    dckvn = _mm(dkv, p["w_ukv"], tb=True, name=tag + "_dckvn")
    g["w_ukv"] = _mm(ckvn, dkv, ta=True, name=tag + "_dwukv")
    dcq, dqn = _rms_bwd(proj, p["q_norm"], dcqn, width=rank, col=0, out_dtype=BF16, name=tag + "_qnb")
    dckv, dkvn = _rms_bwd(proj, p["kv_norm"], dckvn, width=rank, col=1, out_dtype=BF16, name=tag + "_kvnb")
    g["q_norm"] = dqn[0]
    g["kv_norm"] = dkvn[0]
    dproj = jnp.concatenate([dcq, dckv, dkp.astype(BF16)], axis=1)
    dhn = _mm(dproj, p["w_in"], tb=True, name=tag + "_dhn")
    g["w_in"] = _mm(hn, dproj, ta=True, name=tag + "_dwin")
    dh, dnm = _rms_bwd(h, p["norm_mix"], dhn, add=dh1, name=tag + "_rmsb")
    g["norm_mix"] = dnm[0]
    return dh, g


PACK_W = 1024
N_CHIPS = 4
_MESH = pl.DeviceIdType.MESH
_ANY = pl.BlockSpec(memory_space=pl.ANY)


def _place():
    x, y, c = lax.axis_index("x"), lax.axis_index("y"), lax.axis_index("c")
    others = [(1 - x, y), (x, 1 - y), (1 - x, 1 - y)]
    return x, y, c, others


def _row_tile(r, c, itemsize):
    for cand in (512, 256, 128, 64, 32, 16, 8):
        if r % cand == 0 and cand * c * itemsize <= 2 * 1024 * 1024:
            return cand
    return r


def _slot_index(slot_axis, slot, layer, i):
    return (slot, layer, i, 0) if slot_axis == 0 else (layer, slot, i, 0)


def _cast_place(w, ck, *, slot_axis, dtype, name):
    nl, r, c = w.shape
    tr = _row_tile(r, c, 4)
    shape = (N_CHIPS, nl, r, c) if slot_axis == 0 else (nl, N_CHIPS, r, c)

    def body(ck_ref, w_ref, o_ref):
        o_ref[...] = w_ref[...].astype(dtype)

    return pl.pallas_call(
        body, name=name,
        grid_spec=pltpu.PrefetchScalarGridSpec(
            num_scalar_prefetch=1, grid=(nl, r // tr),
            in_specs=[pl.BlockSpec((None, tr, c), lambda l, i, s: (l, i, 0))],
            out_specs=pl.BlockSpec((None, None, tr, c), lambda l, i, s: _slot_index(slot_axis, s[1], l, i))),
        out_shape=jax.ShapeDtypeStruct(shape, dtype),
        compiler_params=_cp("parallel", "parallel"),
    )(ck, w)


def _region(ref, slot_axis, slot, half):
    lh = ref.shape[1 - slot_axis] // 2
    if slot_axis == 0:
        return ref.at[slot, pl.ds(half * lh, lh)]
    return ref.at[pl.ds(half * lh, lh), slot]


def _all_gather_multi(bufs, slot_axes, *, name):
    n = len(bufs)

    def body(*refs):
        outs = refs[n:2 * n]
        send_sems, recv_sems = refs[2 * n], refs[2 * n + 1]
        x, y, c, others = _place()
        k = 2 * x + y
        sibling = (x, y, 1 - c)

        def copy(a, slot, half, sem, to):
            blk = _region(outs[a], slot_axes[a], slot, half)
            return pltpu.make_async_remote_copy(src_ref=blk, dst_ref=blk, send_sem=send_sems.at[6 * a + sem],
                                                recv_sem=recv_sems.at[6 * a + sem], device_id=to, device_id_type=_MESH)

        first = [copy(a, k, c, j, (cx, cy, c)) for a in range(n) for j, (cx, cy) in enumerate(others)]
        for cp in first:
            cp.start()
        passed = []
        for a in range(n):
            for j, (cx, cy) in enumerate(others):
                copy(a, 2 * cx + cy, c, j, (cx, cy, c)).wait_recv()
                fw = copy(a, 2 * cx + cy, c, 3 + j, sibling)
                fw.start()
                passed.append(fw)
        for a in range(n):
            for j, (cx, cy) in enumerate(others):
                copy(a, 2 * cx + cy, 1 - c, 3 + j, sibling).wait_recv()
        for cp in first + passed:
            cp.wait_send()

    return pl.pallas_call(
        body, name=name,
        in_specs=[_ANY] * n, out_specs=[_ANY] * n,
        out_shape=[jax.ShapeDtypeStruct(b.shape, b.dtype) for b in bufs],
        input_output_aliases={a: a for a in range(n)},
        scratch_shapes=[pltpu.SemaphoreType.DMA((6 * n,)), pltpu.SemaphoreType.DMA((6 * n,))],
    )(*bufs)


def _half_shape(shape, slot_axis):
    shape = list(shape)
    shape[1 - slot_axis] //= 2
    return tuple(shape)


def _cast_half(g, ck, *, slot_axis, name):
    lh = g.shape[1 - slot_axis] // 2
    r, c = g.shape[2:]
    tr = _row_tile(r, c, 4)

    def body(ck_ref, g_ref, o_ref):
        o_ref[...] = g_ref[...].astype(BF16)

    blk = (None, None, tr, c)
    return pl.pallas_call(
        body, name=name,
        grid_spec=pltpu.PrefetchScalarGridSpec(
            num_scalar_prefetch=1, grid=(lh, r // tr, N_CHIPS),
            in_specs=[pl.BlockSpec(blk, lambda l, i, j, s: _slot_index(slot_axis, j, (1 - s[0]) * lh + l, i))],
            out_specs=pl.BlockSpec(blk, lambda l, i, j, s: _slot_index(slot_axis, j, l, i))),
        out_shape=jax.ShapeDtypeStruct(_half_shape(g.shape, slot_axis), BF16),
        compiler_params=_cp("parallel", "parallel", "parallel"),
    )(ck, g)


def _pair_exchange_multi(halves, *, name):
    n = len(halves)

    def body(*refs):
        h_refs, a_refs = refs[:n], refs[n:2 * n]
        send_sems, recv_sems = refs[2 * n], refs[2 * n + 1]
        x, y, c, _ = _place()
        copies = [pltpu.make_async_remote_copy(
            src_ref=h_refs[a], dst_ref=a_refs[a], send_sem=send_sems.at[a], recv_sem=recv_sems.at[a],
            device_id=(x, y, 1 - c), device_id_type=_MESH) for a in range(n)]
        for cp in copies:
            cp.start()
        for cp in copies:
            cp.wait()

    return pl.pallas_call(
        body, name=name, in_specs=[_ANY] * n, out_specs=[_ANY] * n,
        out_shape=[jax.ShapeDtypeStruct(h.shape, h.dtype) for h in halves],
        scratch_shapes=[pltpu.SemaphoreType.DMA((n,)), pltpu.SemaphoreType.DMA((n,))],
    )(*halves)


def _pair_add(g, a, ck, *, slot_axis, name):
    lh = a.shape[1 - slot_axis]
    r, c = a.shape[2:]
    tr = _row_tile(r, c, 4)

    def body(ck_ref, g_ref, a_ref, t_ref, own_ref):
        v = g_ref[...] + a_ref[...].astype(F32)
        t_ref[...] = v.astype(BF16)

        @pl.when(pl.program_id(2) == ck_ref[1])
        def _():
            own_ref[...] = v

    blk = (None, None, tr, c)
    return pl.pallas_call(
        body, name=name,
        grid_spec=pltpu.PrefetchScalarGridSpec(
            num_scalar_prefetch=1, grid=(lh, r // tr, N_CHIPS),
            in_specs=[pl.BlockSpec(blk, lambda l, i, j, s: _slot_index(slot_axis, j, s[0] * lh + l, i)),
                      pl.BlockSpec(blk, lambda l, i, j, s: _slot_index(slot_axis, j, l, i))],
            out_specs=[pl.BlockSpec(blk, lambda l, i, j, s: _slot_index(slot_axis, j, l, i)),
                       pl.BlockSpec((None, tr, c), lambda l, i, j, s: (l, i, 0))]),
        out_shape=[jax.ShapeDtypeStruct(a.shape, BF16), jax.ShapeDtypeStruct((lh, r, c), F32)],
        compiler_params=_cp("arbitrary", "arbitrary", "arbitrary"),
    )(ck, g, a)


def _chip_exchange_multi(ts, slot_axes, *, name):
    n = len(ts)

    def body(*refs):
        t_refs, b_refs = refs[:n], refs[n:2 * n]
        send_sems, recv_sems = refs[2 * n], refs[2 * n + 1]
        x, y, c, others = _place()
        copies = []
        for a in range(n):
            for j, (cx, cy) in enumerate(others):
                src = t_refs[a].at[2 * cx + cy] if slot_axes[a] == 0 else t_refs[a].at[:, 2 * cx + cy]
                copies.append(pltpu.make_async_remote_copy(
                    src_ref=src, dst_ref=b_refs[a].at[j], send_sem=send_sems.at[3 * a + j], recv_sem=recv_sems.at[3 * a + j],
                    device_id=(cx, cy, c), device_id_type=_MESH))
        for cp in copies:
            cp.start()
        for cp in copies:
            cp.wait()

    def out_shape(t, ax):
        lh = t.shape[1 - ax]
        return jax.ShapeDtypeStruct((N_CHIPS - 1, lh) + t.shape[2:], t.dtype)

    return pl.pallas_call(
        body, name=name, in_specs=[_ANY] * n, out_specs=[_ANY] * n,
        out_shape=[out_shape(t, ax) for t, ax in zip(ts, slot_axes)],
        scratch_shapes=[pltpu.SemaphoreType.DMA((3 * n,)), pltpu.SemaphoreType.DMA((3 * n,))],
    )(*ts)


def _chip_add(own, b, ck, *, name):
    lh, r, c = own.shape
    tr = _row_tile(r, c, 4)

    def body(ck_ref, o_ref, b_ref, r_ref):
        acc = o_ref[...]
        for j in range(N_CHIPS - 1):
            acc = acc + b_ref[j].astype(F32)
        r_ref[...] = acc

    return pl.pallas_call(
        body, name=name,
        grid_spec=pltpu.PrefetchScalarGridSpec(
            num_scalar_prefetch=1, grid=(lh, r // tr),
            in_specs=[pl.BlockSpec((None, tr, c), lambda l, i, s: (l, i, 0)),
                      pl.BlockSpec((N_CHIPS - 1, None, tr, c), lambda l, i, s: (0, l, i, 0))],
            out_specs=pl.BlockSpec((None, tr, c), lambda l, i, s: (s[0] * lh + l, i, 0))),
        out_shape=jax.ShapeDtypeStruct((2 * lh, r, c), F32),
        compiler_params=_cp("parallel", "parallel"),
    )(ck, own, b)


def _pair_share_multi(finals, *, name):
    n = len(finals)

    def body(*refs):
        outs = refs[n:2 * n]
        send_sems, recv_sems = refs[2 * n], refs[2 * n + 1]
        x, y, c, _ = _place()

        def copy(a, half):
            lh = outs[a].shape[0] // 2
            blk = outs[a].at[pl.ds(half * lh, lh)]
            return pltpu.make_async_remote_copy(src_ref=blk, dst_ref=blk, send_sem=send_sems.at[a], recv_sem=recv_sems.at[a],
                                                device_id=(x, y, 1 - c), device_id_type=_MESH)

        sends = [copy(a, c) for a in range(n)]
        for cp in sends:
            cp.start()
        for a in range(n):
            copy(a, 1 - c).wait_recv()
        for cp in sends:
            cp.wait_send()

    return pl.pallas_call(
        body, name=name, in_specs=[_ANY] * n, out_specs=[_ANY] * n,
        out_shape=[jax.ShapeDtypeStruct(f.shape, f.dtype) for f in finals],
        input_output_aliases={a: a for a in range(n)},
        scratch_shapes=[pltpu.SemaphoreType.DMA((n,)), pltpu.SemaphoreType.DMA((n,))],
    )(*finals)


def _col_pieces(cs, segments):
    pieces = []
    for k in range(N_CHIPS):
        for gs, ge, oi, ds in segments:
            lo, hi = max(k * cs, gs), min((k + 1) * cs, ge)
            if lo < hi:
                pieces.append((k, lo - k * cs, oi, ds + lo - gs, hi - lo))
    return pieces


def _assemble(f, layer, pieces, widths, *, name):
    _, _, r, c = f.shape
    tr = _row_tile(r, max(max(widths), N_CHIPS * c), f.dtype.itemsize)
    covered = sum(p[4] for p in pieces) == sum(widths)

    def body(f_ref, *o_refs):
        if not covered:
            for o in o_refs:
                o[...] = jnp.zeros_like(o)
        for k, s0, oi, d0, wd in pieces:
            o_refs[oi][:, d0:d0 + wd] = f_ref[k, :, s0:s0 + wd]

    return pl.pallas_call(
        body, name=name, grid=(r // tr,),
        in_specs=[pl.BlockSpec((N_CHIPS, None, tr, c), lambda i: (0, layer, i, 0))],
        out_specs=[pl.BlockSpec((tr, w), lambda i: (i, 0)) for w in widths],
        out_shape=[jax.ShapeDtypeStruct((r, w), f.dtype) for w in widths],
        compiler_params=_cp("parallel"),
    )(f)


def _split(fulls, layer, pieces, g_prev, shape, *, name):
    _, _, r, c = shape
    widths = [t.shape[1] for t in fulls]
    tr = _row_tile(r, max(max(widths), N_CHIPS * c), 4)
    nf = len(fulls)

    def body(*refs):
        g_ref = refs[-1]
        for k, s0, oi, d0, wd in pieces:
            g_ref[k, :, s0:s0 + wd] = refs[oi][:, d0:d0 + wd]

    in_specs = [pl.BlockSpec((tr, w), lambda i: (i, 0)) for w in widths]
    args = list(fulls)
    aliases = {}
    if g_prev is not None:
        in_specs.append(_ANY)
        args.append(g_prev)
        aliases = {nf: 0}
    return pl.pallas_call(
        body, name=name, grid=(r // tr,),
        in_specs=in_specs,
        out_specs=pl.BlockSpec((N_CHIPS, None, tr, c), lambda i: (0, layer, i, 0)),
        out_shape=jax.ShapeDtypeStruct(shape, F32),
        input_output_aliases=aliases,
        compiler_params=_cp("parallel"),
    )(*args)


def _all_reduce_small(v, *, name):
    r, w = v.shape
    n_dev = 8

    def body(x_ref, sum_ref, out_ref, send_sems, recv_sems, local_sem):
        x, y, c, others = _place()
        me, sibling = (x, y, c), (x, y, 1 - c)

        def rows(px, py, pc):
            return out_ref.at[pl.ds((4 * px + 2 * py + pc) * r, r), :]

        def copy(k, block, to, src=None):
            return pltpu.make_async_remote_copy(
                src_ref=rows(*block) if src is None else src, dst_ref=rows(*block),
                send_sem=send_sems.at[k], recv_sem=recv_sems.at[k], device_id=to, device_id_type=_MESH)

        mine = pltpu.make_async_copy(x_ref, rows(*me), local_sem)
        mine.start()
        first = [copy(0, me, sibling, src=x_ref)]
        first += [copy(1 + j, me, (*chip, c), src=x_ref) for j, chip in enumerate(others)]
        for cp in first:
            cp.start()
        passed = [copy(4 + j, (*chip, c), sibling) for j, chip in enumerate(others)]
        for j, chip in enumerate(others):
            copy(1 + j, (*chip, c), me).wait_recv()
            passed[j].start()
        copy(0, sibling, me).wait_recv()
        for j, chip in enumerate(others):
            copy(4 + j, (*chip, 1 - c), me).wait_recv()
        for cp in first + passed:
            cp.wait_send()
        mine.wait()
        acc = out_ref[pl.ds(0, r), :]
        for dev in range(1, n_dev):
            acc = acc + out_ref[pl.ds(dev * r, r), :]
        sum_ref[...] = acc

    vmem = pl.BlockSpec(memory_space=pltpu.VMEM)
    return pl.pallas_call(
        body, name=name, in_specs=[vmem], out_specs=[vmem, vmem],
        out_shape=[jax.ShapeDtypeStruct((r, w), F32), jax.ShapeDtypeStruct((n_dev * r, w), F32)],
        scratch_shapes=[pltpu.SemaphoreType.DMA((7,)), pltpu.SemaphoreType.DMA((7,)), pltpu.SemaphoreType.DMA],
        compiler_params=pltpu.CompilerParams(vmem_limit_bytes=V7X_VMEM_LIMIT),
    )(v)[0]


def _adamw(w, g, m, v, *, name):
    shape = w.shape
    cols = shape[-1]
    rows = max(1, math.prod(shape[:-1]))
    tr = rows
    for cand in (512, 256, 128, 64, 32, 16, 8):
        if rows % cand == 0 and cand * cols * 4 <= 2 * 1024 * 1024:
            tr = cand
            break
    c1 = 1.0 - ADAM_B1 ** ADAM_STEP
    c2 = 1.0 - ADAM_B2 ** ADAM_STEP

    def body(w_ref, g_ref, m_ref, v_ref, go_ref, d_ref, mo_ref, vo_ref):
        gv = g_ref[...]
        go_ref[...] = gv
        mn = ADAM_B1 * m_ref[...] + (1.0 - ADAM_B1) * gv
        vn = ADAM_B2 * v_ref[...] + (1.0 - ADAM_B2) * (gv * gv)
        d_ref[...] = -ADAM_LR * ((mn / c1) / (jnp.sqrt(vn / c2) + ADAM_EPS) + ADAM_WD * w_ref[...])
        mo_ref[...] = mn
        vo_ref[...] = vn

    spec = pl.BlockSpec((tr, cols), lambda i: (i, 0))
    outs = pl.pallas_call(
        body, name=name, grid=(rows // tr,),
        in_specs=[spec] * 4, out_specs=[spec] * 4,
        out_shape=[jax.ShapeDtypeStruct((rows, cols), F32)] * 4,
        compiler_params=_cp("parallel"),
    )(*[t.reshape(rows, cols) for t in (w, g, m, v)])
    return [o.reshape(shape) for o in outs]


_WEIGHTS = ["norm_mix", "norm_ffn", "norm_final", "ev_w_in", "ev_gm_ln_g", "ev_gm_ln_b", "ev_gm_ws", "ev_gm_bs",
            "ev_conv_w", "ev_conv_b", "ev_dt_bias", "ev_a_log", "ev_d_skip", "ev_ssm_norm_w", "ev_w_out", "od_w_in",
            "od_q_norm", "od_kv_norm", "od_w_uq", "od_w_ukv", "od_w_o", "ff_w_up", "ff_conv_w", "ff_conv_b", "ff_w_down"]
_BIG = {"ev_w_in": -1, "ev_w_out": -2, "od_w_in": -2, "od_w_uq": -1, "od_w_ukv": -1, "od_w_o": -2,
        "ff_w_up": -1, "ff_w_down": -2}
_SMALL = {"ev_gm_ln_g": -1, "ev_gm_ln_b": -1, "ev_conv_w": -1, "od_q_norm": -1, "od_kv_norm": -1, "ff_conv_w": -1}
_SHARDED = {**_BIG, **_SMALL}
_REPLICATED = [n for n in _WEIGHTS if n not in _SHARDED]
N_CHUNKS = 4


def _from_slabs(slabs, axis):
    t = jnp.moveaxis(slabs, 0, axis - 1)
    shape = list(t.shape)
    if axis == -1:
        return t.reshape(shape[:-2] + [shape[-2] * shape[-1]])
    return t.reshape(shape[:-3] + [shape[-3] * shape[-2], shape[-1]])


def _to_slabs(full, axis):
    shape = list(full.shape)
    if axis == -1:
        t = full.reshape(shape[:-1] + [N_CHIPS, shape[-1] // N_CHIPS])
    else:
        t = full.reshape(shape[:-2] + [N_CHIPS, shape[-2] // N_CHIPS, shape[-1]])
    return jnp.moveaxis(t, axis - 1, 0)


def _reduce_scatter(gs, slot_axes, names, ck):
    halves = [_cast_half(g, ck, slot_axis=ax, name="rs_ch_" + n) for g, ax, n in zip(gs, slot_axes, names)]
    theirs = _pair_exchange_multi(halves, name="rs_px")
    ts, owns = [], []
    for g, a, ax, n in zip(gs, theirs, slot_axes, names):
        t, own = _pair_add(g, a, ck, slot_axis=ax, name="rs_pa_" + n)
        ts.append(t)
        owns.append(own)
    bs = _chip_exchange_multi(ts, slot_axes, name="rs_cx")
    finals = [_chip_add(own, b, ck, name="rs_ca_" + n) for own, b, n in zip(owns, bs, names)]
    return _pair_share_multi(finals, name="rs_ps")


def _all_reduce(arrs, tag):
    n = sum(a.size for a in arrs)
    rows = -(-n // PACK_W)
    rows = -(-rows // 8) * 8
    flat = jnp.concatenate([a.astype(F32).reshape(-1) for a in arrs])
    flat = jnp.pad(flat, (0, rows * PACK_W - n)).reshape(rows, PACK_W)
    tot = _all_reduce_small(flat, name=tag).reshape(-1)
    res, off = [], 0
    for a in arrs:
        res.append(tot[off:off + a.size].reshape(a.shape))
        off += a.size
    return res


def _pad_cols(w, cols):
    return jnp.pad(w, ((0, 0), (0, cols - w.shape[1])))


def _as3(a):
    return a.reshape(a.shape[0], 1, a.shape[1]) if a.ndim == 2 else a


def _col_layout(name, f):
    cs = f.shape[3]
    total = N_CHIPS * cs
    if name == "ev_w_in":
        main = 4 * f.shape[2] + 2 * SSM_GROUPS * SSM_STATE
        return _col_pieces(cs, [(0, main, 0, 0), (main, total, 1, 0)]), [main, 128]
    if name == "od_w_uq":
        qk = MLA_NOPE + MLA_ROPE
        heads = total // qk
        return _col_pieces(cs, [(hd * qk, (hd + 1) * qk, 0, hd * 256) for hd in range(heads)]), [heads * 256]
    return _col_pieces(cs, [(0, total, 0, 0)]), [total]


def _rows_of(buf, j):
    return buf[j].reshape(N_CHIPS * buf.shape[2], buf.shape[3])


def _layer_params(full, gathered, layer):
    j = layer // 2
    tag = f"asm{layer}_"
    p = {"norm_mix": full["norm_mix"][layer]}
    if layer % 2 == 0:
        f_in = gathered["ev_w_in"]
        w_main, w_dt = _assemble(f_in, j, *_col_layout("ev_w_in", f_in), name=tag + "in")
        p.update(w_in_main=w_main, w_in_dt=w_dt,
                 gm_ln_g=full["ev_gm_ln_g"][j], gm_ln_b=full["ev_gm_ln_b"][j], gm_ws=full["ev_gm_ws"][j],
                 gm_bs=full["ev_gm_bs"][j], conv_w=full["ev_conv_w"][j], conv_b=full["ev_conv_b"][j],
                 dt_bias=full["ev_dt_bias"][j], a_log=full["ev_a_log"][j], d_skip=full["ev_d_skip"][j],
                 ssm_norm_w=full["ev_ssm_norm_w"][j], w_out=_rows_of(gathered["ev_w_out"], j))
    else:
        f_uq, f_ukv = gathered["od_w_uq"], gathered["od_w_ukv"]
        w_in = _rows_of(gathered["od_w_in"], j)
        p.update(w_in=_pad_cols(w_in, -(-w_in.shape[1] // 128) * 128), q_norm=full["od_q_norm"][j],
                 kv_norm=full["od_kv_norm"][j],
                 w_uq=_assemble(f_uq, j, *_col_layout("od_w_uq", f_uq), name=tag + "uq")[0],
                 w_ukv=_assemble(f_ukv, j, *_col_layout("od_w_ukv", f_ukv), name=tag + "ukv")[0],
                 w_o=_rows_of(gathered["od_w_o"], j))
    f = {"norm_ffn": full["norm_ffn"][layer], "w_up_slabs": (gathered["ff_w_up"], layer),
         "conv_w": full["ff_conv_w"][layer], "conv_b": full["ff_conv_b"][layer],
         "w_down": _rows_of(gathered["ff_w_down"], layer)}
    return p, f


def _small_grads(g, gf, layer):
    out = {"norm_mix": g["norm_mix"], "norm_ffn": gf["norm_ffn"], "ff_conv_w": gf["conv_w"], "ff_conv_b": gf["conv_b"]}
    if layer % 2 == 0:
        out.update(ev_gm_ln_g=g["gm_ln_g"], ev_gm_ln_b=g["gm_ln_b"], ev_gm_ws=g["gm_ws"], ev_gm_bs=g["gm_bs"],
                   ev_conv_w=g["conv_w"], ev_conv_b=g["conv_b"], ev_dt_bias=g["dt_bias"], ev_a_log=g["a_log"],
                   ev_d_skip=g["d_skip"], ev_ssm_norm_w=g["ssm_norm_w"])
    else:
        out.update(od_q_norm=g["q_norm"], od_kv_norm=g["kv_norm"])
    return out


def _step(x, positions, loss_target, w, m, v):
    depth = w["norm_mix"].shape[0]
    h = x[0]
    tgt = loss_target[0]
    cos_p, sin_p = _rope_tables(positions[0])
    ck = jnp.stack([lax.axis_index("c"), 2 * lax.axis_index("x") + lax.axis_index("y")]).astype(jnp.int32)

    sharded = list(_SHARDED)
    slot_axes = [1 if _SHARDED[n] == -2 else 0 for n in sharded]
    bufs = [_cast_place(_as3(w[n]), ck, slot_axis=ax, dtype=BF16 if n in _BIG else F32, name="place_" + n)
            for n, ax in zip(sharded, slot_axes)]
    gathered = dict(zip(sharded, _all_gather_multi(bufs, slot_axes, name="ag")))
    full = {n: w[n] for n in _REPLICATED}
    for n in _SMALL:
        full[n] = _from_slabs(gathered[n], -1).reshape(w[n].shape[:-1] + (N_CHIPS * w[n].shape[-1],))

    params, saved = [], []
    for layer in range(depth):
        p, f = _layer_params(full, gathered, layer)
        if layer % 2 == 0:
            h, sv = _even_fwd(h, p, f"l{layer}m")
        else:
            h, sv = _odd_fwd(h, p, cos_p, sin_p, f"l{layer}m")
        h, svf = _ffn_fwd(h, f, f"l{layer}f")
        params.append((p, f))
        saved.append((sv, svf))

    loss, dh, dnf = _loss_bwd(h, w["norm_final"], tgt, name="loss")
    per_layer = []
    col_buf, row_buf = {}, {}
    row_parts = {"od_w_in": {}}

    def split(name, fulls, j):
        f = gathered[name]
        col_buf[name] = _split(fulls, j, _col_layout(name, f)[0], col_buf.get(name), f.shape, name=f"split_{name}{j}")

    for layer in reversed(range(depth)):
        p, f = params[layer]
        sv, svf = saved[layer]
        j = layer // 2
        dh, gf = _ffn_bwd(dh, f, svf, f"l{layer}fb", col_buf.get("ff_w_up"), (row_buf.get("ff_w_down"), layer, depth))
        col_buf["ff_w_up"] = gf["w_up_slabs"]
        row_buf["ff_w_down"] = gf["w_down"]
        if layer % 2 == 0:
            dh, g = _even_bwd(dh, p, sv, f"l{layer}mb", (row_buf.get("ev_w_out"), j, w["ev_w_out"].shape[0]))
            split("ev_w_in", [g["w_in_main"], g["w_in_dt"]], j)
            row_buf["ev_w_out"] = g["w_out"]
        else:
            dh, g = _odd_bwd(dh, p, cos_p, sin_p, sv, f"l{layer}mb", (row_buf.get("od_w_o"), j, w["od_w_o"].shape[0]))
            split("od_w_uq", [g["w_uq"]], j)
            split("od_w_ukv", [g["w_ukv"]], j)
            row_parts["od_w_in"][j] = g["w_in"][:, :w["od_w_in"].shape[2]]
            row_buf["od_w_o"] = g["w_o"]
        per_layer.append((layer, _small_grads(g, gf, layer)))
    per_layer.sort(key=lambda t: t[0])
    local = {"norm_final": dnf[0]}
    for n in list(_SMALL) + _REPLICATED:
        if n != "norm_final":
            local[n] = jnp.stack([lg[n] for _, lg in per_layer if n in lg], axis=0)

    gs = []
    for n in sharded:
        if n in col_buf:
            gs.append(col_buf[n])
        elif n in row_buf:
            gs.append(row_buf[n].reshape(gathered[n].shape))
        elif n in row_parts:
            parts = row_parts[n]
            gs.append(jnp.stack([parts[i] for i in range(len(parts))], axis=0).reshape(gathered[n].shape))
        else:
            gs.append(_to_slabs(_as3(local[n]), -1))
    grads = dict(zip(sharded, _reduce_scatter(gs, slot_axes, sharded, ck)))
    grads.update(zip(_REPLICATED, _all_reduce([local[n] for n in _REPLICATED], "ar")))
    loss = lax.psum(loss[0, 0], ("x", "y", "c"))

    delta, new_m, new_v = {}, {}, {}
    for n in _WEIGHTS:
        grads[n] = grads[n].reshape(w[n].shape)
        grads[n], delta[n], new_m[n], new_v[n] = _adamw(w[n], grads[n], m[n], v[n], name="adamw_" + n)
    return (loss, dh[None], *[grads[n] for n in _WEIGHTS], *[delta[n] for n in _WEIGHTS],
            *[new_m[n] for n in _WEIGHTS], *[new_v[n] for n in _WEIGHTS])


def kernel(x, positions, norm_mix, norm_ffn, norm_final, ev_w_in, ev_gm_ln_g, ev_gm_ln_b, ev_gm_ws, ev_gm_bs, ev_conv_w, ev_conv_b, ev_dt_bias, ev_a_log, ev_d_skip, ev_ssm_norm_w, ev_w_out, od_w_in, od_q_norm, od_kv_norm, od_w_uq, od_w_ukv, od_w_o, ff_w_up, ff_conv_w, ff_conv_b, ff_w_down, loss_target, m_norm_mix, m_norm_ffn, m_norm_final, m_ev_w_in, m_ev_gm_ln_g, m_ev_gm_ln_b, m_ev_gm_ws, m_ev_gm_bs, m_ev_conv_w, m_ev_conv_b, m_ev_dt_bias, m_ev_a_log, m_ev_d_skip, m_ev_ssm_norm_w, m_ev_w_out, m_od_w_in, m_od_q_norm, m_od_kv_norm, m_od_w_uq, m_od_w_ukv, m_od_w_o, m_ff_w_up, m_ff_conv_w, m_ff_conv_b, m_ff_w_down, v_norm_mix, v_norm_ffn, v_norm_final, v_ev_w_in, v_ev_gm_ln_g, v_ev_gm_ln_b, v_ev_gm_ws, v_ev_gm_bs, v_ev_conv_w, v_ev_conv_b, v_ev_dt_bias, v_ev_a_log, v_ev_d_skip, v_ev_ssm_norm_w, v_ev_w_out, v_od_w_in, v_od_q_norm, v_od_kv_norm, v_od_w_uq, v_od_w_ukv, v_od_w_o, v_ff_w_up, v_ff_conv_w, v_ff_conv_b, v_ff_w_down):
    ws = (norm_mix, norm_ffn, norm_final, ev_w_in, ev_gm_ln_g, ev_gm_ln_b, ev_gm_ws, ev_gm_bs, ev_conv_w, ev_conv_b, ev_dt_bias, ev_a_log, ev_d_skip, ev_ssm_norm_w, ev_w_out, od_w_in, od_q_norm, od_kv_norm, od_w_uq, od_w_ukv, od_w_o, ff_w_up, ff_conv_w, ff_conv_b, ff_w_down)
    ms = (m_norm_mix, m_norm_ffn, m_norm_final, m_ev_w_in, m_ev_gm_ln_g, m_ev_gm_ln_b, m_ev_gm_ws, m_ev_gm_bs, m_ev_conv_w, m_ev_conv_b, m_ev_dt_bias, m_ev_a_log, m_ev_d_skip, m_ev_ssm_norm_w, m_ev_w_out, m_od_w_in, m_od_q_norm, m_od_kv_norm, m_od_w_uq, m_od_w_ukv, m_od_w_o, m_ff_w_up, m_ff_conv_w, m_ff_conv_b, m_ff_w_down)
    vs = (v_norm_mix, v_norm_ffn, v_norm_final, v_ev_w_in, v_ev_gm_ln_g, v_ev_gm_ln_b, v_ev_gm_ws, v_ev_gm_bs, v_ev_conv_w, v_ev_conv_b, v_ev_dt_bias, v_ev_a_log, v_ev_d_skip, v_ev_ssm_norm_w, v_ev_w_out, v_od_w_in, v_od_q_norm, v_od_kv_norm, v_od_w_uq, v_od_w_ukv, v_od_w_o, v_ff_w_up, v_ff_conv_w, v_ff_conv_b, v_ff_w_down)
    return _step(x, positions, loss_target, dict(zip(_WEIGHTS, ws)), dict(zip(_WEIGHTS, ms)), dict(zip(_WEIGHTS, vs)))
```
